```python
import jax, jax.numpy as jnp
from jax import lax
import numpy as np

D_MODEL = 1024
BATCH = 8
SEQ = 8192
DEPTH = 4

GRID_W = 64
CTX_LEN = 256
N_EVEN = (DEPTH + 1) // 2
N_ODD = DEPTH // 2
EPS = 1e-6

GLA_HEADS = 4
GLA_DK = 64
GLA_DV = 128
GLA_GATE_RANK = 16
GLA_TAU = 16.0
GLA_CHUNK = 64

MLA_HEADS = 4
MLA_Q_RANK = 256
MLA_KV_RANK = 128
MLA_NOPE = 128
MLA_ROPE = 64
MLA_V = 128
ROPE_BASE = 10000.0
Q_BLOCK = 128

FOURIER_GROUPS = 4
FOURIER_GW = D_MODEL // FOURIER_GROUPS

D_FF = 2816
N_EXPERTS = 8
TOP_K = 2
D_FF_EXPERT = 3584

IN_SPLITS = (GLA_HEADS * GLA_DK, GLA_HEADS * GLA_DK, GLA_HEADS * GLA_DV, GLA_GATE_RANK, GLA_GATE_RANK,
             GLA_HEADS * GLA_DV, MLA_Q_RANK, MLA_KV_RANK, MLA_ROPE)
D_IN = (2 * GLA_HEADS * GLA_DK + 2 * GLA_HEADS * GLA_DV + 2 * GLA_GATE_RANK
        + MLA_Q_RANK + MLA_KV_RANK + MLA_ROPE)
MIX_WIDTH = GLA_HEADS * GLA_DV + MLA_HEADS * MLA_V

kernel_name = "hybrid_gla_mla_fnet_moe_dit"


def rmsnorm(x, g):
    xf = x.astype(jnp.float32)
    y = xf * lax.rsqrt(jnp.mean(xf * xf, axis=-1, keepdims=True) + EPS)
    return (y * g.astype(jnp.float32)).astype(x.dtype)


def adaln(cond, w, b):
    m = jax.nn.silu(cond) @ w + b
    return jnp.split(m, 6, axis=-1)


def modulate(h, shift, scale):
    return h * (1.0 + scale[:, None, :]) + shift[:, None, :]


def _flip(t):
    return jnp.flip(t, axis=1)


def axial_rope_tables(n_tok):
    rows = n_tok // GRID_W
    row = jnp.broadcast_to(jnp.arange(rows, dtype=jnp.float32)[:, None], (rows, GRID_W)).reshape(-1)
    col = jnp.broadcast_to(jnp.arange(GRID_W, dtype=jnp.float32)[None, :], (rows, GRID_W)).reshape(-1)
    half = MLA_ROPE // 2
    inv = 1.0 / (ROPE_BASE ** (jnp.arange(0, half, 2, dtype=jnp.float32) / half))
    ang = jnp.concatenate([row[:, None] * inv, col[:, None] * inv], axis=-1)
    return jnp.cos(ang), jnp.sin(ang)


def _rotate(u, cos, sin):
    u1, u2 = jnp.split(u, 2, axis=-1)
    return jnp.concatenate([u1 * cos - u2 * sin, u2 * cos + u1 * sin], axis=-1)


def apply_axial_rope(x, cos, sin):
    xf = x.astype(jnp.float32)
    c = cos[None, :, None, :]
    s = sin[None, :, None, :]
    nf = cos.shape[-1] // 2
    xr, xcol = jnp.split(xf, 2, axis=-1)
    out = jnp.concatenate([_rotate(xr, c[..., :nf], s[..., :nf]),
                           _rotate(xcol, c[..., nf:], s[..., nf:])], axis=-1)
    return out.astype(x.dtype)


def split_columns(p):
    out, start = [], 0
    for w in IN_SPLITS:
        out.append(p[..., start:start + w])
        start += w
    return out


def gla_inputs(q, k, v, zf, zb, w_gate_f, b_gate_f, w_gate_b, b_gate_b):
    B, L, _ = q.shape
    def heads(t, d):
        return t.astype(jnp.float32).reshape(B, L, GLA_HEADS, d)
    def log_alpha(z, w, b):
        return (jax.nn.log_sigmoid((z @ w + b).astype(jnp.float32)) / GLA_TAU).reshape(B, L, GLA_HEADS, GLA_DK)
    return (heads(q, GLA_DK) * GLA_DK ** -0.5, heads(k, GLA_DK), heads(v, GLA_DV),
            log_alpha(zf, w_gate_f, b_gate_f), log_alpha(zb, w_gate_b, b_gate_b))


def gla_states(k, v, log_a, s0):
    B, L, H, dk = k.shape
    n = L // GLA_CHUNK
    kc = k.reshape(B, n, GLA_CHUNK, H, dk)
    vc = v.reshape(B, n, GLA_CHUNK, H, -1)
    bcum = jnp.cumsum(log_a.reshape(B, n, GLA_CHUNK, H, dk), axis=2)
    b_last = bcum[:, :, -1]
    d_state = jnp.einsum('bnchk,bnchv->bnhkv', kc * jnp.exp(b_last[:, :, None] - bcum), vc)
    def step(S, inp):
        decay, ds = inp
        return decay[..., None] * S + ds, S
    s_fin, s_prev = lax.scan(step, s0, (jnp.moveaxis(jnp.exp(b_last), 1, 0), jnp.moveaxis(d_state, 1, 0)))
    return jnp.moveaxis(s_prev, 0, 1), s_fin, bcum


def gla_output(q, k, v, bcum, s_prev):
    B, L, H, dk = q.shape
    n = L // GLA_CHUNK
    qc = q.reshape(B, n, GLA_CHUNK, H, dk) * jnp.exp(bcum)
    kc = k.reshape(B, n, GLA_CHUNK, H, dk) * jnp.exp(-bcum)
    vc = v.reshape(B, n, GLA_CHUNK, H, -1)
    att = jnp.einsum('bnihk,bnjhk->bnhij', qc, kc)
    att = jnp.where(jnp.tril(jnp.ones((GLA_CHUNK, GLA_CHUNK), dtype=bool)), att, 0.0)
    o = jnp.einsum('bnhij,bnjhv->bnihv', att, vc) + jnp.einsum('bnihk,bnhkv->bnihv', qc, s_prev)
    return o.reshape(B, L, H, -1)


def gla_scan(q, k, v, log_a, s0):
    s_prev, _, bcum = gla_states(k, v, log_a, s0)
    return gla_output(q, k, v, bcum, s_prev)


def gla_finish(o, r, g):
    B, L = o.shape[:2]
    return rmsnorm(o, g).reshape(B, L, -1).astype(r.dtype) * jax.nn.silu(r)


def mla_q(cq, q_norm, w_uq):
    B, L, _ = cq.shape
    q = (rmsnorm(cq, q_norm) @ w_uq).reshape(B, L, MLA_HEADS, MLA_NOPE + MLA_ROPE)
    return q[..., :MLA_NOPE], q[..., MLA_NOPE:]


def mla_kv(ckv, kv_norm, w_ukv):
    B, L, _ = ckv.shape
    kv = (rmsnorm(ckv, kv_norm) @ w_ukv).reshape(B, L, MLA_HEADS, MLA_NOPE + MLA_V)
    return kv[..., :MLA_NOPE], kv[..., MLA_NOPE:]


def mla_attend(q_nope, q_rope, k_nope, k_rope, v):
    B, S, H, _ = q_nope.shape
    nb = S // Q_BLOCK
    scale = (MLA_NOPE + MLA_ROPE) ** -0.5
    def block(args):
        qn, qr = args
        s = jnp.einsum('bqhd,bkhd->bhqk', qn, k_nope) + jnp.einsum('bqhr,bkr->bhqk', qr, k_rope)
        p = jax.nn.softmax(s.astype(jnp.float32) * scale, axis=-1).astype(v.dtype)
        return jnp.einsum('bhqk,bkhd->bqhd', p, v)
    def to_blocks(t):
        return jnp.moveaxis(t.reshape(B, nb, Q_BLOCK, *t.shape[2:]), 1, 0)
    o = lax.map(block, (to_blocks(q_nope), to_blocks(q_rope)))
    return jnp.moveaxis(o, 0, 1).reshape(B, S, H * MLA_V)


def even_mixer(hx, hc, ctx_full, cos, sin, w_in, w_gate_f, b_gate_f, w_gate_b, b_gate_b,
               gla_norm, q_norm, w_uq, kv_norm, w_ukv, w_o):
    gq, gk, gv, gzf, gzb, gr, cq, ckv, kr = split_columns(hx @ w_in)
    cgq, cgk, cgv, cgzf, cgzb, cgr, ccq, cckv, ckr = split_columns(hc @ w_in)
    gate_w = (w_gate_f, b_gate_f, w_gate_b, b_gate_b)
    qx, kx, vx, lafx, labx = gla_inputs(gq, gk, gv, gzf, gzb, *gate_w)
    qc, kc, vc, lafc, labc = gla_inputs(cgq, cgk, cgv, cgzf, cgzb, *gate_w)
    s0 = jnp.zeros((hc.shape[0], GLA_HEADS, GLA_DK, GLA_DV), jnp.float32)
    spf, sff, bcf = gla_states(kc, vc, lafc, s0)
    spb, sfb, bcb = gla_states(_flip(kc), _flip(vc), _flip(labc), s0)
    o_x = gla_scan(qx, kx, vx, lafx, sff) + _flip(gla_scan(_flip(qx), _flip(kx), _flip(vx), _flip(labx), sfb))
    knc, vmc = mla_kv(cckv, kv_norm, w_ukv)
    knx, vmx = mla_kv(ckv, kv_norm, w_ukv)
    qnx, qrx = mla_q(cq, q_norm, w_uq)
    qrx = apply_axial_rope(qrx, cos, sin)
    krx = apply_axial_rope(kr[:, :, None, :], cos, sin)[:, :, 0, :]
    a_x = mla_attend(qnx, qrx, jnp.concatenate([knc, knx], axis=1),
                     jnp.concatenate([ckr, krx], axis=1), jnp.concatenate([vmc, vmx], axis=1))
    y_x = jnp.concatenate([gla_finish(o_x, gr, gla_norm), a_x], axis=-1) @ w_o
    if not ctx_full:
        return y_x, None
    o_c = gla_output(qc, kc, vc, bcf, spf) + _flip(gla_output(_flip(qc), _flip(kc), _flip(vc), bcb, spb))
    qnc, qrc = mla_q(ccq, q_norm, w_uq)
    a_c = mla_attend(qnc, qrc, knc, ckr, vmc)
    y_c = jnp.concatenate([gla_finish(o_c, cgr, gla_norm), a_c], axis=-1) @ w_o
    return y_x, y_c


def fourier_mix(h, w_o):
    B, L, D = h.shape
    hg = h.astype(jnp.float32).reshape(B, L, FOURIER_GROUPS, D // FOURIER_GROUPS)
    f = jnp.fft.fft2(hg, axes=(1, 3), norm="ortho").real
    return f.reshape(B, L, D).astype(h.dtype) @ w_o


def swiglu(h, w1, w3, w2):
    return (jax.nn.silu(h @ w1) * (h @ w3)) @ w2


def moe_swiglu(t, w_router, w1, w3, w2):
    logits = (t @ w_router).astype(jnp.float32)
    top_v, top_i = lax.top_k(logits, TOP_K)
    gates = jax.nn.softmax(top_v, axis=-1).astype(t.dtype)
    e_flat = top_i.reshape(-1)
    order = jnp.argsort(e_flat)
    tok = order // TOP_K
    xs = t[tok]
    sizes = jnp.bincount(e_flat, length=N_EXPERTS).astype(jnp.int32)
    h = jax.nn.silu(lax.ragged_dot(xs, w1, sizes)) * lax.ragged_dot(xs, w3, sizes)
    y = lax.ragged_dot(h, w2, sizes) * gates.reshape(-1)[order][:, None]
    return jnp.zeros_like(t).at[tok].add(y)


def setup_inputs(seed: int = 0) -> dict:
    key = jax.random.key(seed)
    ks = iter(jax.random.split(key, 40))
    f32 = jnp.float32
    D = D_MODEL
    def nrm(shape, fan_in, g=1.0):
        return (g * fan_in ** -0.5) * jax.random.normal(next(ks), shape, f32)
    def gain(shape):
        return 1.0 + 0.05 * jax.random.normal(next(ks), shape, f32)
    def bias(shape, s):
        return s * jax.random.normal(next(ks), shape, f32)
    return {
        "x": jax.random.normal(next(ks), (BATCH, SEQ, D), f32),
        "c": jax.random.normal(next(ks), (BATCH, D), f32),
        "ctx": jax.random.normal(next(ks), (BATCH, CTX_LEN, D), f32),
        "c_ctx": jax.random.normal(next(ks), (D,), f32),
        "w_mod": nrm((DEPTH, D, 6 * D), D, 0.5),
        "b_mod": bias((DEPTH, 6 * D), 0.02),
        "g_mix_pre": gain((DEPTH, D)),
        "g_mix_post": gain((DEPTH, D)),
        "g_ffn_pre": gain((DEPTH, D)),
        "g_ffn_post": gain((DEPTH, D)),
        "e_w_in": nrm((N_EVEN, D, D_IN), D),
        "e_w_gate_f": nrm((N_EVEN, GLA_GATE_RANK, GLA_HEADS * GLA_DK), GLA_GATE_RANK),
        "e_b_gate_f": bias((N_EVEN, GLA_HEADS * GLA_DK), 0.1),
        "e_w_gate_b": nrm((N_EVEN, GLA_GATE_RANK, GLA_HEADS * GLA_DK), GLA_GATE_RANK),
        "e_b_gate_b": bias((N_EVEN, GLA_HEADS * GLA_DK), 0.1),
        "e_gla_norm": gain((N_EVEN, GLA_DV)),
        "e_q_norm": gain((N_EVEN, MLA_Q_RANK)),
        "e_w_uq": nrm((N_EVEN, MLA_Q_RANK, MLA_HEADS * (MLA_NOPE + MLA_ROPE)), MLA_Q_RANK),
        "e_kv_norm": gain((N_EVEN, MLA_KV_RANK)),
        "e_w_ukv": nrm((N_EVEN, MLA_KV_RANK, MLA_HEADS * (MLA_NOPE + MLA_V)), MLA_KV_RANK),
        "e_w_o": nrm((N_EVEN, MIX_WIDTH, D), MIX_WIDTH),
        "e_w1": nrm((N_EVEN, D, D_FF), D),
        "e_w3": nrm((N_EVEN, D, D_FF), D),
        "e_w2": nrm((N_EVEN, D_FF, D), D_FF),
        "o_w_o": nrm((N_ODD, D, D), D),
        "o_w_router": nrm((N_ODD, D, N_EXPERTS), D),
        "o_w1": nrm((N_ODD, N_EXPERTS, D, D_FF_EXPERT), D),
        "o_w3": nrm((N_ODD, N_EXPERTS, D, D_FF_EXPERT), D),
        "o_w2": nrm((N_ODD, N_EXPERTS, D_FF_EXPERT, D), D_FF_EXPERT),
    }


def reference(x, c, ctx, c_ctx, w_mod, b_mod, g_mix_pre, g_mix_post, g_ffn_pre, g_ffn_post,
              e_w_in, e_w_gate_f, e_b_gate_f, e_w_gate_b, e_b_gate_b, e_gla_norm,
              e_q_norm, e_w_uq, e_kv_norm, e_w_ukv, e_w_o, e_w1, e_w3, e_w2,
              o_w_o, o_w_router, o_w1, o_w3, o_w2):
    cos, sin = axial_rope_tables(x.shape[1])
    last_read = 2 * ((DEPTH - 1) // 2)
    xc = ctx
    hc = None
    for i in range(DEPTH):
        j = i // 2
        even = (i % 2 == 0)
        ctx_live = i <= last_read
        ctx_full = i < last_read

        def channel_mixer(h):
            if even:
                return swiglu(h, e_w1[j], e_w3[j], e_w2[j])
            return lax.map(lambda t: moe_swiglu(t, o_w_router[j], o_w1[j], o_w3[j], o_w2[j]), h)

        sh_m, sc_m, ga_m, sh_f, sc_f, ga_f = adaln(c, w_mod[i], b_mod[i])
        hx = modulate(rmsnorm(x, g_mix_pre[i]), sh_m, sc_m)
        if ctx_live:
            csh_m, csc_m, cga_m, csh_f, csc_f, cga_f = adaln(c_ctx[None, :], w_mod[i], b_mod[i])
            hc = modulate(rmsnorm(xc, g_mix_pre[i]), csh_m, csc_m)
        if even:
            y, yc = even_mixer(hx, hc, ctx_full, cos, sin, e_w_in[j], e_w_gate_f[j], e_b_gate_f[j],
                               e_w_gate_b[j], e_b_gate_b[j], e_gla_norm[j], e_q_norm[j], e_w_uq[j],
                               e_kv_norm[j], e_w_ukv[j], e_w_o[j])
        else:
            y = fourier_mix(hx, o_w_o[j])
            yc = fourier_mix(hc, o_w_o[j]) if ctx_full else None
        x = x + ga_m[:, None, :] * rmsnorm(y, g_mix_post[i])
        hx = modulate(rmsnorm(x, g_ffn_pre[i]), sh_f, sc_f)
        x = x + ga_f[:, None, :] * rmsnorm(channel_mixer(hx), g_ffn_post[i])
        if ctx_full:
            xc = xc + cga_m[:, None, :] * rmsnorm(yc, g_mix_post[i])
            hcf = modulate(rmsnorm(xc, g_ffn_pre[i]), csh_f, csc_f)
            xc = xc + cga_f[:, None, :] * rmsnorm(channel_mixer(hcf), g_ffn_post[i])
    return x
```

```python
import functools

import numpy as np
import jax
import jax.numpy as jnp
from jax import lax
from jax.experimental import pallas as pl
from jax.experimental.pallas import tpu as pltpu

F32 = jnp.float32
BF16 = jnp.bfloat16

EPS = 1e-6
D_MODEL = 1024
DEPTH = 4
GRID_W = 64
GLA_HEADS = 4
GLA_DK = 64
GLA_DV = 128
GLA_RANK = 16
GLA_TAU = 16.0
GLA_CHUNK = 64
MLA_HEADS = 4
MLA_Q_RANK = 256
MLA_KV_RANK = 128
MLA_NOPE = 128
MLA_ROPE = 64
MLA_V = 128
MLA_QK = MLA_NOPE + MLA_ROPE
ROPE_BASE = 10000.0
FOURIER_GROUPS = 4
FOURIER_GW = D_MODEL // FOURIER_GROUPS
N_EXPERTS = 8
LANES = 128
VMEM_LIMIT = 48 * 1024 * 1024


def _cparams(sem):
    return pltpu.CompilerParams(dimension_semantics=sem, vmem_limit_bytes=VMEM_LIMIT)


def _dot(a, b):
    return jnp.dot(a, b, preferred_element_type=F32)


def _dot_nt(a, b):
    return lax.dot_general(a, b, (((1,), (1,)), ((), ())), preferred_element_type=F32)


def _dot_tn(a, b):
    return lax.dot_general(a, b, (((0,), (0,)), ((), ())), preferred_element_type=F32)


def _split(x):
    hi = x.astype(BF16)
    lo = (x - hi.astype(F32)).astype(BF16)
    return hi, lo


def _dot3(a, b_hi, b_lo):
    a_hi, a_lo = _split(a)
    return _dot(a_hi, b_hi) + _dot(a_lo, b_hi) + _dot(a_hi, b_lo)


def _rms(x, g):
    return x * lax.rsqrt(jnp.mean(x * x, axis=-1, keepdims=True) + EPS) * g


def _silu(x):
    return x / (1.0 + jnp.exp(-x))


def _const_spec(shape):
    nd = len(shape)
    return pl.BlockSpec(shape, lambda *_: (0,) * nd)


def _batch_row_spec(d):
    return pl.BlockSpec((1, 1, d), lambda b, *_: (b, 0, 0))


def _adaln_kernel(c_ref, w_ref, b_ref, o_ref):
    a = _silu(c_ref[...])
    w_hi, w_lo = _split(w_ref[0])
    o_ref[0] = _dot3(a, w_hi, w_lo) + b_ref[0]


def _adaln(cond, w_mod, b_mod):
    depth, d, n = w_mod.shape
    rows = cond.shape[0]
    bn = 1536
    return pl.pallas_call(
        _adaln_kernel,
        grid=(depth, n // bn),
        in_specs=[
            pl.BlockSpec((rows, d), lambda i, j: (0, 0)),
            pl.BlockSpec((1, d, bn), lambda i, j: (i, 0, j)),
            pl.BlockSpec((1, 1, bn), lambda i, j: (i, 0, j)),
        ],
        out_specs=pl.BlockSpec((1, rows, bn), lambda i, j: (i, 0, j)),
        out_shape=jax.ShapeDtypeStruct((depth, rows, n), F32),
        compiler_params=_cparams(("arbitrary", "arbitrary")),
        name="adaln",
    )(cond, w_mod, b_mod.reshape(depth, 1, n))


def _in_proj_kernel(x_ref, g_ref, sh_ref, sc_ref, wp_ref, wkr_ref, wgh_ref, wgl_ref, bg_ref,
                    qn_ref, wuq_ref, kvn_ref, wuk_ref, wuv_ref, cq_ref, sq_ref, ck_ref, sk_ref,
                    gq_ref, gk_ref, gv_ref, gr_ref, laf_ref, lab_ref, q_ref, kt_ref, v_ref):
    h = _rms(x_ref[0], g_ref[...]) * (1.0 + sc_ref[0]) + sh_ref[0]
    hb = h.astype(BF16)
    p = _dot(hb, wp_ref[...])
    nqk = GLA_HEADS * GLA_DK
    nv = GLA_HEADS * GLA_DV
    gq_ref[0] = p[:, 0:nqk] * (GLA_DK ** -0.5)
    gk_ref[0] = p[:, nqk:2 * nqk]
    gv_ref[0] = p[:, 2 * nqk:2 * nqk + nv]
    gr_ref[0] = p[:, 2 * nqk + nv:2 * nqk + 2 * nv]
    o = 2 * nqk + 2 * nv
    cq = p[:, o:o + MLA_Q_RANK]
    ckv = p[:, o + MLA_Q_RANK:o + MLA_Q_RANK + MLA_KV_RANK]
    tail = p[:, o + MLA_Q_RANK + MLA_KV_RANK:]
    pre = _dot3(tail, wgh_ref[...], wgl_ref[...]) + bg_ref[...]
    la = (jnp.minimum(pre, 0.0) - jnp.log(1.0 + jnp.exp(-jnp.abs(pre)))) * (1.0 / GLA_TAU)
    laf_ref[0] = la[:, :nqk]
    lab_ref[0] = la[:, nqk:]
    q = _dot(_rms(cq, qn_ref[...]).astype(BF16), wuq_ref[...])
    att_scale = MLA_QK ** -0.5
    nn = MLA_HEADS * MLA_NOPE
    nr = MLA_HEADS * MLA_ROPE
    q_rope = q[:, nn:nn + nr] * cq_ref[...] + q[:, nn + nr:] * sq_ref[...]
    for hd in range(MLA_HEADS):
        q_ref[0, hd, :, 0:MLA_NOPE] = (q[:, hd * MLA_NOPE:(hd + 1) * MLA_NOPE] * att_scale).astype(BF16)
        q_ref[0, hd, :, MLA_NOPE:MLA_QK] = (q_rope[:, hd * MLA_ROPE:(hd + 1) * MLA_ROPE] * att_scale).astype(BF16)
    ckvn = _rms(ckv, kvn_ref[...]).astype(BF16)
    kt = _dot_nt(wuk_ref[...], ckvn)
    v = _dot(ckvn, wuv_ref[...])
    kr2 = _dot_nt(wkr_ref[...], hb)
    kr = (kr2[:MLA_ROPE] * ck_ref[...] + kr2[MLA_ROPE:] * sk_ref[...]).astype(BF16)
    for hd in range(MLA_HEADS):
        kt_ref[0, hd, 0:MLA_NOPE, :] = kt[hd * MLA_NOPE:(hd + 1) * MLA_NOPE].astype(BF16)
        kt_ref[0, hd, MLA_NOPE:MLA_QK, :] = kr
        v_ref[0, hd] = v[:, hd * MLA_V:(hd + 1) * MLA_V].astype(BF16)


def _in_proj(tok, g, sh, sc, w, tabs, tm):
    b, t, d = tok.shape
    cq, sq, ck, sk = tabs
    nqk = GLA_HEADS * GLA_DK
    nv = GLA_HEADS * GLA_DV
    row = lambda n: pl.BlockSpec((1, tm, n), lambda bi, ti: (bi, ti, 0))
    weights = (w["wp"], w["wkr"], w["wg_hi"], w["wg_lo"], w["bg"], w["qn"], w["wuq"], w["kvn"],
               w["wuk"], w["wuv"])
    return pl.pallas_call(
        _in_proj_kernel,
        grid=(b, t // tm),
        in_specs=[row(d), _const_spec((1, d)), _batch_row_spec(d), _batch_row_spec(d)]
        + [_const_spec(a.shape) for a in weights]
        + [pl.BlockSpec((tm, MLA_HEADS * MLA_ROPE), lambda bi, ti: (ti, 0)),
           pl.BlockSpec((tm, MLA_HEADS * MLA_ROPE), lambda bi, ti: (ti, 0)),
           pl.BlockSpec((MLA_ROPE, tm), lambda bi, ti: (0, ti)),
           pl.BlockSpec((MLA_ROPE, tm), lambda bi, ti: (0, ti))],
        out_specs=[row(nqk), row(nqk), row(nv), row(nv), row(nqk), row(nqk),
                   pl.BlockSpec((1, MLA_HEADS, tm, MLA_QK), lambda bi, ti: (bi, 0, ti, 0)),
                   pl.BlockSpec((1, MLA_HEADS, MLA_QK, tm), lambda bi, ti: (bi, 0, 0, ti)),
                   pl.BlockSpec((1, MLA_HEADS, tm, MLA_V), lambda bi, ti: (bi, 0, ti, 0))],
        out_shape=[jax.ShapeDtypeStruct((b, t, nqk), F32), jax.ShapeDtypeStruct((b, t, nqk), F32),
                   jax.ShapeDtypeStruct((b, t, nv), F32), jax.ShapeDtypeStruct((b, t, nv), F32),
                   jax.ShapeDtypeStruct((b, t, nqk), F32), jax.ShapeDtypeStruct((b, t, nqk), F32),
                   jax.ShapeDtypeStruct((b, MLA_HEADS, t, MLA_QK), BF16),
                   jax.ShapeDtypeStruct((b, MLA_HEADS, MLA_QK, t), BF16),
                   jax.ShapeDtypeStruct((b, MLA_HEADS, t, MLA_V), BF16)],
        compiler_params=_cparams(("parallel", "parallel")),
        name="even_in_proj",
    )(tok, g, sh, sc, *weights, cq, sq, ck, sk)


def _gla_chunk(q_ref, k_ref, v_ref, l_ref, o_ref, st_ref, row0, tri, mask, last_row):
    c = GLA_CHUNK
    la = l_ref[0, pl.ds(row0, c), :]
    la_hi, la_lo = _split(la)
    bc = _dot(tri, la_hi) + _dot(tri, la_lo)
    bl = bc[last_row:last_row + 1, :]
    q = q_ref[0, pl.ds(row0, c), :]
    k = k_ref[0, pl.ds(row0, c), :]
    qc = (q * jnp.exp(bc)).astype(BF16)
    kc = (k * jnp.exp(-bc)).astype(BF16)
    kd = (k * jnp.exp(bl - bc)).astype(BF16)
    vb = v_ref[0, pl.ds(row0, c), :].astype(BF16)
    st = st_ref[...]
    stb = st.astype(BF16)
    outs, upds = [], []
    for hd in range(GLA_HEADS):
        ks = slice(hd * GLA_DK, (hd + 1) * GLA_DK)
        vs = slice(hd * GLA_DV, (hd + 1) * GLA_DV)
        att = jnp.where(mask, _dot_nt(qc[:, ks], kc[:, ks]), 0.0).astype(BF16)
        outs.append(_dot(att, vb[:, vs]) + _dot_nt(qc[:, ks], stb[:, ks]))
        upds.append(_dot_tn(vb[:, vs], kd[:, ks]))
    o_ref[0, pl.ds(row0, c), :] = jnp.concatenate(outs, axis=1)
    st_ref[...] = st * jnp.exp(bl) + jnp.concatenate(upds, axis=1)


def _gla_kernel(qf, kf, vf, lf, qb, kb, vb, lb, s0f, s0b, of, ob, sff, sfb, stf, stb, *, nc):
    j = pl.program_id(1)

    @pl.when(j == 0)
    def _():
        stf[...] = s0f[0]
        stb[...] = s0b[0]

    c = GLA_CHUNK
    r = lax.broadcasted_iota(jnp.int32, (c, c), 0)
    cc = lax.broadcasted_iota(jnp.int32, (c, c), 1)
    lower = r >= cc
    upper = r <= cc
    tri_l = jnp.where(lower, 1.0, 0.0).astype(BF16)
    tri_u = jnp.where(upper, 1.0, 0.0).astype(BF16)

    def body(ci, carry):
        rf = pl.multiple_of(ci * c, c)
        rb = pl.multiple_of((nc - 1 - ci) * c, c)
        _gla_chunk(qf, kf, vf, lf, of, stf, rf, tri_l, lower, c - 1)
        _gla_chunk(qb, kb, vb, lb, ob, stb, rb, tri_u, upper, 0)
        return carry

    lax.fori_loop(0, nc, body, 0)

    @pl.when(j == pl.num_programs(1) - 1)
    def _():
        sff[0] = stf[...]
        sfb[0] = stb[...]


def _gla(gq, gk, gv, laf, lab, s0f, s0b, tb):
    b, t, nqk = gq.shape
    nv = gv.shape[-1]
    nblk = t // tb
    fwd = lambda n: pl.BlockSpec((1, tb, n), lambda bi, j: (bi, j, 0))
    bwd = lambda n: pl.BlockSpec((1, tb, n), lambda bi, j: (bi, nblk - 1 - j, 0))
    st = pl.BlockSpec((1, GLA_DV, nqk), lambda bi, j: (bi, 0, 0))
    return pl.pallas_call(
        functools.partial(_gla_kernel, nc=tb // GLA_CHUNK),
        grid=(b, nblk),
        in_specs=[fwd(nqk), fwd(nqk), fwd(nv), fwd(nqk), bwd(nqk), bwd(nqk), bwd(nv), bwd(nqk), st, st],
        out_specs=[fwd(nv), bwd(nv), st, st],
        out_shape=[jax.ShapeDtypeStruct((b, t, nv), F32), jax.ShapeDtypeStruct((b, t, nv), F32),
                   jax.ShapeDtypeStruct((b, GLA_DV, nqk), F32), jax.ShapeDtypeStruct((b, GLA_DV, nqk), F32)],
        scratch_shapes=[pltpu.VMEM((GLA_DV, nqk), F32), pltpu.VMEM((GLA_DV, nqk), F32)],
        compiler_params=_cparams(("parallel", "arbitrary")),
        name="gla_scan",
    )(gq, gk, gv, laf, gq, gk, gv, lab, s0f, s0b)


def _attn_kernel(*refs, n_src):
    q_ref = refs[0]
    kts = refs[1:1 + 2 * n_src:2]
    vs = refs[2:2 + 2 * n_src:2]
    o_ref = refs[1 + 2 * n_src]
    q = q_ref[0, 0]
    ss = [_dot(q, kt[0, 0]) for kt in kts]
    m = ss[0].max(axis=-1, keepdims=True)
    for s in ss[1:]:
        m = jnp.maximum(m, s.max(axis=-1, keepdims=True))
    ps = [jnp.exp(s - m) for s in ss]
    l = ps[0].sum(axis=-1, keepdims=True)
    for p in ps[1:]:
        l = l + p.sum(axis=-1, keepdims=True)
    o = _dot(ps[0].astype(BF16), vs[0][0, 0])
    for p, v in zip(ps[1:], vs[1:]):
        o = o + _dot(p.astype(BF16), v[0, 0])
    o_ref[0] = (o / l).astype(o_ref.dtype)


def _attention(q, srcs, bq):
    b, nh, t, dqk = q.shape
    in_specs = [pl.BlockSpec((1, 1, bq, dqk), lambda bi, hi, qi: (bi, hi, qi, 0))]
    args = [q]
    for kt, v in srcs:
        tk = kt.shape[-1]
        in_specs.append(pl.BlockSpec((1, 1, dqk, tk), lambda bi, hi, qi: (bi, hi, 0, 0)))
        in_specs.append(pl.BlockSpec((1, 1, tk, MLA_V), lambda bi, hi, qi: (bi, hi, 0, 0)))
        args += [kt, v]
    return pl.pallas_call(
        functools.partial(_attn_kernel, n_src=len(srcs)),
        grid=(b, nh, t // bq),
        in_specs=in_specs,
        out_specs=pl.BlockSpec((1, bq, MLA_V), lambda bi, hi, qi: (bi, qi, hi)),
        out_shape=jax.ShapeDtypeStruct((b, t, nh * MLA_V), BF16),
        compiler_params=_cparams(("parallel", "parallel", "arbitrary")),
        name="mla_attention",
    )(*args)


def _even_out_kernel(x_ref, of_ref, ob_ref, gr_ref, a_ref, gn_ref, wo1_ref, wo2_ref, gpost_ref, ga_ref,
                     gpre_ref, sh_ref, sc_ref, xo_ref, h_ref):
    o = of_ref[0] + ob_ref[0]
    parts = [_rms(o[:, hd * GLA_DV:(hd + 1) * GLA_DV], gn_ref[...]) for hd in range(GLA_HEADS)]
    fin = jnp.concatenate(parts, axis=1) * _silu(gr_ref[0])
    y = _dot(fin.astype(BF16), wo1_ref[...]) + _dot(a_ref[0], wo2_ref[...])
    xn = x_ref[0] + ga_ref[0] * _rms(y, gpost_ref[...])
    xo_ref[0] = xn
    h_ref[0] = (_rms(xn, gpre_ref[...]) * (1.0 + sc_ref[0]) + sh_ref[0]).astype(BF16)


def _even_out(tok, o_f, o_b, gr, a, gn, wo1, wo2, gpost, ga, gpre, sh, sc, tm):
    b, t, d = tok.shape
    nv = o_f.shape[-1]
    row = lambda n: pl.BlockSpec((1, tm, n), lambda bi, ti: (bi, ti, 0))
    return pl.pallas_call(
        _even_out_kernel,
        grid=(b, t // tm),
        in_specs=[row(d), row(nv), row(nv), row(nv), row(a.shape[-1]), _const_spec(gn.shape),
                  _const_spec(wo1.shape), _const_spec(wo2.shape), _const_spec((1, d)), _batch_row_spec(d),
                  _const_spec((1, d)), _batch_row_spec(d), _batch_row_spec(d)],
        out_specs=[row(d), row(d)],
        out_shape=[jax.ShapeDtypeStruct((b, t, d), F32), jax.ShapeDtypeStruct((b, t, d), BF16)],
        compiler_params=_cparams(("parallel", "parallel")),
        name="even_out_proj",
    )(tok, o_f, o_b, gr, a, gn, wo1, wo2, gpost, ga, gpre, sh, sc)


def _ffn_kernel(x_ref, h_ref, w1_ref, w3_ref, w2_ref, g_ref, ga_ref, xo_ref, acc_ref):
    f = pl.program_id(2)
    h = h_ref[0]
    a = _dot(h, w1_ref[...])
    mid = (_silu(a) * _dot(h, w3_ref[...])).astype(BF16)
    contrib = _dot(mid, w2_ref[...])

    @pl.when(f == 0)
    def _():
        acc_ref[...] = contrib

    @pl.when(f > 0)
    def _():
        acc_ref[...] += contrib

    @pl.when(f == pl.num_programs(2) - 1)
    def _():
        xo_ref[0] = x_ref[0] + ga_ref[0] * _rms(acc_ref[...], g_ref[...])


def _ffn(tok, h, w1, w3, w2, g, ga, tm, fc):
    b, t, d = tok.shape
    dff = w1.shape[-1]
    row = lambda: pl.BlockSpec((1, tm, d), lambda bi, ti, fi: (bi, ti, 0))
    return pl.pallas_call(
        _ffn_kernel,
        grid=(b, t // tm, dff // fc),
        in_specs=[row(), row(),
                  pl.BlockSpec((d, fc), lambda bi, ti, fi: (0, fi)),
                  pl.BlockSpec((d, fc), lambda bi, ti, fi: (0, fi)),
                  pl.BlockSpec((fc, d), lambda bi, ti, fi: (fi, 0)),
                  _const_spec((1, d)), _batch_row_spec(d)],
        out_specs=row(),
        out_shape=jax.ShapeDtypeStruct((b, t, d), F32),
        scratch_shapes=[pltpu.VMEM((tm, d), F32)],
        compiler_params=_cparams(("parallel", "parallel", "arbitrary")),
        name="swiglu_ffn",
    )(tok, h, w1, w3, w2, g, ga)


def _dft_cos_sin(n):
    idx = (np.arange(n)[:, None] * np.arange(n)[None, :]) % n
    ang = 2.0 * np.pi * idx.astype(np.float64) / n
    return np.cos(ang), np.sin(ang)


def _fourier1_kernel(x_ref, g_ref, sh_ref, sc_ref, f1_ref, tc_ref, ts_ref, o_ref, *, nb, n1):
    d = D_MODEL
    for j in range(nb):
        h = _rms(x_ref[0, :, j * d:(j + 1) * d], g_ref[...]) * (1.0 + sc_ref[0]) + sh_ref[0]
        a = _dot(f1_ref[...], h.astype(BF16))
        ar, ai = a[:n1], a[n1:]
        tc = jnp.concatenate([tc_ref[j]] * (d // LANES), axis=1)
        ts = jnp.concatenate([ts_ref[j]] * (d // LANES), axis=1)
        o_ref[0, 0, j] = (ar * tc - ai * ts).astype(BF16)
        o_ref[0, 1, j] = (ar * ts + ai * tc).astype(BF16)


def _fourier2_kernel(b_ref, x_ref, f2_ref, cc_ref, sc_ref, wo_ref, gpost_ref, ga_ref, xo_ref, *, kb, n2):
    d = D_MODEL
    u = _dot(f2_ref[...], b_ref[0])
    ur = jnp.concatenate([u[:n2, j * d:(j + 1) * d] for j in range(kb)], axis=0).astype(BF16)
    ui = jnp.concatenate([u[n2:, j * d:(j + 1) * d] for j in range(kb)], axis=0).astype(BF16)
    gw = FOURIER_GW
    f = jnp.concatenate(
        [_dot(ur[:, g * gw:(g + 1) * gw], cc_ref[...]) + _dot(ui[:, g * gw:(g + 1) * gw], sc_ref[...])
         for g in range(FOURIER_GROUPS)], axis=1)
    y = _dot(f.astype(BF16), wo_ref[...])
    yn = ga_ref[0] * _rms(y, gpost_ref[...])
    for j in range(kb):
        xo_ref[0, :, j * d:(j + 1) * d] = x_ref[0, :, j * d:(j + 1) * d] + yn[j * n2:(j + 1) * n2]


def _fourier_x(tok, g, sh, sc, wo, gpost, ga):
    b, t, d = tok.shape
    n1, n2 = 128, t // 128
    nb, kb = 2, 8
    c1, s1 = _dft_cos_sin(n1)
    c2, s2 = _dft_cos_sin(n2)
    cg, sg = _dft_cos_sin(FOURIER_GW)
    f1 = jnp.asarray(np.concatenate([c1, -s1], axis=0) / np.sqrt(n1), BF16)
    f2 = jnp.asarray(np.block([[c2, s2], [-s2, c2]]) / np.sqrt(n2), BF16)
    ccg = jnp.asarray(cg / np.sqrt(FOURIER_GW), BF16)
    scg = jnp.asarray(sg / np.sqrt(FOURIER_GW), BF16)
    tw = 2.0 * np.pi * ((np.arange(n2)[:, None] * np.arange(n1)[None, :]) % t).astype(np.float64) / t
    tc = jnp.asarray(np.broadcast_to(np.cos(tw)[:, :, None], (n2, n1, LANES)), F32)
    ts = jnp.asarray(np.broadcast_to(-np.sin(tw)[:, :, None], (n2, n1, LANES)), F32)
    stage1 = pl.pallas_call(
        functools.partial(_fourier1_kernel, nb=nb, n1=n1),
        grid=(b, n2 // nb),
        in_specs=[pl.BlockSpec((1, n1, nb * d), lambda bi, ji: (bi, 0, ji)),
                  _const_spec((1, d)), _batch_row_spec(d), _batch_row_spec(d),
                  _const_spec(f1.shape),
                  pl.BlockSpec((nb, n1, LANES), lambda bi, ji: (ji, 0, 0)),
                  pl.BlockSpec((nb, n1, LANES), lambda bi, ji: (ji, 0, 0))],
        out_specs=pl.BlockSpec((1, 2, nb, n1, d), lambda bi, ji: (bi, 0, ji, 0, 0)),
        out_shape=jax.ShapeDtypeStruct((b, 2, n2, n1, d), BF16),
        compiler_params=_cparams(("parallel", "parallel")),
        name="fourier_stage1",
    )(tok.reshape(b, n1, n2 * d), g, sh, sc, f1, tc, ts)
    out = pl.pallas_call(
        functools.partial(_fourier2_kernel, kb=kb, n2=n2),
        grid=(b, n1 // kb),
        in_specs=[pl.BlockSpec((1, 2 * n2, kb * d), lambda bi, ki: (bi, 0, ki)),
                  pl.BlockSpec((1, n2, kb * d), lambda bi, ki: (bi, 0, ki)),
                  _const_spec(f2.shape), _const_spec(ccg.shape), _const_spec(scg.shape),
                  _const_spec(wo.shape), _const_spec((1, d)), _batch_row_spec(d)],
        out_specs=pl.BlockSpec((1, n2, kb * d), lambda bi, ki: (bi, 0, ki)),
        out_shape=jax.ShapeDtypeStruct((b, n2, n1 * d), F32),
        compiler_params=_cparams(("parallel", "parallel")),
        name="fourier_stage2",
    )(stage1.reshape(b, 2 * n2, n1 * d), tok.reshape(b, n2, n1 * d), f2, ccg, scg, wo, gpost, ga)
    return out.reshape(b, t, d)


def _fourier_ctx_kernel(x_ref, g_ref, sh_ref, sc_ref, fl_ref, cc_ref, scg_ref, wo_ref, gpost_ref, ga_ref,
                        xo_ref):
    x = x_ref[0]
    t = x.shape[0]
    h = _rms(x, g_ref[...]) * (1.0 + sc_ref[0]) + sh_ref[0]
    u = _dot(fl_ref[...], h.astype(BF16))
    ur = u[:t].astype(BF16)
    ui = u[t:].astype(BF16)
    gw = FOURIER_GW
    f = jnp.concatenate(
        [_dot(ur[:, g * gw:(g + 1) * gw], cc_ref[...]) + _dot(ui[:, g * gw:(g + 1) * gw], scg_ref[...])
         for g in range(FOURIER_GROUPS)], axis=1)
    y = _dot(f.astype(BF16), wo_ref[...])
    xo_ref[0] = x + ga_ref[0] * _rms(y, gpost_ref[...])


def _fourier_ctx(tok, g, sh, sc, wo, gpost, ga):
    b, t, d = tok.shape
    cl, sl = _dft_cos_sin(t)
    cg, sg = _dft_cos_sin(FOURIER_GW)
    fl = jnp.asarray(np.concatenate([cl, -sl], axis=0) / np.sqrt(t), BF16)
    ccg = jnp.asarray(cg / np.sqrt(FOURIER_GW), BF16)
    scg = jnp.asarray(sg / np.sqrt(FOURIER_GW), BF16)
    row = pl.BlockSpec((1, t, d), lambda bi: (bi, 0, 0))
    return pl.pallas_call(
        _fourier_ctx_kernel,
        grid=(b,),
        in_specs=[row, _const_spec((1, d)), _batch_row_spec(d), _batch_row_spec(d), _const_spec(fl.shape),
                  _const_spec(ccg.shape), _const_spec(scg.shape), _const_spec(wo.shape),
                  _const_spec((1, d)), _batch_row_spec(d)],
        out_specs=row,
        out_shape=jax.ShapeDtypeStruct((b, t, d), F32),
        compiler_params=_cparams(("parallel",)),
        name="fourier_ctx",
    )(tok, g, sh, sc, fl, ccg, scg, wo, gpost, ga)


def _router_kernel(x_ref, g_ref, sh_ref, sc_ref, wrh_ref, wrl_ref, h_ref, gate_ref):
    h = _rms(x_ref[0], g_ref[...]) * (1.0 + sc_ref[0]) + sh_ref[0]
    h_ref[0] = h.astype(BF16)
    logits = _dot3(h, wrh_ref[...], wrl_ref[...])
    lane = lax.broadcasted_iota(jnp.int32, logits.shape, 1)
    neg = -jnp.inf
    l1 = jnp.where(lane < N_EXPERTS, logits, neg)
    m1 = l1.max(axis=-1, keepdims=True)
    i1 = jnp.where(l1 == m1, lane, LANES).min(axis=-1, keepdims=True)
    l2 = jnp.where(lane == i1, neg, l1)
    m2 = l2.max(axis=-1, keepdims=True)
    i2 = jnp.where(l2 == m2, lane, LANES).min(axis=-1, keepdims=True)
    e = jnp.exp(m2 - m1)
    g1 = 1.0 / (1.0 + e)
    g2 = e / (1.0 + e)
    gate_ref[0] = jnp.where(lane == i1, g1, jnp.where(lane == i2, g2, 0.0))


def _router(tok, g, sh, sc, wr_hi, wr_lo, tm):
    b, t, d = tok.shape
    row = lambda n: pl.BlockSpec((1, tm, n), lambda bi, ti: (bi, ti, 0))
    return pl.pallas_call(
        _router_kernel,
        grid=(b, t // tm),
        in_specs=[row(d), _const_spec((1, d)), _batch_row_spec(d), _batch_row_spec(d),
                  _const_spec(wr_hi.shape), _const_spec(wr_lo.shape)],
        out_specs=[row(d), row(LANES)],
        out_shape=[jax.ShapeDtypeStruct((b, t, d), BF16), jax.ShapeDtypeStruct((b, t, LANES), F32)],
        compiler_params=_cparams(("parallel", "parallel")),
        name="moe_router",
    )(tok, g, sh, sc, wr_hi, wr_lo)


def _moe_dense_kernel(x_ref, h_ref, gate_ref, w1_ref, w3_ref, w2_ref, g_ref, ga_ref, xo_ref, acc_ref):
    e = pl.program_id(2)
    f = pl.program_id(3)
    h = h_ref[0]
    gates = gate_ref[0]
    lane = lax.broadcasted_iota(jnp.int32, gates.shape, 1)
    ge = jnp.where(lane == e, gates, 0.0).sum(axis=-1, keepdims=True)
    a = _dot(h, w1_ref[0])
    mid = (_silu(a) * _dot(h, w3_ref[0])).astype(BF16)
    contrib = jnp.where(ge > 0.0, ge * _dot(mid, w2_ref[0]), 0.0)
    first = jnp.logical_and(e == 0, f == 0)

    @pl.when(first)
    def _():
        acc_ref[...] = contrib

    @pl.when(jnp.logical_not(first))
    def _():
        acc_ref[...] += contrib

    @pl.when(jnp.logical_and(e == pl.num_programs(2) - 1, f == pl.num_programs(3) - 1))
    def _():
        xo_ref[0] = x_ref[0] + ga_ref[0] * _rms(acc_ref[...], g_ref[...])


def _moe_dense(tok, h, gates, w1, w3, w2, g, ga, tm, fc):
    b, t, d = tok.shape
    ne, _, dff = w1.shape
    row = lambda n: pl.BlockSpec((1, tm, n), lambda bi, ti, ei, fi: (bi, ti, 0))
    return pl.pallas_call(
        _moe_dense_kernel,
        grid=(b, t // tm, ne, dff // fc),
        in_specs=[row(d), row(d), row(LANES),
                  pl.BlockSpec((1, d, fc), lambda bi, ti, ei, fi: (ei, 0, fi)),
                  pl.BlockSpec((1, d, fc), lambda bi, ti, ei, fi: (ei, 0, fi)),
                  pl.BlockSpec((1, fc, d), lambda bi, ti, ei, fi: (ei, fi, 0)),
                  _const_spec((1, d)),
                  pl.BlockSpec((1, 1, d), lambda bi, ti, ei, fi: (bi, 0, 0))],
        out_specs=row(d),
        out_shape=jax.ShapeDtypeStruct((b, t, d), F32),
        scratch_shapes=[pltpu.VMEM((tm, d), F32)],
        compiler_params=_cparams(("parallel", "parallel", "arbitrary", "arbitrary")),
        name="moe_experts",
    )(tok, h, gates, w1, w3, w2, g, ga)


_ROPE_SWAP = np.concatenate([np.arange(16, 32), np.arange(0, 16), np.arange(48, 64), np.arange(32, 48)])


def _rope_tables(n_tok):
    rows = n_tok // GRID_W
    row = jnp.broadcast_to(jnp.arange(rows, dtype=F32)[:, None], (rows, GRID_W)).reshape(-1)
    col = jnp.broadcast_to(jnp.arange(GRID_W, dtype=F32)[None, :], (rows, GRID_W)).reshape(-1)
    half = MLA_ROPE // 2
    inv = 1.0 / (ROPE_BASE ** (jnp.arange(0, half, 2, dtype=F32) / half))
    cr, sr = jnp.cos(row[:, None] * inv), jnp.sin(row[:, None] * inv)
    cc, sc = jnp.cos(col[:, None] * inv), jnp.sin(col[:, None] * inv)
    cos64 = jnp.concatenate([cr, cr, cc, cc], axis=-1)
    sin64 = jnp.concatenate([-sr, sr, -sc, sc], axis=-1)
    return (jnp.tile(cos64, (1, MLA_HEADS)), jnp.tile(sin64, (1, MLA_HEADS)), cos64.T, sin64.T)


def _identity_rope_tables(n_tok):
    one = jnp.ones((n_tok, MLA_ROPE), F32)
    zero = jnp.zeros((n_tok, MLA_ROPE), F32)
    return (jnp.tile(one, (1, MLA_HEADS)), jnp.tile(zero, (1, MLA_HEADS)), one.T, zero.T)


def _even_weights(w_in, w_gate_f, b_gate_f, w_gate_b, b_gate_b, q_norm, w_uq, kv_norm, w_ukv):
    nqk = GLA_HEADS * GLA_DK
    nv = GLA_HEADS * GLA_DV
    o_z = 2 * nqk + nv
    o_r = o_z + 2 * GLA_RANK
    o_cq = o_r + nv
    o_kv = o_cq + MLA_Q_RANK
    o_kr = o_kv + MLA_KV_RANK
    d = w_in.shape[0]
    wp = jnp.concatenate([w_in[:, :o_z], w_in[:, o_r:o_kr], w_in[:, o_z:o_r],
                          jnp.zeros((d, LANES - 2 * GLA_RANK), F32)], axis=1).astype(BF16)
    kr = w_in[:, o_kr:o_kr + MLA_ROPE]
    wkr = jnp.concatenate([kr, kr[:, _ROPE_SWAP]], axis=1).T.astype(BF16)
    wg = jnp.zeros((LANES, 2 * nqk), F32)
    wg = wg.at[0:GLA_RANK, 0:nqk].set(w_gate_f).at[GLA_RANK:2 * GLA_RANK, nqk:].set(w_gate_b)
    wg_hi = wg.astype(BF16)
    wg_lo = (wg - wg_hi.astype(F32)).astype(BF16)
    bg = jnp.concatenate([b_gate_f, b_gate_b])[None, :]
    hq = np.arange(MLA_HEADS)[:, None] * MLA_QK
    nope_idx = (hq + np.arange(MLA_NOPE)[None, :]).reshape(-1)
    rope_idx = (hq + MLA_NOPE + np.arange(MLA_ROPE)[None, :]).reshape(-1)
    swap_idx = (hq + MLA_NOPE + _ROPE_SWAP[None, :]).reshape(-1)
    wuq = w_uq[:, np.concatenate([nope_idx, rope_idx, swap_idx])].astype(BF16)
    hk = np.arange(MLA_HEADS)[:, None] * (MLA_NOPE + MLA_V)
    k_idx = (hk + np.arange(MLA_NOPE)[None, :]).reshape(-1)
    v_idx = (hk + MLA_NOPE + np.arange(MLA_V)[None, :]).reshape(-1)
    return dict(wp=wp, wkr=wkr, wg_hi=wg_hi, wg_lo=wg_lo, bg=bg, qn=q_norm[None, :], wuq=wuq,
                kvn=kv_norm[None, :], wuk=w_ukv[:, k_idx].T.astype(BF16), wuv=w_ukv[:, v_idx].astype(BF16))


def _mods(m, rows, batch):
    d = D_MODEL
    if rows is None:
        return [jnp.broadcast_to(m[batch, k * d:(k + 1) * d][None, None, :], (batch, 1, d)) for k in range(6)]
    return [m[:batch, k * d:(k + 1) * d][:, None, :] for k in range(6)]


def kernel(x, c, ctx, c_ctx, w_mod, b_mod, g_mix_pre, g_mix_post, g_ffn_pre, g_ffn_post, e_w_in, e_w_gate_f, e_b_gate_f, e_w_gate_b, e_b_gate_b, e_gla_norm, e_q_norm, e_w_uq, e_kv_norm, e_w_ukv, e_w_o, e_w1, e_w3, e_w2, o_w_o, o_w_router, o_w1, o_w3, o_w2):
    batch, seq, d = x.shape
    n_ctx = ctx.shape[1]
    cond = jnp.zeros((16, d), F32).at[:batch].set(c).at[batch].set(c_ctx)
    mods = _adaln(cond, w_mod, b_mod)
    rope_x = _rope_tables(seq)
    rope_c = _identity_rope_tables(n_ctx)
    nqk = GLA_HEADS * GLA_DK
    last_read = 2 * ((DEPTH - 1) // 2)
    xs, xc = x, ctx
    for i in range(DEPTH):
        j = i // 2
        ctx_live = i <= last_read
        ctx_full = i < last_read
        mx = _mods(mods[i], 0, batch)
        mc = _mods(mods[i], None, batch)
        gpre, gpost = g_mix_pre[i][None, :], g_mix_post[i][None, :]
        fpre, fpost = g_ffn_pre[i][None, :], g_ffn_post[i][None, :]
        if i % 2 == 0:
            w = _even_weights(e_w_in[j], e_w_gate_f[j], e_b_gate_f[j], e_w_gate_b[j], e_b_gate_b[j],
                              e_q_norm[j], e_w_uq[j], e_kv_norm[j], e_w_ukv[j])
            wo1 = e_w_o[j][:GLA_HEADS * GLA_DV].astype(BF16)
            wo2 = e_w_o[j][GLA_HEADS * GLA_DV:].astype(BF16)
            gn = e_gla_norm[j][None, :]
            w1, w3, w2 = e_w1[j].astype(BF16), e_w3[j].astype(BF16), e_w2[j].astype(BF16)
            zero_state = jnp.zeros((batch, GLA_DV, nqk), F32)
            if ctx_live:
                cgq, cgk, cgv, cgr, claf, clab, cq, ckt, cv = _in_proj(xc, gpre, mc[0], mc[1], w, rope_c, n_ctx)
                co_f, co_b, s_f, s_b = _gla(cgq, cgk, cgv, claf, clab, zero_state, zero_state, n_ctx)
                srcs_c = [(ckt, cv)]
            else:
                s_f = s_b = zero_state
                srcs_c = []
            gq, gk, gv, gr, laf, lab, q, kt, v = _in_proj(xs, gpre, mx[0], mx[1], w, rope_x, min(512, seq))
            o_f, o_b, _, _ = _gla(gq, gk, gv, laf, lab, s_f, s_b, min(512, seq))
            a = _attention(q, [(kt, v)] + srcs_c, min(256, seq))
            xs, hx = _even_out(xs, o_f, o_b, gr, a, gn, wo1, wo2, gpost, mx[2], fpre, mx[3], mx[4], min(512, seq))
            xs = _ffn(xs, hx, w1, w3, w2, fpost, mx[5], min(512, seq), 1408)
            if ctx_full:
                ac = _attention(cq, srcs_c, n_ctx)
                xc, hc = _even_out(xc, co_f, co_b, cgr, ac, gn, wo1, wo2, gpost, mc[2], fpre, mc[3], mc[4], n_ctx)
                xc = _ffn(xc, hc, w1, w3, w2, fpost, mc[5], n_ctx, 1408)
        else:
            wo = o_w_o[j].astype(BF16)
            wr = jnp.zeros((d, LANES), F32).at[:, :N_EXPERTS].set(o_w_router[j])
            wr_hi = wr.astype(BF16)
            wr_lo = (wr - wr_hi.astype(F32)).astype(BF16)
            w1, w3, w2 = o_w1[j].astype(BF16), o_w3[j].astype(BF16), o_w2[j].astype(BF16)
            xs = _fourier_x(xs, gpre, mx[0], mx[1], wo, gpost, mx[2])
            hx, gates = _router(xs, fpre, mx[3], mx[4], wr_hi, wr_lo, min(512, seq))
            xs = _moe_dense(xs, hx, gates, w1, w3, w2, fpost, mx[5], min(512, seq), 1792)
            if ctx_full:
                xc = _fourier_ctx(xc, gpre, mc[0], mc[1], wo, gpost, mc[2])
                hc, gates_c = _router(xc, fpre, mc[3], mc[4], wr_hi, wr_lo, n_ctx)
                xc = _moe_dense(xc, hc, gates_c, w1, w3, w2, fpost, mc[5], n_ctx, 1792)
    return xs
```

```python
import functools

import numpy as np
import jax
import jax.numpy as jnp
from jax import lax
from jax.experimental import pallas as pl
from jax.experimental.pallas import tpu as pltpu
from jax.experimental.pallas import tpu_sc as plsc

F32 = jnp.float32
BF16 = jnp.bfloat16

EPS = 1e-6
D_MODEL = 1024
DEPTH = 4
GRID_W = 64
GLA_HEADS = 4
GLA_DK = 64
GLA_DV = 128
GLA_RANK = 16
GLA_TAU = 16.0
GLA_CHUNK = 64
MLA_HEADS = 4
MLA_Q_RANK = 256
MLA_KV_RANK = 128
MLA_NOPE = 128
MLA_ROPE = 64
MLA_V = 128
MLA_QK = MLA_NOPE + MLA_ROPE
ROPE_BASE = 10000.0
FOURIER_GROUPS = 4
FOURIER_GW = D_MODEL // FOURIER_GROUPS
N_EXPERTS = 8
LANES = 128
VMEM_LIMIT = 48 * 1024 * 1024


def _cparams(sem):
    return pltpu.CompilerParams(dimension_semantics=sem, vmem_limit_bytes=VMEM_LIMIT)


def _dot(a, b):
    return jnp.dot(a, b, preferred_element_type=F32)


def _dot_nt(a, b):
    return lax.dot_general(a, b, (((1,), (1,)), ((), ())), preferred_element_type=F32)


def _dot_tn(a, b):
    return lax.dot_general(a, b, (((0,), (0,)), ((), ())), preferred_element_type=F32)


def _split(x):
    hi = x.astype(BF16)
    lo = (x - hi.astype(F32)).astype(BF16)
    return hi, lo


def _dot3(a, b_hi, b_lo):
    a_hi, a_lo = _split(a)
    return _dot(a_hi, b_hi) + _dot(a_lo, b_hi) + _dot(a_hi, b_lo)


def _rms(x, g):
    return x * lax.rsqrt(jnp.mean(x * x, axis=-1, keepdims=True) + EPS) * g


def _silu(x):
    return x / (1.0 + jnp.exp(-x))


def _const_spec(shape):
    nd = len(shape)
    return pl.BlockSpec(shape, lambda *_: (0,) * nd)


def _batch_row_spec(d):
    return pl.BlockSpec((1, 1, d), lambda b, *_: (b, 0, 0))


def _adaln_kernel(c_ref, w_ref, b_ref, o_ref):
    a = _silu(c_ref[...])
    w_hi, w_lo = _split(w_ref[0])
    o_ref[0] = _dot3(a, w_hi, w_lo) + b_ref[0]


def _adaln(cond, w_mod, b_mod):
    depth, d, n = w_mod.shape
    rows = cond.shape[0]
    bn = 1536
    return pl.pallas_call(
        _adaln_kernel,
        grid=(depth, n // bn),
        in_specs=[
            pl.BlockSpec((rows, d), lambda i, j: (0, 0)),
            pl.BlockSpec((1, d, bn), lambda i, j: (i, 0, j)),
            pl.BlockSpec((1, 1, bn), lambda i, j: (i, 0, j)),
        ],
        out_specs=pl.BlockSpec((1, rows, bn), lambda i, j: (i, 0, j)),
        out_shape=jax.ShapeDtypeStruct((depth, rows, n), F32),
        compiler_params=_cparams(("arbitrary", "arbitrary")),
        name="adaln",
    )(cond, w_mod, b_mod.reshape(depth, 1, n))


def _in_proj_kernel(x_ref, g_ref, sh_ref, sc_ref, wp_ref, wkr_ref, wgh_ref, wgl_ref, bg_ref,
                    qn_ref, wuq_ref, kvn_ref, wuk_ref, wuv_ref, cq_ref, sq_ref, ck_ref, sk_ref,
                    gq_ref, gk_ref, gv_ref, gr_ref, laf_ref, lab_ref, q_ref, kt_ref, v_ref):
    h = _rms(x_ref[0], g_ref[...]) * (1.0 + sc_ref[0]) + sh_ref[0]
    hb = h.astype(BF16)
    p = _dot(hb, wp_ref[...])
    nqk = GLA_HEADS * GLA_DK
    nv = GLA_HEADS * GLA_DV
    gq_ref[0] = p[:, 0:nqk] * (GLA_DK ** -0.5)
    gk_ref[0] = p[:, nqk:2 * nqk]
    gv_ref[0] = p[:, 2 * nqk:2 * nqk + nv]
    gr_ref[0] = p[:, 2 * nqk + nv:2 * nqk + 2 * nv]
    o = 2 * nqk + 2 * nv
    cq = p[:, o:o + MLA_Q_RANK]
    ckv = p[:, o + MLA_Q_RANK:o + MLA_Q_RANK + MLA_KV_RANK]
    tail = p[:, o + MLA_Q_RANK + MLA_KV_RANK:]
    pre = _dot3(tail, wgh_ref[...], wgl_ref[...]) + bg_ref[...]
    la = (jnp.minimum(pre, 0.0) - jnp.log(1.0 + jnp.exp(-jnp.abs(pre)))) * (1.0 / GLA_TAU)
    laf_ref[0] = la[:, :nqk]
    lab_ref[0] = la[:, nqk:]
    q = _dot(_rms(cq, qn_ref[...]).astype(BF16), wuq_ref[...])
    att_scale = MLA_QK ** -0.5
    nn = MLA_HEADS * MLA_NOPE
    nr = MLA_HEADS * MLA_ROPE
    q_rope = q[:, nn:nn + nr] * cq_ref[...] + q[:, nn + nr:] * sq_ref[...]
    for hd in range(MLA_HEADS):
        q_ref[0, hd, :, 0:MLA_NOPE] = (q[:, hd * MLA_NOPE:(hd + 1) * MLA_NOPE] * att_scale).astype(BF16)
        q_ref[0, hd, :, MLA_NOPE:MLA_QK] = (q_rope[:, hd * MLA_ROPE:(hd + 1) * MLA_ROPE] * att_scale).astype(BF16)
    ckvn = _rms(ckv, kvn_ref[...]).astype(BF16)
    kt = _dot_nt(wuk_ref[...], ckvn)
    v = _dot(ckvn, wuv_ref[...])
    kr2 = _dot_nt(wkr_ref[...], hb)
    kr = (kr2[:MLA_ROPE] * ck_ref[...] + kr2[MLA_ROPE:] * sk_ref[...]).astype(BF16)
    for hd in range(MLA_HEADS):
        kt_ref[0, hd, 0:MLA_NOPE, :] = kt[hd * MLA_NOPE:(hd + 1) * MLA_NOPE].astype(BF16)
        kt_ref[0, hd, MLA_NOPE:MLA_QK, :] = kr
        v_ref[0, hd] = v[:, hd * MLA_V:(hd + 1) * MLA_V].astype(BF16)


def _in_proj(tok, g, sh, sc, w, tabs, tm):
    b, t, d = tok.shape
    cq, sq, ck, sk = tabs
    nqk = GLA_HEADS * GLA_DK
    nv = GLA_HEADS * GLA_DV
    row = lambda n: pl.BlockSpec((1, tm, n), lambda bi, ti: (bi, ti, 0))
    weights = (w["wp"], w["wkr"], w["wg_hi"], w["wg_lo"], w["bg"], w["qn"], w["wuq"], w["kvn"],
               w["wuk"], w["wuv"])
    return pl.pallas_call(
        _in_proj_kernel,
        grid=(b, t // tm),
        in_specs=[row(d), _const_spec((1, d)), _batch_row_spec(d), _batch_row_spec(d)]
        + [_const_spec(a.shape) for a in weights]
        + [pl.BlockSpec((tm, MLA_HEADS * MLA_ROPE), lambda bi, ti: (ti, 0)),
           pl.BlockSpec((tm, MLA_HEADS * MLA_ROPE), lambda bi, ti: (ti, 0)),
           pl.BlockSpec((MLA_ROPE, tm), lambda bi, ti: (0, ti)),
           pl.BlockSpec((MLA_ROPE, tm), lambda bi, ti: (0, ti))],
        out_specs=[row(nqk), row(nqk), row(nv), row(nv), row(nqk), row(nqk),
                   pl.BlockSpec((1, MLA_HEADS, tm, MLA_QK), lambda bi, ti: (bi, 0, ti, 0)),
                   pl.BlockSpec((1, MLA_HEADS, MLA_QK, tm), lambda bi, ti: (bi, 0, 0, ti)),
                   pl.BlockSpec((1, MLA_HEADS, tm, MLA_V), lambda bi, ti: (bi, 0, ti, 0))],
        out_shape=[jax.ShapeDtypeStruct((b, t, nqk), F32), jax.ShapeDtypeStruct((b, t, nqk), F32),
                   jax.ShapeDtypeStruct((b, t, nv), F32), jax.ShapeDtypeStruct((b, t, nv), F32),
                   jax.ShapeDtypeStruct((b, t, nqk), F32), jax.ShapeDtypeStruct((b, t, nqk), F32),
                   jax.ShapeDtypeStruct((b, MLA_HEADS, t, MLA_QK), BF16),
                   jax.ShapeDtypeStruct((b, MLA_HEADS, MLA_QK, t), BF16),
                   jax.ShapeDtypeStruct((b, MLA_HEADS, t, MLA_V), BF16)],
        compiler_params=_cparams(("parallel", "parallel")),
        name="even_in_proj",
    )(tok, g, sh, sc, *weights, cq, sq, ck, sk)


def _gla_chunk(q_ref, k_ref, v_ref, l_ref, o_ref, st_ref, row0, tri, mask, last_row):
    c = GLA_CHUNK
    la = l_ref[0, pl.ds(row0, c), :]
    la_hi, la_lo = _split(la)
    bc = _dot(tri, la_hi) + _dot(tri, la_lo)
    bl = bc[last_row:last_row + 1, :]
    q = q_ref[0, pl.ds(row0, c), :]
    k = k_ref[0, pl.ds(row0, c), :]
    qc = (q * jnp.exp(bc)).astype(BF16)
    kc = (k * jnp.exp(-bc)).astype(BF16)
    kd = (k * jnp.exp(bl - bc)).astype(BF16)
    vb = v_ref[0, pl.ds(row0, c), :].astype(BF16)
    st = st_ref[...]
    stb = st.astype(BF16)
    outs, upds = [], []
    for hd in range(GLA_HEADS):
        ks = slice(hd * GLA_DK, (hd + 1) * GLA_DK)
        vs = slice(hd * GLA_DV, (hd + 1) * GLA_DV)
        att = jnp.where(mask, _dot_nt(qc[:, ks], kc[:, ks]), 0.0).astype(BF16)
        outs.append(_dot(att, vb[:, vs]) + _dot_nt(qc[:, ks], stb[:, ks]))
        upds.append(_dot_tn(vb[:, vs], kd[:, ks]))
    o_ref[0, pl.ds(row0, c), :] = jnp.concatenate(outs, axis=1)
    st_ref[...] = st * jnp.exp(bl) + jnp.concatenate(upds, axis=1)


def _gla_kernel(qf, kf, vf, lf, qb, kb, vb, lb, s0f, s0b, of, ob, sff, sfb, stf, stb, *, nc):
    j = pl.program_id(1)

    @pl.when(j == 0)
    def _():
        stf[...] = s0f[0]
        stb[...] = s0b[0]

    c = GLA_CHUNK
    r = lax.broadcasted_iota(jnp.int32, (c, c), 0)
    cc = lax.broadcasted_iota(jnp.int32, (c, c), 1)
    lower = r >= cc
    upper = r <= cc
    tri_l = jnp.where(lower, 1.0, 0.0).astype(BF16)
    tri_u = jnp.where(upper, 1.0, 0.0).astype(BF16)

    def body(ci, carry):
        rf = pl.multiple_of(ci * c, c)
        rb = pl.multiple_of((nc - 1 - ci) * c, c)
        _gla_chunk(qf, kf, vf, lf, of, stf, rf, tri_l, lower, c - 1)
        _gla_chunk(qb, kb, vb, lb, ob, stb, rb, tri_u, upper, 0)
        return carry

    lax.fori_loop(0, nc, body, 0)

    @pl.when(j == pl.num_programs(1) - 1)
    def _():
        sff[0] = stf[...]
        sfb[0] = stb[...]


def _gla(gq, gk, gv, laf, lab, s0f, s0b, tb):
    b, t, nqk = gq.shape
    nv = gv.shape[-1]
    nblk = t // tb
    fwd = lambda n: pl.BlockSpec((1, tb, n), lambda bi, j: (bi, j, 0))
    bwd = lambda n: pl.BlockSpec((1, tb, n), lambda bi, j: (bi, nblk - 1 - j, 0))
    st = pl.BlockSpec((1, GLA_DV, nqk), lambda bi, j: (bi, 0, 0))
    return pl.pallas_call(
        functools.partial(_gla_kernel, nc=tb // GLA_CHUNK),
        grid=(b, nblk),
        in_specs=[fwd(nqk), fwd(nqk), fwd(nv), fwd(nqk), bwd(nqk), bwd(nqk), bwd(nv), bwd(nqk), st, st],
        out_specs=[fwd(nv), bwd(nv), st, st],
        out_shape=[jax.ShapeDtypeStruct((b, t, nv), F32), jax.ShapeDtypeStruct((b, t, nv), F32),
                   jax.ShapeDtypeStruct((b, GLA_DV, nqk), F32), jax.ShapeDtypeStruct((b, GLA_DV, nqk), F32)],
        scratch_shapes=[pltpu.VMEM((GLA_DV, nqk), F32), pltpu.VMEM((GLA_DV, nqk), F32)],
        compiler_params=_cparams(("parallel", "arbitrary")),
        name="gla_scan",
    )(gq, gk, gv, laf, gq, gk, gv, lab, s0f, s0b)


def _attn_kernel(*refs, n_src):
    q_ref = refs[0]
    kts = refs[1:1 + 2 * n_src:2]
    vs = refs[2:2 + 2 * n_src:2]
    o_ref = refs[1 + 2 * n_src]
    q = q_ref[0, 0]
    ss = [_dot(q, kt[0, 0]) for kt in kts]
    m = ss[0].max(axis=-1, keepdims=True)
    for s in ss[1:]:
        m = jnp.maximum(m, s.max(axis=-1, keepdims=True))
    ps = [jnp.exp(s - m) for s in ss]
    l = ps[0].sum(axis=-1, keepdims=True)
    for p in ps[1:]:
        l = l + p.sum(axis=-1, keepdims=True)
    o = _dot(ps[0].astype(BF16), vs[0][0, 0])
    for p, v in zip(ps[1:], vs[1:]):
        o = o + _dot(p.astype(BF16), v[0, 0])
    o_ref[0] = (o / l).astype(o_ref.dtype)


def _attention(q, srcs, bq):
    b, nh, t, dqk = q.shape
    in_specs = [pl.BlockSpec((1, 1, bq, dqk), lambda bi, hi, qi: (bi, hi, qi, 0))]
    args = [q]
    for kt, v in srcs:
        tk = kt.shape[-1]
        in_specs.append(pl.BlockSpec((1, 1, dqk, tk), lambda bi, hi, qi: (bi, hi, 0, 0)))
        in_specs.append(pl.BlockSpec((1, 1, tk, MLA_V), lambda bi, hi, qi: (bi, hi, 0, 0)))
        args += [kt, v]
    return pl.pallas_call(
        functools.partial(_attn_kernel, n_src=len(srcs)),
        grid=(b, nh, t // bq),
        in_specs=in_specs,
        out_specs=pl.BlockSpec((1, bq, MLA_V), lambda bi, hi, qi: (bi, qi, hi)),
        out_shape=jax.ShapeDtypeStruct((b, t, nh * MLA_V), BF16),
        compiler_params=_cparams(("parallel", "parallel", "arbitrary")),
        name="mla_attention",
    )(*args)


def _even_out_kernel(x_ref, of_ref, ob_ref, gr_ref, a_ref, gn_ref, wo1_ref, wo2_ref, gpost_ref, ga_ref,
                     gpre_ref, sh_ref, sc_ref, xo_ref, h_ref):
    o = of_ref[0] + ob_ref[0]
    parts = [_rms(o[:, hd * GLA_DV:(hd + 1) * GLA_DV], gn_ref[...]) for hd in range(GLA_HEADS)]
    fin = jnp.concatenate(parts, axis=1) * _silu(gr_ref[0])
    y = _dot(fin.astype(BF16), wo1_ref[...]) + _dot(a_ref[0], wo2_ref[...])
    xn = x_ref[0] + ga_ref[0] * _rms(y, gpost_ref[...])
    xo_ref[0] = xn
    h_ref[0] = (_rms(xn, gpre_ref[...]) * (1.0 + sc_ref[0]) + sh_ref[0]).astype(BF16)


def _even_out(tok, o_f, o_b, gr, a, gn, wo1, wo2, gpost, ga, gpre, sh, sc, tm):
    b, t, d = tok.shape
    nv = o_f.shape[-1]
    row = lambda n: pl.BlockSpec((1, tm, n), lambda bi, ti: (bi, ti, 0))
    return pl.pallas_call(
        _even_out_kernel,
        grid=(b, t // tm),
        in_specs=[row(d), row(nv), row(nv), row(nv), row(a.shape[-1]), _const_spec(gn.shape),
                  _const_spec(wo1.shape), _const_spec(wo2.shape), _const_spec((1, d)), _batch_row_spec(d),
                  _const_spec((1, d)), _batch_row_spec(d), _batch_row_spec(d)],
        out_specs=[row(d), row(d)],
        out_shape=[jax.ShapeDtypeStruct((b, t, d), F32), jax.ShapeDtypeStruct((b, t, d), BF16)],
        compiler_params=_cparams(("parallel", "parallel")),
        name="even_out_proj",
    )(tok, o_f, o_b, gr, a, gn, wo1, wo2, gpost, ga, gpre, sh, sc)


def _ffn_kernel(x_ref, h_ref, w1_ref, w3_ref, w2_ref, g_ref, ga_ref, xo_ref, acc_ref):
    f = pl.program_id(2)
    h = h_ref[0]
    a = _dot(h, w1_ref[...])
    mid = (_silu(a) * _dot(h, w3_ref[...])).astype(BF16)
    contrib = _dot(mid, w2_ref[...])

    @pl.when(f == 0)
    def _():
        acc_ref[...] = contrib

    @pl.when(f > 0)
    def _():
        acc_ref[...] += contrib

    @pl.when(f == pl.num_programs(2) - 1)
    def _():
        xo_ref[0] = x_ref[0] + ga_ref[0] * _rms(acc_ref[...], g_ref[...])


def _ffn(tok, h, w1, w3, w2, g, ga, tm, fc):
    b, t, d = tok.shape
    dff = w1.shape[-1]
    row = lambda: pl.BlockSpec((1, tm, d), lambda bi, ti, fi: (bi, ti, 0))
    return pl.pallas_call(
        _ffn_kernel,
        grid=(b, t // tm, dff // fc),
        in_specs=[row(), row(),
                  pl.BlockSpec((d, fc), lambda bi, ti, fi: (0, fi)),
                  pl.BlockSpec((d, fc), lambda bi, ti, fi: (0, fi)),
                  pl.BlockSpec((fc, d), lambda bi, ti, fi: (fi, 0)),
                  _const_spec((1, d)), _batch_row_spec(d)],
        out_specs=row(),
        out_shape=jax.ShapeDtypeStruct((b, t, d), F32),
        scratch_shapes=[pltpu.VMEM((tm, d), F32)],
        compiler_params=_cparams(("parallel", "parallel", "arbitrary")),
        name="swiglu_ffn",
    )(tok, h, w1, w3, w2, g, ga)


def _dft_cos_sin(n):
    idx = (np.arange(n)[:, None] * np.arange(n)[None, :]) % n
    ang = 2.0 * np.pi * idx.astype(np.float64) / n
    return np.cos(ang), np.sin(ang)


def _fourier1_kernel(x_ref, g_ref, sh_ref, sc_ref, f1_ref, tc_ref, ts_ref, o_ref, *, nb, n1):
    d = D_MODEL
    for j in range(nb):
        h = _rms(x_ref[0, :, j * d:(j + 1) * d], g_ref[...]) * (1.0 + sc_ref[0]) + sh_ref[0]
        a = _dot(f1_ref[...], h.astype(BF16))
        ar, ai = a[:n1], a[n1:]
        tc = jnp.concatenate([tc_ref[j]] * (d // LANES), axis=1)
        ts = jnp.concatenate([ts_ref[j]] * (d // LANES), axis=1)
        o_ref[0, 0, j] = (ar * tc - ai * ts).astype(BF16)
        o_ref[0, 1, j] = (ar * ts + ai * tc).astype(BF16)


def _fourier2_kernel(b_ref, x_ref, f2_ref, cc_ref, sc_ref, wo_ref, gpost_ref, ga_ref, xo_ref, *, kb, n2):
    d = D_MODEL
    u = _dot(f2_ref[...], b_ref[0])
    ur = jnp.concatenate([u[:n2, j * d:(j + 1) * d] for j in range(kb)], axis=0).astype(BF16)
    ui = jnp.concatenate([u[n2:, j * d:(j + 1) * d] for j in range(kb)], axis=0).astype(BF16)
    gw = FOURIER_GW
    f = jnp.concatenate(
        [_dot(ur[:, g * gw:(g + 1) * gw], cc_ref[...]) + _dot(ui[:, g * gw:(g + 1) * gw], sc_ref[...])
         for g in range(FOURIER_GROUPS)], axis=1)
    y = _dot(f.astype(BF16), wo_ref[...])
    yn = ga_ref[0] * _rms(y, gpost_ref[...])
    for j in range(kb):
        xo_ref[0, :, j * d:(j + 1) * d] = x_ref[0, :, j * d:(j + 1) * d] + yn[j * n2:(j + 1) * n2]


def _fourier_x(tok, g, sh, sc, wo, gpost, ga):
    b, t, d = tok.shape
    n1, n2 = 128, t // 128
    nb, kb = 2, 8
    c1, s1 = _dft_cos_sin(n1)
    c2, s2 = _dft_cos_sin(n2)
    cg, sg = _dft_cos_sin(FOURIER_GW)
    f1 = jnp.asarray(np.concatenate([c1, -s1], axis=0) / np.sqrt(n1), BF16)
    f2 = jnp.asarray(np.block([[c2, s2], [-s2, c2]]) / np.sqrt(n2), BF16)
    ccg = jnp.asarray(cg / np.sqrt(FOURIER_GW), BF16)
    scg = jnp.asarray(sg / np.sqrt(FOURIER_GW), BF16)
    tw = 2.0 * np.pi * ((np.arange(n2)[:, None] * np.arange(n1)[None, :]) % t).astype(np.float64) / t
    tc = jnp.asarray(np.broadcast_to(np.cos(tw)[:, :, None], (n2, n1, LANES)), F32)
    ts = jnp.asarray(np.broadcast_to(-np.sin(tw)[:, :, None], (n2, n1, LANES)), F32)
    stage1 = pl.pallas_call(
        functools.partial(_fourier1_kernel, nb=nb, n1=n1),
        grid=(b, n2 // nb),
        in_specs=[pl.BlockSpec((1, n1, nb * d), lambda bi, ji: (bi, 0, ji)),
                  _const_spec((1, d)), _batch_row_spec(d), _batch_row_spec(d),
                  _const_spec(f1.shape),
                  pl.BlockSpec((nb, n1, LANES), lambda bi, ji: (ji, 0, 0)),
                  pl.BlockSpec((nb, n1, LANES), lambda bi, ji: (ji, 0, 0))],
        out_specs=pl.BlockSpec((1, 2, nb, n1, d), lambda bi, ji: (bi, 0, ji, 0, 0)),
        out_shape=jax.ShapeDtypeStruct((b, 2, n2, n1, d), BF16),
        compiler_params=_cparams(("parallel", "parallel")),
        name="fourier_stage1",
    )(tok.reshape(b, n1, n2 * d), g, sh, sc, f1, tc, ts)
    out = pl.pallas_call(
        functools.partial(_fourier2_kernel, kb=kb, n2=n2),
        grid=(b, n1 // kb),
        in_specs=[pl.BlockSpec((1, 2 * n2, kb * d), lambda bi, ki: (bi, 0, ki)),
                  pl.BlockSpec((1, n2, kb * d), lambda bi, ki: (bi, 0, ki)),
                  _const_spec(f2.shape), _const_spec(ccg.shape), _const_spec(scg.shape),
                  _const_spec(wo.shape), _const_spec((1, d)), _batch_row_spec(d)],
        out_specs=pl.BlockSpec((1, n2, kb * d), lambda bi, ki: (bi, 0, ki)),
        out_shape=jax.ShapeDtypeStruct((b, n2, n1 * d), F32),
        compiler_params=_cparams(("parallel", "parallel")),
        name="fourier_stage2",
    )(stage1.reshape(b, 2 * n2, n1 * d), tok.reshape(b, n2, n1 * d), f2, ccg, scg, wo, gpost, ga)
    return out.reshape(b, t, d)


def _fourier_ctx_kernel(x_ref, g_ref, sh_ref, sc_ref, fl_ref, cc_ref, scg_ref, wo_ref, gpost_ref, ga_ref,
                        xo_ref):
    x = x_ref[0]
    t = x.shape[0]
    h = _rms(x, g_ref[...]) * (1.0 + sc_ref[0]) + sh_ref[0]
    u = _dot(fl_ref[...], h.astype(BF16))
    ur = u[:t].astype(BF16)
    ui = u[t:].astype(BF16)
    gw = FOURIER_GW
    f = jnp.concatenate(
        [_dot(ur[:, g * gw:(g + 1) * gw], cc_ref[...]) + _dot(ui[:, g * gw:(g + 1) * gw], scg_ref[...])
         for g in range(FOURIER_GROUPS)], axis=1)
    y = _dot(f.astype(BF16), wo_ref[...])
    xo_ref[0] = x + ga_ref[0] * _rms(y, gpost_ref[...])


def _fourier_ctx(tok, g, sh, sc, wo, gpost, ga):
    b, t, d = tok.shape
    cl, sl = _dft_cos_sin(t)
    cg, sg = _dft_cos_sin(FOURIER_GW)
    fl = jnp.asarray(np.concatenate([cl, -sl], axis=0) / np.sqrt(t), BF16)
    ccg = jnp.asarray(cg / np.sqrt(FOURIER_GW), BF16)
    scg = jnp.asarray(sg / np.sqrt(FOURIER_GW), BF16)
    row = pl.BlockSpec((1, t, d), lambda bi: (bi, 0, 0))
    return pl.pallas_call(
        _fourier_ctx_kernel,
        grid=(b,),
        in_specs=[row, _const_spec((1, d)), _batch_row_spec(d), _batch_row_spec(d), _const_spec(fl.shape),
                  _const_spec(ccg.shape), _const_spec(scg.shape), _const_spec(wo.shape),
                  _const_spec((1, d)), _batch_row_spec(d)],
        out_specs=row,
        out_shape=jax.ShapeDtypeStruct((b, t, d), F32),
        compiler_params=_cparams(("parallel",)),
        name="fourier_ctx",
    )(tok, g, sh, sc, fl, ccg, scg, wo, gpost, ga)


def _router_kernel(x_ref, g_ref, sh_ref, sc_ref, wrh_ref, wrl_ref, h_ref, gate_ref):
    h = _rms(x_ref[0], g_ref[...]) * (1.0 + sc_ref[0]) + sh_ref[0]
    h_ref[0] = h.astype(BF16)
    logits = _dot3(h, wrh_ref[...], wrl_ref[...])
    lane = lax.broadcasted_iota(jnp.int32, logits.shape, 1)
    neg = -jnp.inf
    l1 = jnp.where(lane < N_EXPERTS, logits, neg)
    m1 = l1.max(axis=-1, keepdims=True)
    i1 = jnp.where(l1 == m1, lane, LANES).min(axis=-1, keepdims=True)
    l2 = jnp.where(lane == i1, neg, l1)
    m2 = l2.max(axis=-1, keepdims=True)
    i2 = jnp.where(l2 == m2, lane, LANES).min(axis=-1, keepdims=True)
    e = jnp.exp(m2 - m1)
    g1 = 1.0 / (1.0 + e)
    g2 = e / (1.0 + e)
    gate_ref[0] = jnp.where(lane == i1, g1, jnp.where(lane == i2, g2, 0.0))


def _router(tok, g, sh, sc, wr_hi, wr_lo, tm):
    b, t, d = tok.shape
    row = lambda n: pl.BlockSpec((1, tm, n), lambda bi, ti: (bi, ti, 0))
    return pl.pallas_call(
        _router_kernel,
        grid=(b, t // tm),
        in_specs=[row(d), _const_spec((1, d)), _batch_row_spec(d), _batch_row_spec(d),
                  _const_spec(wr_hi.shape), _const_spec(wr_lo.shape)],
        out_specs=[row(d), row(LANES)],
        out_shape=[jax.ShapeDtypeStruct((b, t, d), BF16), jax.ShapeDtypeStruct((b, t, LANES), F32)],
        compiler_params=_cparams(("parallel", "parallel")),
        name="moe_router",
    )(tok, g, sh, sc, wr_hi, wr_lo)


def _moe_dense_kernel(x_ref, h_ref, gate_ref, w1_ref, w3_ref, w2_ref, g_ref, ga_ref, xo_ref, acc_ref):
    e = pl.program_id(2)
    f = pl.program_id(3)
    h = h_ref[0]
    gates = gate_ref[0]
    lane = lax.broadcasted_iota(jnp.int32, gates.shape, 1)
    ge = jnp.where(lane == e, gates, 0.0).sum(axis=-1, keepdims=True)
    a = _dot(h, w1_ref[0])
    mid = (_silu(a) * _dot(h, w3_ref[0])).astype(BF16)
    contrib = jnp.where(ge > 0.0, ge * _dot(mid, w2_ref[0]), 0.0)
    first = jnp.logical_and(e == 0, f == 0)

    @pl.when(first)
    def _():
        acc_ref[...] = contrib

    @pl.when(jnp.logical_not(first))
    def _():
        acc_ref[...] += contrib

    @pl.when(jnp.logical_and(e == pl.num_programs(2) - 1, f == pl.num_programs(3) - 1))
    def _():
        xo_ref[0] = x_ref[0] + ga_ref[0] * _rms(acc_ref[...], g_ref[...])


def _moe_dense(tok, h, gates, w1, w3, w2, g, ga, tm, fc):
    b, t, d = tok.shape
    ne, _, dff = w1.shape
    row = lambda n: pl.BlockSpec((1, tm, n), lambda bi, ti, ei, fi: (bi, ti, 0))
    return pl.pallas_call(
        _moe_dense_kernel,
        grid=(b, t // tm, ne, dff // fc),
        in_specs=[row(d), row(d), row(LANES),
                  pl.BlockSpec((1, d, fc), lambda bi, ti, ei, fi: (ei, 0, fi)),
                  pl.BlockSpec((1, d, fc), lambda bi, ti, ei, fi: (ei, 0, fi)),
                  pl.BlockSpec((1, fc, d), lambda bi, ti, ei, fi: (ei, fi, 0)),
                  _const_spec((1, d)),
                  pl.BlockSpec((1, 1, d), lambda bi, ti, ei, fi: (bi, 0, 0))],
        out_specs=row(d),
        out_shape=jax.ShapeDtypeStruct((b, t, d), F32),
        scratch_shapes=[pltpu.VMEM((tm, d), F32)],
        compiler_params=_cparams(("parallel", "parallel", "arbitrary", "arbitrary")),
        name="moe_experts",
    )(tok, h, gates, w1, w3, w2, g, ga)


PLANE_W = 256
N_PLANES = D_MODEL // (2 * PLANE_W)
SC_WINDOW = 128


def _pack_planes(h):
    out = []
    for p in range(N_PLANES):
        base = 2 * p * PLANE_W
        hi = pltpu.bitcast(h[:, base:base + PLANE_W].astype(BF16).astype(F32), jnp.uint32)
        lo = pltpu.bitcast(h[:, base + PLANE_W:base + 2 * PLANE_W].astype(BF16).astype(F32), jnp.uint32)
        out.append(hi | (lo >> 16))
    return out


def _unpack_planes(planes):
    cols = []
    for w in planes:
        cols.append(pltpu.bitcast(w & jnp.uint32(0xFFFF0000), F32))
        cols.append(pltpu.bitcast(w << 16, F32))
    return jnp.concatenate(cols, axis=1)


def _route_kernel(x_ref, g_ref, sh_ref, sc_ref, wrh_ref, wrl_ref, hp_ref, im_ref, gm_ref, cnt_ref, carry_ref):
    first = jnp.logical_and(pl.program_id(0) == 0, pl.program_id(1) == 0)

    @pl.when(first)
    def _():
        carry_ref[...] = jnp.zeros_like(carry_ref)

    h = _rms(x_ref[0], g_ref[...]) * (1.0 + sc_ref[0]) + sh_ref[0]
    for p, w in enumerate(_pack_planes(h)):
        hp_ref[p, 0] = w
    logits = _dot3(h, wrh_ref[...], wrl_ref[...])
    tm = logits.shape[0]
    lane = lax.broadcasted_iota(jnp.int32, logits.shape, 1)
    neg = -jnp.inf
    l1 = jnp.where(lane < N_EXPERTS, logits, neg)
    m1 = l1.max(axis=-1, keepdims=True)
    i1 = jnp.where(l1 == m1, lane, LANES).min(axis=-1, keepdims=True)
    l2 = jnp.where(lane == i1, neg, l1)
    m2 = l2.max(axis=-1, keepdims=True)
    i2 = jnp.where(l2 == m2, lane, LANES).min(axis=-1, keepdims=True)
    e = jnp.exp(m2 - m1)
    g1 = 1.0 / (1.0 + e)
    g2 = e / (1.0 + e)
    sel = jnp.logical_or(lane == i1, lane == i2)
    cnt = jnp.where(sel, 1.0, 0.0)
    r = lax.broadcasted_iota(jnp.int32, (tm, tm), 0)
    c = lax.broadcasted_iota(jnp.int32, (tm, tm), 1)
    below = jnp.where(r > c, 1.0, 0.0).astype(BF16)
    before = _dot(below, cnt.astype(BF16)) + carry_ref[...]
    r1 = jnp.where(lane == i1, before, 0.0).sum(axis=-1, keepdims=True).astype(jnp.int32)
    r2 = jnp.where(lane == i2, before, 0.0).sum(axis=-1, keepdims=True).astype(jnp.int32)
    im_ref[0] = jnp.where(lane == 0, i1, jnp.where(lane == 1, i2, jnp.where(lane == 2, r1, r2)))
    gm_ref[0] = jnp.where(lane == 0, g1, g2)
    total = carry_ref[...] + cnt.sum(axis=0, keepdims=True)
    carry_ref[...] = total
    cnt_ref[...] = total


def _route(tok, g, sh, sc, wr_hi, wr_lo, tm):
    b, t, d = tok.shape
    row = lambda n: pl.BlockSpec((1, tm, n), lambda bi, ti: (bi, ti, 0))
    return pl.pallas_call(
        _route_kernel,
        grid=(b, t // tm),
        in_specs=[row(d), _const_spec((1, d)), _batch_row_spec(d), _batch_row_spec(d),
                  _const_spec(wr_hi.shape), _const_spec(wr_lo.shape)],
        out_specs=[pl.BlockSpec((N_PLANES, 1, tm, PLANE_W), lambda bi, ti: (0, bi, ti, 0)),
                   row(LANES), row(LANES), _const_spec((1, LANES))],
        out_shape=[jax.ShapeDtypeStruct((N_PLANES, b, t, PLANE_W), jnp.uint32),
                   jax.ShapeDtypeStruct((b, t, LANES), jnp.int32),
                   jax.ShapeDtypeStruct((b, t, LANES), F32),
                   jax.ShapeDtypeStruct((1, LANES), F32)],
        scratch_shapes=[pltpu.VMEM((1, LANES), F32)],
        compiler_params=_cparams(("arbitrary", "arbitrary")),
        name="moe_route",
    )(tok, g, sh, sc, wr_hi, wr_lo)


def _sc_mesh():
    return plsc.VectorSubcoreMesh(core_axis_name="core", subcore_axis_name="subcore")


def _sc_gather_rows(table, idx):
    n = idx.shape[0]
    w = table.shape[1]

    @pl.kernel(out_type=jax.ShapeDtypeStruct((n, w), table.dtype), mesh=_sc_mesh())
    def gather(t_hbm, i_hbm, o_hbm):
        def body(i_vmem, o_vmem):
            pltpu.sync_copy(t_hbm.at[i_vmem.at[0]], o_vmem)

        pltpu.emit_pipeline(
            body, grid=(n // SC_WINDOW,),
            in_specs=[pl.BlockSpec((1, SC_WINDOW), index_map=lambda i: (0, i))],
            out_specs=[pl.BlockSpec((SC_WINDOW, w), index_map=lambda i: (i, 0))],
            core_axis_name=("core", "subcore"), dimension_semantics=(pltpu.PARALLEL,),
        )(i_hbm, o_hbm)

    return gather(table, idx.reshape(1, n))


def _sc_scatter_rows(src, idx, n_out):
    n = idx.shape[0]
    w = src.shape[1]
    n_src_windows = src.shape[0] // SC_WINDOW

    @pl.kernel(out_type=jax.ShapeDtypeStruct((n_out, w), src.dtype), mesh=_sc_mesh(), scratch_types=[])
    def scatter(s_hbm, i_hbm, o_hbm):
        def body(s_vmem, i_vmem):
            pltpu.sync_copy(s_vmem, o_hbm.at[i_vmem.at[0]])

        pltpu.emit_pipeline(
            body, grid=(n // SC_WINDOW,),
            in_specs=[pl.BlockSpec((SC_WINDOW, w), index_map=lambda i: (i % n_src_windows, 0)),
                      pl.BlockSpec((1, SC_WINDOW), index_map=lambda i: (0, i))],
            out_specs=[],
            core_axis_name=("core", "subcore"), dimension_semantics=(pltpu.PARALLEL,),
        )(s_hbm, i_hbm)

    return scatter(src, idx.reshape(1, n))


def _grouped_ffn_kernel(te_ref, nv_ref, xs_ref, w1_ref, w3_ref, w2_ref, y_ref, hb_ref, acc_ref):
    i = pl.program_id(0)
    f = pl.program_id(1)

    @pl.when(i < nv_ref[0])
    def _():
        @pl.when(f == 0)
        def _():
            hb_ref[...] = _unpack_planes([xs_ref[p] for p in range(N_PLANES)]).astype(BF16)

        h = hb_ref[...]
        a = _dot(h, w1_ref[0])
        mid = (_silu(a) * _dot(h, w3_ref[0])).astype(BF16)
        contrib = _dot(mid, w2_ref[0])

        @pl.when(f == 0)
        def _():
            acc_ref[...] = contrib

        @pl.when(f > 0)
        def _():
            acc_ref[...] += contrib

        @pl.when(f == pl.num_programs(1) - 1)
        def _():
            for p, w in enumerate(_pack_planes(acc_ref[...])):
                y_ref[p] = w


def _grouped_ffn(xs, tile_expert, n_valid, w1, w3, w2, tm, fc):
    n_pad = xs.shape[1]
    ne, d, dff = w1.shape
    plane = pl.BlockSpec((N_PLANES, tm, PLANE_W), lambda i, f, te, nv: (0, i, 0))
    return pl.pallas_call(
        _grouped_ffn_kernel,
        grid_spec=pltpu.PrefetchScalarGridSpec(
            num_scalar_prefetch=2,
            grid=(n_pad // tm, dff // fc),
            in_specs=[plane,
                      pl.BlockSpec((1, d, fc), lambda i, f, te, nv: (te[i], 0, f)),
                      pl.BlockSpec((1, d, fc), lambda i, f, te, nv: (te[i], 0, f)),
                      pl.BlockSpec((1, fc, d), lambda i, f, te, nv: (te[i], f, 0))],
            out_specs=plane,
            scratch_shapes=[pltpu.VMEM((tm, d), BF16), pltpu.VMEM((tm, d), F32)]),
        out_shape=jax.ShapeDtypeStruct(xs.shape, jnp.uint32),
        compiler_params=_cparams(("arbitrary", "arbitrary")),
        name="moe_grouped_ffn",
    )(tile_expert, n_valid, xs, w1, w3, w2)


def _combine_kernel(x_ref, y_ref, gm_ref, g_ref, ga_ref, xo_ref):
    gm = gm_ref[0]
    y1 = _unpack_planes([y_ref[p, 0, 0] for p in range(N_PLANES)])
    y2 = _unpack_planes([y_ref[p, 1, 0] for p in range(N_PLANES)])
    mix = gm[:, 0:1] * y1 + gm[:, 1:2] * y2
    xo_ref[0] = x_ref[0] + ga_ref[0] * _rms(mix, g_ref[...])


def _combine(tok, yg, gm, g, ga, tm):
    b, t, d = tok.shape
    row = lambda n: pl.BlockSpec((1, tm, n), lambda bi, ti: (bi, ti, 0))
    return pl.pallas_call(
        _combine_kernel,
        grid=(b, t // tm),
        in_specs=[row(d),
                  pl.BlockSpec((N_PLANES, 2, 1, tm, PLANE_W), lambda bi, ti: (0, 0, bi, ti, 0)),
                  row(LANES), _const_spec((1, d)), _batch_row_spec(d)],
        out_specs=row(d),
        out_shape=jax.ShapeDtypeStruct((b, t, d), F32),
        compiler_params=_cparams(("parallel", "parallel")),
        name="moe_combine",
    )(tok, yg, gm, g, ga)


def _moe_sorted(tok, g_pre, sh, sc, wr_hi, wr_lo, w1, w3, w2, g_post, ga, tm, tm_e, fc):
    b, t, d = tok.shape
    n_tok = b * t
    hp, im, gm, cnt = _route(tok, g_pre, sh, sc, wr_hi, wr_lo, tm)
    counts = cnt[0, :N_EXPERTS].astype(jnp.int32)
    padded = ((counts + tm_e - 1) // tm_e) * tm_e
    ends = jnp.cumsum(padded)
    starts = ends - padded
    n_pad = 2 * n_tok + N_EXPERTS * tm_e
    n_tiles = n_pad // tm_e
    tile_expert = jnp.minimum(
        jnp.sum((jnp.arange(n_tiles, dtype=jnp.int32)[:, None] * tm_e >= ends[None, :]).astype(jnp.int32), axis=1),
        N_EXPERTS - 1).astype(jnp.int32)
    n_valid = (ends[-1:] // tm_e).astype(jnp.int32)
    im2 = im.reshape(n_tok, LANES)
    pos = jnp.stack([starts[im2[:, 0]] + im2[:, 2], starts[im2[:, 1]] + im2[:, 3]], axis=0)
    plane_off = (jnp.arange(N_PLANES, dtype=jnp.int32) * n_pad)
    idx_dispatch = (pos[:, None, :] + plane_off[None, :, None]).reshape(-1)
    idx_return = (pos[None, :, :] + plane_off[:, None, None]).reshape(-1)
    xs = _sc_scatter_rows(hp.reshape(N_PLANES * n_tok, PLANE_W), idx_dispatch, N_PLANES * n_pad)
    ys = _grouped_ffn(xs.reshape(N_PLANES, n_pad, PLANE_W), tile_expert, n_valid, w1, w3, w2, tm_e, fc)
    yg = _sc_gather_rows(ys.reshape(N_PLANES * n_pad, PLANE_W), idx_return)
    return _combine(tok, yg.reshape(N_PLANES, 2, b, t, PLANE_W), gm, g_post, ga, tm)


_ROPE_SWAP = np.concatenate([np.arange(16, 32), np.arange(0, 16), np.arange(48, 64), np.arange(32, 48)])


def _rope_tables(n_tok):
    rows = n_tok // GRID_W
    row = jnp.broadcast_to(jnp.arange(rows, dtype=F32)[:, None], (rows, GRID_W)).reshape(-1)
    col = jnp.broadcast_to(jnp.arange(GRID_W, dtype=F32)[None, :], (rows, GRID_W)).reshape(-1)
    half = MLA_ROPE // 2
    inv = 1.0 / (ROPE_BASE ** (jnp.arange(0, half, 2, dtype=F32) / half))
    cr, sr = jnp.cos(row[:, None] * inv), jnp.sin(row[:, None] * inv)
    cc, sc = jnp.cos(col[:, None] * inv), jnp.sin(col[:, None] * inv)
    cos64 = jnp.concatenate([cr, cr, cc, cc], axis=-1)
    sin64 = jnp.concatenate([-sr, sr, -sc, sc], axis=-1)
    return (jnp.tile(cos64, (1, MLA_HEADS)), jnp.tile(sin64, (1, MLA_HEADS)), cos64.T, sin64.T)


def _identity_rope_tables(n_tok):
    one = jnp.ones((n_tok, MLA_ROPE), F32)
    zero = jnp.zeros((n_tok, MLA_ROPE), F32)
    return (jnp.tile(one, (1, MLA_HEADS)), jnp.tile(zero, (1, MLA_HEADS)), one.T, zero.T)


def _even_weights(w_in, w_gate_f, b_gate_f, w_gate_b, b_gate_b, q_norm, w_uq, kv_norm, w_ukv):
    nqk = GLA_HEADS * GLA_DK
    nv = GLA_HEADS * GLA_DV
    o_z = 2 * nqk + nv
    o_r = o_z + 2 * GLA_RANK
    o_cq = o_r + nv
    o_kv = o_cq + MLA_Q_RANK
    o_kr = o_kv + MLA_KV_RANK
    d = w_in.shape[0]
    wp = jnp.concatenate([w_in[:, :o_z], w_in[:, o_r:o_kr], w_in[:, o_z:o_r],
                          jnp.zeros((d, LANES - 2 * GLA_RANK), F32)], axis=1).astype(BF16)
    kr = w_in[:, o_kr:o_kr + MLA_ROPE]
    wkr = jnp.concatenate([kr, kr[:, _ROPE_SWAP]], axis=1).T.astype(BF16)
    wg = jnp.zeros((LANES, 2 * nqk), F32)
    wg = wg.at[0:GLA_RANK, 0:nqk].set(w_gate_f).at[GLA_RANK:2 * GLA_RANK, nqk:].set(w_gate_b)
    wg_hi = wg.astype(BF16)
    wg_lo = (wg - wg_hi.astype(F32)).astype(BF16)
    bg = jnp.concatenate([b_gate_f, b_gate_b])[None, :]
    hq = np.arange(MLA_HEADS)[:, None] * MLA_QK
    nope_idx = (hq + np.arange(MLA_NOPE)[None, :]).reshape(-1)
    rope_idx = (hq + MLA_NOPE + np.arange(MLA_ROPE)[None, :]).reshape(-1)
    swap_idx = (hq + MLA_NOPE + _ROPE_SWAP[None, :]).reshape(-1)
    wuq = w_uq[:, np.concatenate([nope_idx, rope_idx, swap_idx])].astype(BF16)
    hk = np.arange(MLA_HEADS)[:, None] * (MLA_NOPE + MLA_V)
    k_idx = (hk + np.arange(MLA_NOPE)[None, :]).reshape(-1)
    v_idx = (hk + MLA_NOPE + np.arange(MLA_V)[None, :]).reshape(-1)
    return dict(wp=wp, wkr=wkr, wg_hi=wg_hi, wg_lo=wg_lo, bg=bg, qn=q_norm[None, :], wuq=wuq,
                kvn=kv_norm[None, :], wuk=w_ukv[:, k_idx].T.astype(BF16), wuv=w_ukv[:, v_idx].astype(BF16))


def _mods(m, rows, batch):
    d = D_MODEL
    if rows is None:
        return [jnp.broadcast_to(m[batch, k * d:(k + 1) * d][None, None, :], (batch, 1, d)) for k in range(6)]
    return [m[:batch, k * d:(k + 1) * d][:, None, :] for k in range(6)]


def kernel(x, c, ctx, c_ctx, w_mod, b_mod, g_mix_pre, g_mix_post, g_ffn_pre, g_ffn_post, e_w_in, e_w_gate_f, e_b_gate_f, e_w_gate_b, e_b_gate_b, e_gla_norm, e_q_norm, e_w_uq, e_kv_norm, e_w_ukv, e_w_o, e_w1, e_w3, e_w2, o_w_o, o_w_router, o_w1, o_w3, o_w2):
    batch, seq, d = x.shape
    n_ctx = ctx.shape[1]
    cond = jnp.zeros((16, d), F32).at[:batch].set(c).at[batch].set(c_ctx)
    mods = _adaln(cond, w_mod, b_mod)
    rope_x = _rope_tables(seq)
    rope_c = _identity_rope_tables(n_ctx)
    nqk = GLA_HEADS * GLA_DK
    last_read = 2 * ((DEPTH - 1) // 2)
    xs, xc = x, ctx
    for i in range(DEPTH):
        j = i // 2
        ctx_live = i <= last_read
        ctx_full = i < last_read
        mx = _mods(mods[i], 0, batch)
        mc = _mods(mods[i], None, batch)
        gpre, gpost = g_mix_pre[i][None, :], g_mix_post[i][None, :]
        fpre, fpost = g_ffn_pre[i][None, :], g_ffn_post[i][None, :]
        if i % 2 == 0:
            w = _even_weights(e_w_in[j], e_w_gate_f[j], e_b_gate_f[j], e_w_gate_b[j], e_b_gate_b[j],
                              e_q_norm[j], e_w_uq[j], e_kv_norm[j], e_w_ukv[j])
            wo1 = e_w_o[j][:GLA_HEADS * GLA_DV].astype(BF16)
            wo2 = e_w_o[j][GLA_HEADS * GLA_DV:].astype(BF16)
            gn = e_gla_norm[j][None, :]
            w1, w3, w2 = e_w1[j].astype(BF16), e_w3[j].astype(BF16), e_w2[j].astype(BF16)
            zero_state = jnp.zeros((batch, GLA_DV, nqk), F32)
            if ctx_live:
                cgq, cgk, cgv, cgr, claf, clab, cq, ckt, cv = _in_proj(xc, gpre, mc[0], mc[1], w, rope_c, n_ctx)
                co_f, co_b, s_f, s_b = _gla(cgq, cgk, cgv, claf, clab, zero_state, zero_state, n_ctx)
                srcs_c = [(ckt, cv)]
            else:
                s_f = s_b = zero_state
                srcs_c = []
            gq, gk, gv, gr, laf, lab, q, kt, v = _in_proj(xs, gpre, mx[0], mx[1], w, rope_x, min(512, seq))
            o_f, o_b, _, _ = _gla(gq, gk, gv, laf, lab, s_f, s_b, min(512, seq))
            a = _attention(q, [(kt, v)] + srcs_c, min(256, seq))
            xs, hx = _even_out(xs, o_f, o_b, gr, a, gn, wo1, wo2, gpost, mx[2], fpre, mx[3], mx[4], min(512, seq))
            xs = _ffn(xs, hx, w1, w3, w2, fpost, mx[5], min(512, seq), 1408)
            if ctx_full:
                ac = _attention(cq, srcs_c, n_ctx)
                xc, hc = _even_out(xc, co_f, co_b, cgr, ac, gn, wo1, wo2, gpost, mc[2], fpre, mc[3], mc[4], n_ctx)
                xc = _ffn(xc, hc, w1, w3, w2, fpost, mc[5], n_ctx, 1408)
        else:
            wo = o_w_o[j].astype(BF16)
            wr = jnp.zeros((d, LANES), F32).at[:, :N_EXPERTS].set(o_w_router[j])
            wr_hi = wr.astype(BF16)
            wr_lo = (wr - wr_hi.astype(F32)).astype(BF16)
            w1, w3, w2 = o_w1[j].astype(BF16), o_w3[j].astype(BF16), o_w2[j].astype(BF16)
            xs = _fourier_x(xs, gpre, mx[0], mx[1], wo, gpost, mx[2])
            xs = _moe_sorted(xs, fpre, mx[3], mx[4], wr_hi, wr_lo, w1, w3, w2, fpost, mx[5],
                             min(512, seq), min(1024, seq), 896)
            if ctx_full:
                xc = _fourier_ctx(xc, gpre, mc[0], mc[1], wo, gpost, mc[2])
                hc, gates_c = _router(xc, fpre, mc[3], mc[4], wr_hi, wr_lo, n_ctx)
                xc = _moe_dense(xc, hc, gates_c, w1, w3, w2, fpost, mc[5], n_ctx, 1792)
    return xs
```

```python
import functools

import numpy as np
import jax
import jax.numpy as jnp
from jax import lax
from jax.experimental import pallas as pl
from jax.experimental.pallas import tpu as pltpu
from jax.experimental.pallas import tpu_sc as plsc

F32 = jnp.float32
BF16 = jnp.bfloat16

EPS = 1e-6
D_MODEL = 1024
DEPTH = 4
GRID_W = 64
GLA_HEADS = 4
GLA_DK = 64
GLA_DV = 128
GLA_RANK = 16
GLA_TAU = 16.0
GLA_CHUNK = 64
MLA_HEADS = 4
MLA_Q_RANK = 256
MLA_KV_RANK = 128
MLA_NOPE = 128
MLA_ROPE = 64
MLA_V = 128
MLA_QK = MLA_NOPE + MLA_ROPE
ROPE_BASE = 10000.0
FOURIER_GROUPS = 4
FOURIER_GW = D_MODEL // FOURIER_GROUPS
N_EXPERTS = 8
LANES = 128
VMEM_LIMIT = 48 * 1024 * 1024


def _cparams(sem):
    return pltpu.CompilerParams(dimension_semantics=sem, vmem_limit_bytes=VMEM_LIMIT)


def _dot(a, b):
    return jnp.dot(a, b, preferred_element_type=F32)


def _dot_nt(a, b):
    return lax.dot_general(a, b, (((1,), (1,)), ((), ())), preferred_element_type=F32)


def _dot_tn(a, b):
    return lax.dot_general(a, b, (((0,), (0,)), ((), ())), preferred_element_type=F32)


def _split(x):
    hi = x.astype(BF16)
    lo = (x - hi.astype(F32)).astype(BF16)
    return hi, lo


def _dot3(a, b_hi, b_lo):
    a_hi, a_lo = _split(a)
    return _dot(a_hi, b_hi) + _dot(a_lo, b_hi) + _dot(a_hi, b_lo)


def _rms(x, g):
    return x * lax.rsqrt(jnp.mean(x * x, axis=-1, keepdims=True) + EPS) * g


def _silu(x):
    return x / (1.0 + jnp.exp(-x))


def _const_spec(shape):
    nd = len(shape)
    return pl.BlockSpec(shape, lambda *_: (0,) * nd)


def _batch_row_spec(d):
    return pl.BlockSpec((1, 1, d), lambda b, *_: (b, 0, 0))


def _adaln_kernel(c_ref, w_ref, b_ref, o_ref):
    a = _silu(c_ref[...])
    w_hi, w_lo = _split(w_ref[0])
    o_ref[0] = _dot3(a, w_hi, w_lo) + b_ref[0]


def _adaln(cond, w_mod, b_mod):
    depth, d, n = w_mod.shape
    rows = cond.shape[0]
    bn = 1536
    return pl.pallas_call(
        _adaln_kernel,
        grid=(depth, n // bn),
        in_specs=[
            pl.BlockSpec((rows, d), lambda i, j: (0, 0)),
            pl.BlockSpec((1, d, bn), lambda i, j: (i, 0, j)),
            pl.BlockSpec((1, 1, bn), lambda i, j: (i, 0, j)),
        ],
        out_specs=pl.BlockSpec((1, rows, bn), lambda i, j: (i, 0, j)),
        out_shape=jax.ShapeDtypeStruct((depth, rows, n), F32),
        compiler_params=_cparams(("arbitrary", "arbitrary")),
        name="adaln",
    )(cond, w_mod, b_mod.reshape(depth, 1, n))


def _in_proj_kernel(x_ref, g_ref, sh_ref, sc_ref, wp_ref, wkr_ref, wgh_ref, wgl_ref, bg_ref,
                    qn_ref, wuq_ref, kvn_ref, wuk_ref, wuv_ref, cq_ref, sq_ref, ck_ref, sk_ref,
                    gq_ref, gk_ref, gv_ref, gr_ref, laf_ref, lab_ref, q_ref, kt_ref, v_ref):
    h = _rms(x_ref[0], g_ref[...]) * (1.0 + sc_ref[0]) + sh_ref[0]
    hb = h.astype(BF16)
    p = _dot(hb, wp_ref[...])
    nqk = GLA_HEADS * GLA_DK
    nv = GLA_HEADS * GLA_DV
    gq_ref[0] = p[:, 0:nqk] * (GLA_DK ** -0.5)
    gk_ref[0] = p[:, nqk:2 * nqk]
    gv_ref[0] = p[:, 2 * nqk:2 * nqk + nv]
    gr_ref[0] = p[:, 2 * nqk + nv:2 * nqk + 2 * nv]
    o = 2 * nqk + 2 * nv
    cq = p[:, o:o + MLA_Q_RANK]
    ckv = p[:, o + MLA_Q_RANK:o + MLA_Q_RANK + MLA_KV_RANK]
    tail = p[:, o + MLA_Q_RANK + MLA_KV_RANK:]
    pre = _dot3(tail, wgh_ref[...], wgl_ref[...]) + bg_ref[...]
    la = (jnp.minimum(pre, 0.0) - jnp.log(1.0 + jnp.exp(-jnp.abs(pre)))) * (1.0 / GLA_TAU)
    laf_ref[0] = la[:, :nqk]
    lab_ref[0] = la[:, nqk:]
    q = _dot(_rms(cq, qn_ref[...]).astype(BF16), wuq_ref[...])
    att_scale = MLA_QK ** -0.5
    nn = MLA_HEADS * MLA_NOPE
    nr = MLA_HEADS * MLA_ROPE
    q_rope = q[:, nn:nn + nr] * cq_ref[...] + q[:, nn + nr:] * sq_ref[...]
    for hd in range(MLA_HEADS):
        q_ref[0, hd, :, 0:MLA_NOPE] = (q[:, hd * MLA_NOPE:(hd + 1) * MLA_NOPE] * att_scale).astype(BF16)
        q_ref[0, hd, :, MLA_NOPE:MLA_QK] = (q_rope[:, hd * MLA_ROPE:(hd + 1) * MLA_ROPE] * att_scale).astype(BF16)
    ckvn = _rms(ckv, kvn_ref[...]).astype(BF16)
    kt = _dot_nt(wuk_ref[...], ckvn)
    v = _dot(ckvn, wuv_ref[...])
    kr2 = _dot_nt(wkr_ref[...], hb)
    kr = (kr2[:MLA_ROPE] * ck_ref[...] + kr2[MLA_ROPE:] * sk_ref[...]).astype(BF16)
    for hd in range(MLA_HEADS):
        kt_ref[0, hd, 0:MLA_NOPE, :] = kt[hd * MLA_NOPE:(hd + 1) * MLA_NOPE].astype(BF16)
        kt_ref[0, hd, MLA_NOPE:MLA_QK, :] = kr
        v_ref[0, hd] = v[:, hd * MLA_V:(hd + 1) * MLA_V].astype(BF16)


def _in_proj(tok, g, sh, sc, w, tabs, tm):
    b, t, d = tok.shape
    cq, sq, ck, sk = tabs
    nqk = GLA_HEADS * GLA_DK
    nv = GLA_HEADS * GLA_DV
    row = lambda n: pl.BlockSpec((1, tm, n), lambda bi, ti: (bi, ti, 0))
    weights = (w["wp"], w["wkr"], w["wg_hi"], w["wg_lo"], w["bg"], w["qn"], w["wuq"], w["kvn"],
               w["wuk"], w["wuv"])
    return pl.pallas_call(
        _in_proj_kernel,
        grid=(b, t // tm),
        in_specs=[row(d), _const_spec((1, d)), _batch_row_spec(d), _batch_row_spec(d)]
        + [_const_spec(a.shape) for a in weights]
        + [pl.BlockSpec((tm, MLA_HEADS * MLA_ROPE), lambda bi, ti: (ti, 0)),
           pl.BlockSpec((tm, MLA_HEADS * MLA_ROPE), lambda bi, ti: (ti, 0)),
           pl.BlockSpec((MLA_ROPE, tm), lambda bi, ti: (0, ti)),
           pl.BlockSpec((MLA_ROPE, tm), lambda bi, ti: (0, ti))],
        out_specs=[row(nqk), row(nqk), row(nv), row(nv), row(nqk), row(nqk),
                   pl.BlockSpec((1, MLA_HEADS, tm, MLA_QK), lambda bi, ti: (bi, 0, ti, 0)),
                   pl.BlockSpec((1, MLA_HEADS, MLA_QK, tm), lambda bi, ti: (bi, 0, 0, ti)),
                   pl.BlockSpec((1, MLA_HEADS, tm, MLA_V), lambda bi, ti: (bi, 0, ti, 0))],
        out_shape=[jax.ShapeDtypeStruct((b, t, nqk), F32), jax.ShapeDtypeStruct((b, t, nqk), F32),
                   jax.ShapeDtypeStruct((b, t, nv), F32), jax.ShapeDtypeStruct((b, t, nv), F32),
                   jax.ShapeDtypeStruct((b, t, nqk), F32), jax.ShapeDtypeStruct((b, t, nqk), F32),
                   jax.ShapeDtypeStruct((b, MLA_HEADS, t, MLA_QK), BF16),
                   jax.ShapeDtypeStruct((b, MLA_HEADS, MLA_QK, t), BF16),
                   jax.ShapeDtypeStruct((b, MLA_HEADS, t, MLA_V), BF16)],
        compiler_params=_cparams(("parallel", "parallel")),
        name="even_in_proj",
    )(tok, g, sh, sc, *weights, cq, sq, ck, sk)


def _gla_chunk(q_ref, k_ref, v_ref, l_ref, o_ref, st_ref, row0, tri, mask, last_row):
    c = GLA_CHUNK
    la = l_ref[0, pl.ds(row0, c), :]
    la_hi, la_lo = _split(la)
    bc = _dot(tri, la_hi) + _dot(tri, la_lo)
    bl = bc[last_row:last_row + 1, :]
    q = q_ref[0, pl.ds(row0, c), :]
    k = k_ref[0, pl.ds(row0, c), :]
    qc = (q * jnp.exp(bc)).astype(BF16)
    kc = (k * jnp.exp(-bc)).astype(BF16)
    kd = (k * jnp.exp(bl - bc)).astype(BF16)
    vb = v_ref[0, pl.ds(row0, c), :].astype(BF16)
    st = st_ref[...]
    stb = st.astype(BF16)
    outs, upds = [], []
    for hd in range(GLA_HEADS):
        ks = slice(hd * GLA_DK, (hd + 1) * GLA_DK)
        vs = slice(hd * GLA_DV, (hd + 1) * GLA_DV)
        att = jnp.where(mask, _dot_nt(qc[:, ks], kc[:, ks]), 0.0).astype(BF16)
        outs.append(_dot(att, vb[:, vs]) + _dot_nt(qc[:, ks], stb[:, ks]))
        upds.append(_dot_tn(vb[:, vs], kd[:, ks]))
    o_ref[0, pl.ds(row0, c), :] = jnp.concatenate(outs, axis=1)
    st_ref[...] = st * jnp.exp(bl) + jnp.concatenate(upds, axis=1)


def _gla_kernel(qf, kf, vf, lf, qb, kb, vb, lb, s0f, s0b, of, ob, sff, sfb, stf, stb, *, nc):
    j = pl.program_id(1)

    @pl.when(j == 0)
    def _():
        stf[...] = s0f[0]
        stb[...] = s0b[0]

    c = GLA_CHUNK
    r = lax.broadcasted_iota(jnp.int32, (c, c), 0)
    cc = lax.broadcasted_iota(jnp.int32, (c, c), 1)
    lower = r >= cc
    upper = r <= cc
    tri_l = jnp.where(lower, 1.0, 0.0).astype(BF16)
    tri_u = jnp.where(upper, 1.0, 0.0).astype(BF16)

    for ci in range(nc):
        _gla_chunk(qf, kf, vf, lf, of, stf, ci * c, tri_l, lower, c - 1)
        _gla_chunk(qb, kb, vb, lb, ob, stb, (nc - 1 - ci) * c, tri_u, upper, 0)

    @pl.when(j == pl.num_programs(1) - 1)
    def _():
        sff[0] = stf[...]
        sfb[0] = stb[...]


def _gla(gq, gk, gv, laf, lab, s0f, s0b, tb):
    b, t, nqk = gq.shape
    nv = gv.shape[-1]
    nblk = t // tb
    fwd = lambda n: pl.BlockSpec((1, tb, n), lambda bi, j: (bi, j, 0))
    bwd = lambda n: pl.BlockSpec((1, tb, n), lambda bi, j: (bi, nblk - 1 - j, 0))
    st = pl.BlockSpec((1, GLA_DV, nqk), lambda bi, j: (bi, 0, 0))
    return pl.pallas_call(
        functools.partial(_gla_kernel, nc=tb // GLA_CHUNK),
        grid=(b, nblk),
        in_specs=[fwd(nqk), fwd(nqk), fwd(nv), fwd(nqk), bwd(nqk), bwd(nqk), bwd(nv), bwd(nqk), st, st],
        out_specs=[fwd(nv), bwd(nv), st, st],
        out_shape=[jax.ShapeDtypeStruct((b, t, nv), F32), jax.ShapeDtypeStruct((b, t, nv), F32),
                   jax.ShapeDtypeStruct((b, GLA_DV, nqk), F32), jax.ShapeDtypeStruct((b, GLA_DV, nqk), F32)],
        scratch_shapes=[pltpu.VMEM((GLA_DV, nqk), F32), pltpu.VMEM((GLA_DV, nqk), F32)],
        compiler_params=_cparams(("parallel", "arbitrary")),
        name="gla_scan",
    )(gq, gk, gv, laf, gq, gk, gv, lab, s0f, s0b)


ATTN_KEY_CHUNK = 1024


def _attn_kernel(*refs, n_src):
    q_ref = refs[0]
    kts = refs[1:1 + 2 * n_src:2]
    vs = refs[2:2 + 2 * n_src:2]
    o_ref = refs[1 + 2 * n_src]
    vexts = refs[2 + 2 * n_src:]

    @pl.when(pl.program_id(2) == 0)
    def _():
        for v, vext in zip(vs, vexts):
            tk = v.shape[2]
            lane = lax.broadcasted_iota(jnp.int32, (tk, LANES), 1)
            vext[:, 0:MLA_V] = v[0, 0]
            vext[:, MLA_V:MLA_V + LANES] = jnp.where(lane == 0, 1.0, 0.0).astype(BF16)

    q = q_ref[0, 0]
    bq = q.shape[0]
    m = jnp.full((bq, 1), -jnp.inf, F32)
    acc = jnp.zeros((bq, MLA_V + LANES), F32)
    for kt, vext in zip(kts, vexts):
        tk = kt.shape[3]
        ck = min(ATTN_KEY_CHUNK, tk)
        for c in range(tk // ck):
            s = _dot(q, kt[0, 0, :, c * ck:(c + 1) * ck])
            m_new = jnp.maximum(m, s.max(axis=-1, keepdims=True))
            p = jnp.exp((s - m_new).astype(BF16))
            acc = acc * jnp.exp(m - m_new) + _dot(p, vext[c * ck:(c + 1) * ck, :])
            m = m_new
    o_ref[0] = (acc[:, 0:MLA_V] / acc[:, MLA_V:MLA_V + 1]).astype(o_ref.dtype)


def _attention(q, srcs, bq):
    b, nh, t, dqk = q.shape
    in_specs = [pl.BlockSpec((1, 1, bq, dqk), lambda bi, hi, qi: (bi, hi, qi, 0))]
    args = [q]
    scratch = []
    for kt, v in srcs:
        tk = kt.shape[-1]
        in_specs.append(pl.BlockSpec((1, 1, dqk, tk), lambda bi, hi, qi: (bi, hi, 0, 0)))
        in_specs.append(pl.BlockSpec((1, 1, tk, MLA_V), lambda bi, hi, qi: (bi, hi, 0, 0)))
        args += [kt, v]
        scratch.append(pltpu.VMEM((tk, MLA_V + LANES), BF16))
    return pl.pallas_call(
        functools.partial(_attn_kernel, n_src=len(srcs)),
        grid=(b, nh, t // bq),
        in_specs=in_specs,
        out_specs=pl.BlockSpec((1, bq, MLA_V), lambda bi, hi, qi: (bi, qi, hi)),
        out_shape=jax.ShapeDtypeStruct((b, t, nh * MLA_V), BF16),
        scratch_shapes=scratch,
        compiler_params=_cparams(("parallel", "parallel", "arbitrary")),
        name="mla_attention",
    )(*args)


def _even_out_kernel(x_ref, of_ref, ob_ref, gr_ref, a_ref, gn_ref, wo1_ref, wo2_ref, gpost_ref, ga_ref,
                     gpre_ref, sh_ref, sc_ref, xo_ref, h_ref):
    o = of_ref[0] + ob_ref[0]
    parts = [_rms(o[:, hd * GLA_DV:(hd + 1) * GLA_DV], gn_ref[...]) for hd in range(GLA_HEADS)]
    fin = jnp.concatenate(parts, axis=1) * _silu(gr_ref[0])
    y = _dot(fin.astype(BF16), wo1_ref[...]) + _dot(a_ref[0], wo2_ref[...])
    xn = x_ref[0] + ga_ref[0] * _rms(y, gpost_ref[...])
    xo_ref[0] = xn
    h_ref[0] = (_rms(xn, gpre_ref[...]) * (1.0 + sc_ref[0]) + sh_ref[0]).astype(BF16)


def _even_out(tok, o_f, o_b, gr, a, gn, wo1, wo2, gpost, ga, gpre, sh, sc, tm):
    b, t, d = tok.shape
    nv = o_f.shape[-1]
    row = lambda n: pl.BlockSpec((1, tm, n), lambda bi, ti: (bi, ti, 0))
    return pl.pallas_call(
        _even_out_kernel,
        grid=(b, t // tm),
        in_specs=[row(d), row(nv), row(nv), row(nv), row(a.shape[-1]), _const_spec(gn.shape),
                  _const_spec(wo1.shape), _const_spec(wo2.shape), _const_spec((1, d)), _batch_row_spec(d),
                  _const_spec((1, d)), _batch_row_spec(d), _batch_row_spec(d)],
        out_specs=[row(d), row(d)],
        out_shape=[jax.ShapeDtypeStruct((b, t, d), F32), jax.ShapeDtypeStruct((b, t, d), BF16)],
        compiler_params=_cparams(("parallel", "parallel")),
        name="even_out_proj",
    )(tok, o_f, o_b, gr, a, gn, wo1, wo2, gpost, ga, gpre, sh, sc)


MXU_COLS = 256


def _ff_chunks(dff, max_tiles):
    n_tiles = dff // MXU_COLS
    n_chunks = -(-n_tiles // max_tiles)
    base, extra = divmod(n_tiles, n_chunks)
    bounds, start = [], 0
    for i in range(n_chunks):
        width = (base + (1 if i < extra else 0)) * MXU_COLS
        bounds.append((start, start + width))
        start += width
    return bounds


def _ffn_kernel(x_ref, h_ref, w1_ref, w3_ref, w2_ref, g_ref, ga_ref, xo_ref, *, chunks):
    h = h_ref[0]
    y = None
    for lo, hi in chunks:
        a = _dot(h, w1_ref[:, lo:hi])
        mid = (_silu(a) * _dot(h, w3_ref[:, lo:hi])).astype(BF16)
        part = _dot(mid, w2_ref[lo:hi, :])
        y = part if y is None else y + part
    xo_ref[0] = x_ref[0] + ga_ref[0] * _rms(y, g_ref[...])


def _ffn(tok, h, w1, w3, w2, g, ga, tm):
    b, t, d = tok.shape
    dff = w1.shape[-1]
    row = lambda: pl.BlockSpec((1, tm, d), lambda bi, ti: (bi, ti, 0))
    resident = lambda shape: pl.BlockSpec(shape, lambda bi, ti: (0, 0), pipeline_mode=pl.Buffered(1))
    return pl.pallas_call(
        functools.partial(_ffn_kernel, chunks=_ff_chunks(dff, 6)),
        grid=(b, t // tm),
        in_specs=[row(), row(), resident((d, dff)), resident((d, dff)), resident((dff, d)),
                  _const_spec((1, d)), _batch_row_spec(d)],
        out_specs=row(),
        out_shape=jax.ShapeDtypeStruct((b, t, d), F32),
        compiler_params=_cparams(("parallel", "parallel")),
        name="swiglu_ffn",
    )(tok, h, w1, w3, w2, g, ga)


def _dft_cos_sin(n):
    idx = (np.arange(n)[:, None] * np.arange(n)[None, :]) % n
    ang = 2.0 * np.pi * idx.astype(np.float64) / n
    return np.cos(ang), np.sin(ang)


def _fourier1_kernel(x_ref, g_ref, sh_ref, sc_ref, f1_ref, tc_ref, ts_ref, o_ref, *, nb, n1):
    d = D_MODEL
    for j in range(nb):
        h = _rms(x_ref[0, :, j, :], g_ref[...]) * (1.0 + sc_ref[0]) + sh_ref[0]
        a = _dot(f1_ref[...], h.astype(BF16))
        ar, ai = a[:n1], a[n1:]
        tc = jnp.concatenate([tc_ref[j]] * (d // LANES), axis=1)
        ts = jnp.concatenate([ts_ref[j]] * (d // LANES), axis=1)
        o_ref[0, 0, j] = (ar * tc - ai * ts).astype(BF16)
        o_ref[0, 1, j] = (ar * ts + ai * tc).astype(BF16)


def _fourier2_kernel(b_ref, x_ref, f2_ref, cc_ref, sc_ref, wo_ref, gpost_ref, ga_ref, xo_ref, *, kb, n2):
    urs, uis = [], []
    for j in range(kb):
        bj = jnp.concatenate([b_ref[0, 0, :, j, :], b_ref[0, 1, :, j, :]], axis=0)
        u = _dot(f2_ref[...], bj)
        urs.append(u[:n2])
        uis.append(u[n2:])
    ur = jnp.concatenate(urs, axis=0).astype(BF16)
    ui = jnp.concatenate(uis, axis=0).astype(BF16)
    gw = FOURIER_GW
    f = jnp.concatenate(
        [_dot(ur[:, g * gw:(g + 1) * gw], cc_ref[...]) + _dot(ui[:, g * gw:(g + 1) * gw], sc_ref[...])
         for g in range(FOURIER_GROUPS)], axis=1)
    y = _dot(f.astype(BF16), wo_ref[...])
    yn = ga_ref[0] * _rms(y, gpost_ref[...])
    for j in range(kb):
        xo_ref[0, :, j, :] = x_ref[0, :, j, :] + yn[j * n2:(j + 1) * n2]


def _fourier_x(tok, g, sh, sc, wo, gpost, ga):
    b, t, d = tok.shape
    n1, n2 = 128, t // 128
    nb, kb = min(8, n2), 8
    c1, s1 = _dft_cos_sin(n1)
    c2, s2 = _dft_cos_sin(n2)
    cg, sg = _dft_cos_sin(FOURIER_GW)
    f1 = jnp.asarray(np.concatenate([c1, -s1], axis=0) / np.sqrt(n1), BF16)
    f2 = jnp.asarray(np.block([[c2, s2], [-s2, c2]]) / np.sqrt(n2), BF16)
    ccg = jnp.asarray(cg / np.sqrt(FOURIER_GW), BF16)
    scg = jnp.asarray(sg / np.sqrt(FOURIER_GW), BF16)
    tw = 2.0 * np.pi * ((np.arange(n2)[:, None] * np.arange(n1)[None, :]) % t).astype(np.float64) / t
    tc = jnp.asarray(np.broadcast_to(np.cos(tw)[:, :, None], (n2, n1, LANES)), F32)
    ts = jnp.asarray(np.broadcast_to(-np.sin(tw)[:, :, None], (n2, n1, LANES)), F32)
    stage1 = pl.pallas_call(
        functools.partial(_fourier1_kernel, nb=nb, n1=n1),
        grid=(b, n2 // nb),
        in_specs=[pl.BlockSpec((1, n1, nb, d), lambda bi, ji: (bi, 0, ji, 0)),
                  _const_spec((1, d)), _batch_row_spec(d), _batch_row_spec(d),
                  _const_spec(f1.shape),
                  pl.BlockSpec((nb, n1, LANES), lambda bi, ji: (ji, 0, 0)),
                  pl.BlockSpec((nb, n1, LANES), lambda bi, ji: (ji, 0, 0))],
        out_specs=pl.BlockSpec((1, 2, nb, n1, d), lambda bi, ji: (bi, 0, ji, 0, 0)),
        out_shape=jax.ShapeDtypeStruct((b, 2, n2, n1, d), BF16),
        compiler_params=_cparams(("parallel", "parallel")),
        name="fourier_stage1",
    )(tok.reshape(b, n1, n2, d), g, sh, sc, f1, tc, ts)
    out = pl.pallas_call(
        functools.partial(_fourier2_kernel, kb=kb, n2=n2),
        grid=(b, n1 // kb),
        in_specs=[pl.BlockSpec((1, 2, n2, kb, d), lambda bi, ki: (bi, 0, 0, ki, 0)),
                  pl.BlockSpec((1, n2, kb, d), lambda bi, ki: (bi, 0, ki, 0)),
                  _const_spec(f2.shape), _const_spec(ccg.shape), _const_spec(scg.shape),
                  _const_spec(wo.shape), _const_spec((1, d)), _batch_row_spec(d)],
        out_specs=pl.BlockSpec((1, n2, kb, d), lambda bi, ki: (bi, 0, ki, 0)),
        out_shape=jax.ShapeDtypeStruct((b, n2, n1, d), F32),
        compiler_params=_cparams(("parallel", "parallel")),
        name="fourier_stage2",
    )(stage1, tok.reshape(b, n2, n1, d), f2, ccg, scg, wo, gpost, ga)
    return out.reshape(b, t, d)


def _fourier_ctx_kernel(x_ref, g_ref, sh_ref, sc_ref, fl_ref, cc_ref, scg_ref, wo_ref, gpost_ref, ga_ref,
                        xo_ref):
    x = x_ref[0]
    t = x.shape[0]
    h = _rms(x, g_ref[...]) * (1.0 + sc_ref[0]) + sh_ref[0]
    u = _dot(fl_ref[...], h.astype(BF16))
    ur = u[:t].astype(BF16)
    ui = u[t:].astype(BF16)
    gw = FOURIER_GW
    f = jnp.concatenate(
        [_dot(ur[:, g * gw:(g + 1) * gw], cc_ref[...]) + _dot(ui[:, g * gw:(g + 1) * gw], scg_ref[...])
         for g in range(FOURIER_GROUPS)], axis=1)
    y = _dot(f.astype(BF16), wo_ref[...])
    xo_ref[0] = x + ga_ref[0] * _rms(y, gpost_ref[...])


def _fourier_ctx(tok, g, sh, sc, wo, gpost, ga):
    b, t, d = tok.shape
    cl, sl = _dft_cos_sin(t)
    cg, sg = _dft_cos_sin(FOURIER_GW)
    fl = jnp.asarray(np.concatenate([cl, -sl], axis=0) / np.sqrt(t), BF16)
    ccg = jnp.asarray(cg / np.sqrt(FOURIER_GW), BF16)
    scg = jnp.asarray(sg / np.sqrt(FOURIER_GW), BF16)
    row = pl.BlockSpec((1, t, d), lambda bi: (bi, 0, 0))
    return pl.pallas_call(
        _fourier_ctx_kernel,
        grid=(b,),
        in_specs=[row, _const_spec((1, d)), _batch_row_spec(d), _batch_row_spec(d), _const_spec(fl.shape),
                  _const_spec(ccg.shape), _const_spec(scg.shape), _const_spec(wo.shape),
                  _const_spec((1, d)), _batch_row_spec(d)],
        out_specs=row,
        out_shape=jax.ShapeDtypeStruct((b, t, d), F32),
        compiler_params=_cparams(("parallel",)),
        name="fourier_ctx",
    )(tok, g, sh, sc, fl, ccg, scg, wo, gpost, ga)


def _router_kernel(x_ref, g_ref, sh_ref, sc_ref, wrh_ref, wrl_ref, h_ref, gate_ref):
    h = _rms(x_ref[0], g_ref[...]) * (1.0 + sc_ref[0]) + sh_ref[0]
    h_ref[0] = h.astype(BF16)
    logits = _dot3(h, wrh_ref[...], wrl_ref[...])
    lane = lax.broadcasted_iota(jnp.int32, logits.shape, 1)
    neg = -jnp.inf
    l1 = jnp.where(lane < N_EXPERTS, logits, neg)
    m1 = l1.max(axis=-1, keepdims=True)
    i1 = jnp.where(l1 == m1, lane, LANES).min(axis=-1, keepdims=True)
    l2 = jnp.where(lane == i1, neg, l1)
    m2 = l2.max(axis=-1, keepdims=True)
    i2 = jnp.where(l2 == m2, lane, LANES).min(axis=-1, keepdims=True)
    e = jnp.exp(m2 - m1)
    g1 = 1.0 / (1.0 + e)
    g2 = e / (1.0 + e)
    gate_ref[0] = jnp.where(lane == i1, g1, jnp.where(lane == i2, g2, 0.0))


def _router(tok, g, sh, sc, wr_hi, wr_lo, tm):
    b, t, d = tok.shape
    row = lambda n: pl.BlockSpec((1, tm, n), lambda bi, ti: (bi, ti, 0))
    return pl.pallas_call(
        _router_kernel,
        grid=(b, t // tm),
        in_specs=[row(d), _const_spec((1, d)), _batch_row_spec(d), _batch_row_spec(d),
                  _const_spec(wr_hi.shape), _const_spec(wr_lo.shape)],
        out_specs=[row(d), row(LANES)],
        out_shape=[jax.ShapeDtypeStruct((b, t, d), BF16), jax.ShapeDtypeStruct((b, t, LANES), F32)],
        compiler_params=_cparams(("parallel", "parallel")),
        name="moe_router",
    )(tok, g, sh, sc, wr_hi, wr_lo)


def _moe_dense_kernel(x_ref, h_ref, gate_ref, w1_ref, w3_ref, w2_ref, g_ref, ga_ref, xo_ref, acc_ref):
    e = pl.program_id(2)
    f = pl.program_id(3)
    h = h_ref[0]
    gates = gate_ref[0]
    lane = lax.broadcasted_iota(jnp.int32, gates.shape, 1)
    ge = jnp.where(lane == e, gates, 0.0).sum(axis=-1, keepdims=True)
    a = _dot(h, w1_ref[0])
    mid = (_silu(a) * _dot(h, w3_ref[0])).astype(BF16)
    contrib = jnp.where(ge > 0.0, ge * _dot(mid, w2_ref[0]), 0.0)
    first = jnp.logical_and(e == 0, f == 0)

    @pl.when(first)
    def _():
        acc_ref[...] = contrib

    @pl.when(jnp.logical_not(first))
    def _():
        acc_ref[...] += contrib

    @pl.when(jnp.logical_and(e == pl.num_programs(2) - 1, f == pl.num_programs(3) - 1))
    def _():
        xo_ref[0] = x_ref[0] + ga_ref[0] * _rms(acc_ref[...], g_ref[...])


def _moe_dense(tok, h, gates, w1, w3, w2, g, ga, tm, fc):
    b, t, d = tok.shape
    ne, _, dff = w1.shape
    row = lambda n: pl.BlockSpec((1, tm, n), lambda bi, ti, ei, fi: (bi, ti, 0))
    return pl.pallas_call(
        _moe_dense_kernel,
        grid=(b, t // tm, ne, dff // fc),
        in_specs=[row(d), row(d), row(LANES),
                  pl.BlockSpec((1, d, fc), lambda bi, ti, ei, fi: (ei, 0, fi)),
                  pl.BlockSpec((1, d, fc), lambda bi, ti, ei, fi: (ei, 0, fi)),
                  pl.BlockSpec((1, fc, d), lambda bi, ti, ei, fi: (ei, fi, 0)),
                  _const_spec((1, d)),
                  pl.BlockSpec((1, 1, d), lambda bi, ti, ei, fi: (bi, 0, 0))],
        out_specs=row(d),
        out_shape=jax.ShapeDtypeStruct((b, t, d), F32),
        scratch_shapes=[pltpu.VMEM((tm, d), F32)],
        compiler_params=_cparams(("parallel", "parallel", "arbitrary", "arbitrary")),
        name="moe_experts",
    )(tok, h, gates, w1, w3, w2, g, ga)


PLANE_W = 256
N_PLANES = D_MODEL // (2 * PLANE_W)
SC_WINDOW = 128


def _pack_planes(h):
    out = []
    for p in range(N_PLANES):
        base = 2 * p * PLANE_W
        hi = pltpu.bitcast(h[:, base:base + PLANE_W].astype(BF16).astype(F32), jnp.uint32)
        lo = pltpu.bitcast(h[:, base + PLANE_W:base + 2 * PLANE_W].astype(BF16).astype(F32), jnp.uint32)
        out.append(hi | (lo >> 16))
    return out


def _unpack_planes(planes):
    cols = []
    for w in planes:
        cols.append(pltpu.bitcast(w & jnp.uint32(0xFFFF0000), F32))
        cols.append(pltpu.bitcast(w << 16, F32))
    return jnp.concatenate(cols, axis=1)


def _route_kernel(x_ref, g_ref, sh_ref, sc_ref, wrh_ref, wrl_ref, hp_ref, im_ref, gm_ref, cnt_ref, carry_ref):
    first = jnp.logical_and(pl.program_id(0) == 0, pl.program_id(1) == 0)

    @pl.when(first)
    def _():
        carry_ref[...] = jnp.zeros_like(carry_ref)

    h = _rms(x_ref[0], g_ref[...]) * (1.0 + sc_ref[0]) + sh_ref[0]
    for p, w in enumerate(_pack_planes(h)):
        hp_ref[p, 0] = w
    logits = _dot3(h, wrh_ref[...], wrl_ref[...])
    tm = logits.shape[0]
    lane = lax.broadcasted_iota(jnp.int32, logits.shape, 1)
    neg = -jnp.inf
    l1 = jnp.where(lane < N_EXPERTS, logits, neg)
    m1 = l1.max(axis=-1, keepdims=True)
    i1 = jnp.where(l1 == m1, lane, LANES).min(axis=-1, keepdims=True)
    l2 = jnp.where(lane == i1, neg, l1)
    m2 = l2.max(axis=-1, keepdims=True)
    i2 = jnp.where(l2 == m2, lane, LANES).min(axis=-1, keepdims=True)
    e = jnp.exp(m2 - m1)
    g1 = 1.0 / (1.0 + e)
    g2 = e / (1.0 + e)
    sel = jnp.logical_or(lane == i1, lane == i2)
    cnt = jnp.where(sel, 1.0, 0.0)
    r = lax.broadcasted_iota(jnp.int32, (tm, tm), 0)
    c = lax.broadcasted_iota(jnp.int32, (tm, tm), 1)
    below = jnp.where(r > c, 1.0, 0.0).astype(BF16)
    before = _dot(below, cnt.astype(BF16)) + carry_ref[...]
    r1 = jnp.where(lane == i1, before, 0.0).sum(axis=-1, keepdims=True).astype(jnp.int32)
    r2 = jnp.where(lane == i2, before, 0.0).sum(axis=-1, keepdims=True).astype(jnp.int32)
    im_ref[0] = jnp.where(lane == 0, i1, jnp.where(lane == 1, i2, jnp.where(lane == 2, r1, r2)))
    gm_ref[0] = jnp.where(lane == 0, g1, g2)
    total = carry_ref[...] + cnt.sum(axis=0, keepdims=True)
    carry_ref[...] = total
    cnt_ref[...] = total


def _route(tok, g, sh, sc, wr_hi, wr_lo, tm):
    b, t, d = tok.shape
    row = lambda n: pl.BlockSpec((1, tm, n), lambda bi, ti: (bi, ti, 0))
    return pl.pallas_call(
        _route_kernel,
        grid=(b, t // tm),
        in_specs=[row(d), _const_spec((1, d)), _batch_row_spec(d), _batch_row_spec(d),
                  _const_spec(wr_hi.shape), _const_spec(wr_lo.shape)],
        out_specs=[pl.BlockSpec((N_PLANES, 1, tm, PLANE_W), lambda bi, ti: (0, bi, ti, 0)),
                   row(LANES), row(LANES), _const_spec((1, LANES))],
        out_shape=[jax.ShapeDtypeStruct((N_PLANES, b, t, PLANE_W), jnp.uint32),
                   jax.ShapeDtypeStruct((b, t, LANES), jnp.int32),
                   jax.ShapeDtypeStruct((b, t, LANES), F32),
                   jax.ShapeDtypeStruct((1, LANES), F32)],
        scratch_shapes=[pltpu.VMEM((1, LANES), F32)],
        compiler_params=_cparams(("arbitrary", "arbitrary")),
        name="moe_route",
    )(tok, g, sh, sc, wr_hi, wr_lo)


def _sc_mesh():
    return plsc.VectorSubcoreMesh(core_axis_name="core", subcore_axis_name="subcore")


def _sc_gather_rows(table, idx):
    n = idx.shape[0]
    w = table.shape[1]

    @pl.kernel(out_type=jax.ShapeDtypeStruct((n, w), table.dtype), mesh=_sc_mesh())
    def gather(t_hbm, i_hbm, o_hbm):
        def body(i_vmem, o_vmem):
            pltpu.sync_copy(t_hbm.at[i_vmem.at[0]], o_vmem)

        pltpu.emit_pipeline(
            body, grid=(n // SC_WINDOW,),
            in_specs=[pl.BlockSpec((1, SC_WINDOW), index_map=lambda i: (0, i))],
            out_specs=[pl.BlockSpec((SC_WINDOW, w), index_map=lambda i: (i, 0))],
            core_axis_name=("core", "subcore"), dimension_semantics=(pltpu.PARALLEL,),
        )(i_hbm, o_hbm)

    return gather(table, idx.reshape(1, n))


def _sc_scatter_rows(src, idx, n_out):
    n = idx.shape[0]
    w = src.shape[1]
    n_src_windows = src.shape[0] // SC_WINDOW

    @pl.kernel(out_type=jax.ShapeDtypeStruct((n_out, w), src.dtype), mesh=_sc_mesh(), scratch_types=[])
    def scatter(s_hbm, i_hbm, o_hbm):
        def body(s_vmem, i_vmem):
            pltpu.sync_copy(s_vmem, o_hbm.at[i_vmem.at[0]])

        pltpu.emit_pipeline(
            body, grid=(n // SC_WINDOW,),
            in_specs=[pl.BlockSpec((SC_WINDOW, w), index_map=lambda i: (i % n_src_windows, 0)),
                      pl.BlockSpec((1, SC_WINDOW), index_map=lambda i: (0, i))],
            out_specs=[],
            core_axis_name=("core", "subcore"), dimension_semantics=(pltpu.PARALLEL,),
        )(s_hbm, i_hbm)

    return scatter(src, idx.reshape(1, n))


def _grouped_ffn_kernel(te_ref, nv_ref, xs_ref, w1_ref, w3_ref, w2_ref, y_ref, hb_ref, acc_ref):
    i = pl.program_id(0)
    f = pl.program_id(1)

    @pl.when(i < nv_ref[0])
    def _():
        @pl.when(f == 0)
        def _():
            hb_ref[...] = _unpack_planes([xs_ref[p] for p in range(N_PLANES)]).astype(BF16)

        half = hb_ref.shape[0] // 2
        for r in range(2):
            rows = slice(r * half, (r + 1) * half)
            h = hb_ref[rows, :]
            a = _dot(h, w1_ref[0])
            mid = (_silu(a) * _dot(h, w3_ref[0])).astype(BF16)
            contrib = _dot(mid, w2_ref[0])

            @pl.when(f == 0)
            def _():
                acc_ref[rows, :] = contrib

            @pl.when(f > 0)
            def _():
                acc_ref[rows, :] += contrib

        @pl.when(f == pl.num_programs(1) - 1)
        def _():
            for p, w in enumerate(_pack_planes(acc_ref[...])):
                y_ref[p] = w


def _grouped_ffn(xs, tile_expert, n_valid, w1, w3, w2, tm, fc):
    n_pad = xs.shape[1]
    ne, d, dff = w1.shape
    plane = pl.BlockSpec((N_PLANES, tm, PLANE_W), lambda i, f, te, nv: (0, i, 0))
    return pl.pallas_call(
        _grouped_ffn_kernel,
        grid_spec=pltpu.PrefetchScalarGridSpec(
            num_scalar_prefetch=2,
            grid=(n_pad // tm, dff // fc),
            in_specs=[plane,
                      pl.BlockSpec((1, d, fc), lambda i, f, te, nv: (te[i], 0, f)),
                      pl.BlockSpec((1, d, fc), lambda i, f, te, nv: (te[i], 0, f)),
                      pl.BlockSpec((1, fc, d), lambda i, f, te, nv: (te[i], f, 0))],
            out_specs=plane,
            scratch_shapes=[pltpu.VMEM((tm, d), BF16), pltpu.VMEM((tm, d), F32)]),
        out_shape=jax.ShapeDtypeStruct(xs.shape, jnp.uint32),
        compiler_params=_cparams(("arbitrary", "arbitrary")),
        name="moe_grouped_ffn",
    )(tile_expert, n_valid, xs, w1, w3, w2)


def _combine_kernel(x_ref, y_ref, gm_ref, g_ref, ga_ref, xo_ref):
    gm = gm_ref[0]
    y1 = _unpack_planes([y_ref[p, 0, 0] for p in range(N_PLANES)])
    y2 = _unpack_planes([y_ref[p, 1, 0] for p in range(N_PLANES)])
    mix = gm[:, 0:1] * y1 + gm[:, 1:2] * y2
    xo_ref[0] = x_ref[0] + ga_ref[0] * _rms(mix, g_ref[...])


def _combine(tok, yg, gm, g, ga, tm):
    b, t, d = tok.shape
    row = lambda n: pl.BlockSpec((1, tm, n), lambda bi, ti: (bi, ti, 0))
    return pl.pallas_call(
        _combine_kernel,
        grid=(b, t // tm),
        in_specs=[row(d),
                  pl.BlockSpec((N_PLANES, 2, 1, tm, PLANE_W), lambda bi, ti: (0, 0, bi, ti, 0)),
                  row(LANES), _const_spec((1, d)), _batch_row_spec(d)],
        out_specs=row(d),
        out_shape=jax.ShapeDtypeStruct((b, t, d), F32),
        compiler_params=_cparams(("parallel", "parallel")),
        name="moe_combine",
    )(tok, yg, gm, g, ga)


def _moe_sorted(tok, g_pre, sh, sc, wr_hi, wr_lo, w1, w3, w2, g_post, ga, tm, tm_e, fc):
    b, t, d = tok.shape
    n_tok = b * t
    hp, im, gm, cnt = _route(tok, g_pre, sh, sc, wr_hi, wr_lo, tm)
    counts = cnt[0, :N_EXPERTS].astype(jnp.int32)
    padded = ((counts + tm_e - 1) // tm_e) * tm_e
    ends = jnp.cumsum(padded)
    starts = ends - padded
    n_pad = 2 * n_tok + N_EXPERTS * tm_e
    n_tiles = n_pad // tm_e
    tile_expert = jnp.minimum(
        jnp.sum((jnp.arange(n_tiles, dtype=jnp.int32)[:, None] * tm_e >= ends[None, :]).astype(jnp.int32), axis=1),
        N_EXPERTS - 1).astype(jnp.int32)
    n_valid = (ends[-1:] // tm_e).astype(jnp.int32)
    im2 = im.reshape(n_tok, LANES)
    pos = jnp.stack([starts[im2[:, 0]] + im2[:, 2], starts[im2[:, 1]] + im2[:, 3]], axis=0)
    plane_off = (jnp.arange(N_PLANES, dtype=jnp.int32) * n_pad)
    idx_dispatch = (pos[:, None, :] + plane_off[None, :, None]).reshape(-1)
    idx_return = (pos[None, :, :] + plane_off[:, None, None]).reshape(-1)
    xs = _sc_scatter_rows(hp.reshape(N_PLANES * n_tok, PLANE_W), idx_dispatch, N_PLANES * n_pad)
    ys = _grouped_ffn(xs.reshape(N_PLANES, n_pad, PLANE_W), tile_expert, n_valid, w1, w3, w2, tm_e, fc)
    yg = _sc_gather_rows(ys.reshape(N_PLANES * n_pad, PLANE_W), idx_return)
    return _combine(tok, yg.reshape(N_PLANES, 2, b, t, PLANE_W), gm, g_post, ga, tm)


_ROPE_SWAP = np.concatenate([np.arange(16, 32), np.arange(0, 16), np.arange(48, 64), np.arange(32, 48)])


def _rope_tables(n_tok):
    rows = n_tok // GRID_W
    row = jnp.broadcast_to(jnp.arange(rows, dtype=F32)[:, None], (rows, GRID_W)).reshape(-1)
    col = jnp.broadcast_to(jnp.arange(GRID_W, dtype=F32)[None, :], (rows, GRID_W)).reshape(-1)
    half = MLA_ROPE // 2
    inv = 1.0 / (ROPE_BASE ** (jnp.arange(0, half, 2, dtype=F32) / half))
    cr, sr = jnp.cos(row[:, None] * inv), jnp.sin(row[:, None] * inv)
    cc, sc = jnp.cos(col[:, None] * inv), jnp.sin(col[:, None] * inv)
    cos64 = jnp.concatenate([cr, cr, cc, cc], axis=-1)
    sin64 = jnp.concatenate([-sr, sr, -sc, sc], axis=-1)
    return (jnp.tile(cos64, (1, MLA_HEADS)), jnp.tile(sin64, (1, MLA_HEADS)), cos64.T, sin64.T)


def _identity_rope_tables(n_tok):
    one = jnp.ones((n_tok, MLA_ROPE), F32)
    zero = jnp.zeros((n_tok, MLA_ROPE), F32)
    return (jnp.tile(one, (1, MLA_HEADS)), jnp.tile(zero, (1, MLA_HEADS)), one.T, zero.T)


def _even_weights(w_in, w_gate_f, b_gate_f, w_gate_b, b_gate_b, q_norm, w_uq, kv_norm, w_ukv):
    nqk = GLA_HEADS * GLA_DK
    nv = GLA_HEADS * GLA_DV
    o_z = 2 * nqk + nv
    o_r = o_z + 2 * GLA_RANK
    o_cq = o_r + nv
    o_kv = o_cq + MLA_Q_RANK
    o_kr = o_kv + MLA_KV_RANK
    d = w_in.shape[0]
    wp = jnp.concatenate([w_in[:, :o_z], w_in[:, o_r:o_kr], w_in[:, o_z:o_r],
                          jnp.zeros((d, LANES - 2 * GLA_RANK), F32)], axis=1).astype(BF16)
    kr = w_in[:, o_kr:o_kr + MLA_ROPE]
    wkr = jnp.concatenate([kr, kr[:, _ROPE_SWAP]], axis=1).T.astype(BF16)
    wg = jnp.zeros((LANES, 2 * nqk), F32)
    wg = wg.at[0:GLA_RANK, 0:nqk].set(w_gate_f).at[GLA_RANK:2 * GLA_RANK, nqk:].set(w_gate_b)
    wg_hi = wg.astype(BF16)
    wg_lo = (wg - wg_hi.astype(F32)).astype(BF16)
    bg = jnp.concatenate([b_gate_f, b_gate_b])[None, :]
    hq = np.arange(MLA_HEADS)[:, None] * MLA_QK
    nope_idx = (hq + np.arange(MLA_NOPE)[None, :]).reshape(-1)
    rope_idx = (hq + MLA_NOPE + np.arange(MLA_ROPE)[None, :]).reshape(-1)
    swap_idx = (hq + MLA_NOPE + _ROPE_SWAP[None, :]).reshape(-1)
    wuq = w_uq[:, np.concatenate([nope_idx, rope_idx, swap_idx])].astype(BF16)
    hk = np.arange(MLA_HEADS)[:, None] * (MLA_NOPE + MLA_V)
    k_idx = (hk + np.arange(MLA_NOPE)[None, :]).reshape(-1)
    v_idx = (hk + MLA_NOPE + np.arange(MLA_V)[None, :]).reshape(-1)
    return dict(wp=wp, wkr=wkr, wg_hi=wg_hi, wg_lo=wg_lo, bg=bg, qn=q_norm[None, :], wuq=wuq,
                kvn=kv_norm[None, :], wuk=w_ukv[:, k_idx].T.astype(BF16), wuv=w_ukv[:, v_idx].astype(BF16))


def _mods(m, rows, batch):
    d = D_MODEL
    if rows is None:
        return [jnp.broadcast_to(m[batch, k * d:(k + 1) * d][None, None, :], (batch, 1, d)) for k in range(6)]
    return [m[:batch, k * d:(k + 1) * d][:, None, :] for k in range(6)]


def kernel(x, c, ctx, c_ctx, w_mod, b_mod, g_mix_pre, g_mix_post, g_ffn_pre, g_ffn_post, e_w_in, e_w_gate_f, e_b_gate_f, e_w_gate_b, e_b_gate_b, e_gla_norm, e_q_norm, e_w_uq, e_kv_norm, e_w_ukv, e_w_o, e_w1, e_w3, e_w2, o_w_o, o_w_router, o_w1, o_w3, o_w2):
    batch, seq, d = x.shape
    n_ctx = ctx.shape[1]
    cond = jnp.zeros((16, d), F32).at[:batch].set(c).at[batch].set(c_ctx)
    mods = _adaln(cond, w_mod, b_mod)
    rope_x = _rope_tables(seq)
    rope_c = _identity_rope_tables(n_ctx)
    nqk = GLA_HEADS * GLA_DK
    last_read = 2 * ((DEPTH - 1) // 2)
    xs, xc = x, ctx
    for i in range(DEPTH):
        j = i // 2
        ctx_live = i <= last_read
        ctx_full = i < last_read
        mx = _mods(mods[i], 0, batch)
        mc = _mods(mods[i], None, batch)
        gpre, gpost = g_mix_pre[i][None, :], g_mix_post[i][None, :]
        fpre, fpost = g_ffn_pre[i][None, :], g_ffn_post[i][None, :]
        if i % 2 == 0:
            w = _even_weights(e_w_in[j], e_w_gate_f[j], e_b_gate_f[j], e_w_gate_b[j], e_b_gate_b[j],
                              e_q_norm[j], e_w_uq[j], e_kv_norm[j], e_w_ukv[j])
            wo1 = e_w_o[j][:GLA_HEADS * GLA_DV].astype(BF16)
            wo2 = e_w_o[j][GLA_HEADS * GLA_DV:].astype(BF16)
            gn = e_gla_norm[j][None, :]
            w1, w3, w2 = e_w1[j].astype(BF16), e_w3[j].astype(BF16), e_w2[j].astype(BF16)
            zero_state = jnp.zeros((batch, GLA_DV, nqk), F32)
            if ctx_live:
                cgq, cgk, cgv, cgr, claf, clab, cq, ckt, cv = _in_proj(xc, gpre, mc[0], mc[1], w, rope_c, n_ctx)
                co_f, co_b, s_f, s_b = _gla(cgq, cgk, cgv, claf, clab, zero_state, zero_state, n_ctx)
                srcs_c = [(ckt, cv)]
            else:
                s_f = s_b = zero_state
                srcs_c = []
            gq, gk, gv, gr, laf, lab, q, kt, v = _in_proj(xs, gpre, mx[0], mx[1], w, rope_x, min(512, seq))
            o_f, o_b, _, _ = _gla(gq, gk, gv, laf, lab, s_f, s_b, min(512, seq))
            a = _attention(q, [(kt, v)] + srcs_c, min(256, seq))
            xs, hx = _even_out(xs, o_f, o_b, gr, a, gn, wo1, wo2, gpost, mx[2], fpre, mx[3], mx[4], min(512, seq))
            xs = _ffn(xs, hx, w1, w3, w2, fpost, mx[5], min(512, seq))
            if ctx_full:
                ac = _attention(cq, srcs_c, n_ctx)
                xc, hc = _even_out(xc, co_f, co_b, cgr, ac, gn, wo1, wo2, gpost, mc[2], fpre, mc[3], mc[4], n_ctx)
                xc = _ffn(xc, hc, w1, w3, w2, fpost, mc[5], n_ctx)
        else:
            wo = o_w_o[j].astype(BF16)
            wr = jnp.zeros((d, LANES), F32).at[:, :N_EXPERTS].set(o_w_router[j])
            wr_hi = wr.astype(BF16)
            wr_lo = (wr - wr_hi.astype(F32)).astype(BF16)
            w1, w3, w2 = o_w1[j].astype(BF16), o_w3[j].astype(BF16), o_w2[j].astype(BF16)
            xs = _fourier_x(xs, gpre, mx[0], mx[1], wo, gpost, mx[2])
            xs = _moe_sorted(xs, fpre, mx[3], mx[4], wr_hi, wr_lo, w1, w3, w2, fpost, mx[5],
                             min(512, seq), min(1024, seq), 1792)
            if ctx_full:
                xc = _fourier_ctx(xc, gpre, mc[0], mc[1], wo, gpost, mc[2])
                hc, gates_c = _router(xc, fpre, mc[3], mc[4], wr_hi, wr_lo, n_ctx)
                xc = _moe_dense(xc, hc, gates_c, w1, w3, w2, fpost, mc[5], n_ctx, 1792)
    return xs
```

```python
import functools

import numpy as np
import jax
import jax.numpy as jnp
from jax import lax
from jax.experimental import pallas as pl
from jax.experimental.pallas import tpu as pltpu
from jax.experimental.pallas import tpu_sc as plsc

F32 = jnp.float32
BF16 = jnp.bfloat16

EPS = 1e-6
D_MODEL = 1024
DEPTH = 4
GRID_W = 64
GLA_HEADS = 4
GLA_DK = 64
GLA_DV = 128
GLA_RANK = 16
GLA_TAU = 16.0
GLA_CHUNK = 64
MLA_HEADS = 4
MLA_Q_RANK = 256
MLA_KV_RANK = 128
MLA_NOPE = 128
MLA_ROPE = 64
MLA_V = 128
MLA_QK = MLA_NOPE + MLA_ROPE
ROPE_BASE = 10000.0
FOURIER_GROUPS = 4
FOURIER_GW = D_MODEL // FOURIER_GROUPS
N_EXPERTS = 8
LANES = 128
VMEM_LIMIT = 48 * 1024 * 1024


def _cparams(sem):
    return pltpu.CompilerParams(dimension_semantics=sem, vmem_limit_bytes=VMEM_LIMIT)


def _dot(a, b):
    return jnp.dot(a, b, preferred_element_type=F32)


def _dot_nt(a, b):
    return lax.dot_general(a, b, (((1,), (1,)), ((), ())), preferred_element_type=F32)


def _dot_tn(a, b):
    return lax.dot_general(a, b, (((0,), (0,)), ((), ())), preferred_element_type=F32)


def _split(x):
    hi = x.astype(BF16)
    lo = (x - hi.astype(F32)).astype(BF16)
    return hi, lo


def _dot3(a, b_hi, b_lo):
    a_hi, a_lo = _split(a)
    return _dot(a_hi, b_hi) + _dot(a_lo, b_hi) + _dot(a_hi, b_lo)


def _rms(x, g):
    return x * lax.rsqrt(jnp.mean(x * x, axis=-1, keepdims=True) + EPS) * g


def _silu(x):
    return x / (1.0 + jnp.exp(-x))


def _const_spec(shape):
    nd = len(shape)
    return pl.BlockSpec(shape, lambda *_: (0,) * nd)


def _batch_row_spec(d):
    return pl.BlockSpec((1, 1, d), lambda b, *_: (b, 0, 0))


def _adaln_kernel(c_ref, w_ref, b_ref, o_ref):
    a = _silu(c_ref[...])
    w_hi, w_lo = _split(w_ref[0])
    o_ref[0] = _dot3(a, w_hi, w_lo) + b_ref[0]


def _adaln(cond, w_mod, b_mod):
    depth, d, n = w_mod.shape
    rows = cond.shape[0]
    bn = 1536
    return pl.pallas_call(
        _adaln_kernel,
        grid=(depth, n // bn),
        in_specs=[
            pl.BlockSpec((rows, d), lambda i, j: (0, 0)),
            pl.BlockSpec((1, d, bn), lambda i, j: (i, 0, j)),
            pl.BlockSpec((1, 1, bn), lambda i, j: (i, 0, j)),
        ],
        out_specs=pl.BlockSpec((1, rows, bn), lambda i, j: (i, 0, j)),
        out_shape=jax.ShapeDtypeStruct((depth, rows, n), F32),
        compiler_params=_cparams(("arbitrary", "arbitrary")),
        name="adaln",
    )(cond, w_mod, b_mod.reshape(depth, 1, n))


def _in_proj_kernel(x_ref, g_ref, sh_ref, sc_ref, wp_ref, wkr_ref, wgh_ref, wgl_ref, bg_ref,
                    qn_ref, wuq_ref, kvn_ref, wuk_ref, wuv_ref, cq_ref, sq_ref, ck_ref, sk_ref,
                    gq_ref, gk_ref, gv_ref, gr_ref, laf_ref, lab_ref, q_ref, kt_ref, v_ref):
    h = _rms(x_ref[0], g_ref[...]) * (1.0 + sc_ref[0]) + sh_ref[0]
    hb = h.astype(BF16)
    p = _dot(hb, wp_ref[...])
    nqk = GLA_HEADS * GLA_DK
    nv = GLA_HEADS * GLA_DV
    gq_ref[0] = p[:, 0:nqk] * (GLA_DK ** -0.5)
    gk_ref[0] = p[:, nqk:2 * nqk]
    gv_ref[0] = p[:, 2 * nqk:2 * nqk + nv]
    gr_ref[0] = p[:, 2 * nqk + nv:2 * nqk + 2 * nv]
    o = 2 * nqk + 2 * nv
    cq = p[:, o:o + MLA_Q_RANK]
    ckv = p[:, o + MLA_Q_RANK:o + MLA_Q_RANK + MLA_KV_RANK]
    tail = p[:, o + MLA_Q_RANK + MLA_KV_RANK:]
    pre = _dot3(tail, wgh_ref[...], wgl_ref[...]) + bg_ref[...]
    la = (jnp.minimum(pre, 0.0) - jnp.log(1.0 + jnp.exp(-jnp.abs(pre)))) * (1.0 / GLA_TAU)
    laf_ref[0] = la[:, :nqk]
    lab_ref[0] = la[:, nqk:]
    q = _dot(_rms(cq, qn_ref[...]).astype(BF16), wuq_ref[...])
    att_scale = MLA_QK ** -0.5
    nn = MLA_HEADS * MLA_NOPE
    nr = MLA_HEADS * MLA_ROPE
    q_rope = q[:, nn:nn + nr] * cq_ref[...] + q[:, nn + nr:] * sq_ref[...]
    for hd in range(MLA_HEADS):
        q_ref[0, hd, :, 0:MLA_NOPE] = (q[:, hd * MLA_NOPE:(hd + 1) * MLA_NOPE] * att_scale).astype(BF16)
        q_ref[0, hd, :, MLA_NOPE:MLA_QK] = (q_rope[:, hd * MLA_ROPE:(hd + 1) * MLA_ROPE] * att_scale).astype(BF16)
    ckvn = _rms(ckv, kvn_ref[...]).astype(BF16)
    kt = _dot_nt(wuk_ref[...], ckvn)
    v = _dot(ckvn, wuv_ref[...])
    kr2 = _dot_nt(wkr_ref[...], hb)
    kr = (kr2[:MLA_ROPE] * ck_ref[...] + kr2[MLA_ROPE:] * sk_ref[...]).astype(BF16)
    for hd in range(MLA_HEADS):
        kt_ref[0, hd, 0:MLA_NOPE, :] = kt[hd * MLA_NOPE:(hd + 1) * MLA_NOPE].astype(BF16)
        kt_ref[0, hd, MLA_NOPE:MLA_QK, :] = kr
        v_ref[0, hd] = v[:, hd * MLA_V:(hd + 1) * MLA_V].astype(BF16)


def _in_proj(tok, g, sh, sc, w, tabs, tm):
    b, t, d = tok.shape
    cq, sq, ck, sk = tabs
    nqk = GLA_HEADS * GLA_DK
    nv = GLA_HEADS * GLA_DV
    row = lambda n: pl.BlockSpec((1, tm, n), lambda bi, ti: (bi, ti, 0))
    weights = (w["wp"], w["wkr"], w["wg_hi"], w["wg_lo"], w["bg"], w["qn"], w["wuq"], w["kvn"],
               w["wuk"], w["wuv"])
    return pl.pallas_call(
        _in_proj_kernel,
        grid=(b, t // tm),
        in_specs=[row(d), _const_spec((1, d)), _batch_row_spec(d), _batch_row_spec(d)]
        + [_const_spec(a.shape) for a in weights]
        + [pl.BlockSpec((tm, MLA_HEADS * MLA_ROPE), lambda bi, ti: (ti, 0)),
           pl.BlockSpec((tm, MLA_HEADS * MLA_ROPE), lambda bi, ti: (ti, 0)),
           pl.BlockSpec((MLA_ROPE, tm), lambda bi, ti: (0, ti)),
           pl.BlockSpec((MLA_ROPE, tm), lambda bi, ti: (0, ti))],
        out_specs=[row(nqk), row(nqk), row(nv), row(nv), row(nqk), row(nqk),
                   pl.BlockSpec((1, MLA_HEADS, tm, MLA_QK), lambda bi, ti: (bi, 0, ti, 0)),
                   pl.BlockSpec((1, MLA_HEADS, MLA_QK, tm), lambda bi, ti: (bi, 0, 0, ti)),
                   pl.BlockSpec((1, MLA_HEADS, tm, MLA_V), lambda bi, ti: (bi, 0, ti, 0))],
        out_shape=[jax.ShapeDtypeStruct((b, t, nqk), F32), jax.ShapeDtypeStruct((b, t, nqk), F32),
                   jax.ShapeDtypeStruct((b, t, nv), F32), jax.ShapeDtypeStruct((b, t, nv), F32),
                   jax.ShapeDtypeStruct((b, t, nqk), F32), jax.ShapeDtypeStruct((b, t, nqk), F32),
                   jax.ShapeDtypeStruct((b, MLA_HEADS, t, MLA_QK), BF16),
                   jax.ShapeDtypeStruct((b, MLA_HEADS, MLA_QK, t), BF16),
                   jax.ShapeDtypeStruct((b, MLA_HEADS, t, MLA_V), BF16)],
        compiler_params=_cparams(("parallel", "parallel")),
        name="even_in_proj",
    )(tok, g, sh, sc, *weights, cq, sq, ck, sk)


def _gla_chunks(streams):
    c = GLA_CHUNK
    heads = range(GLA_HEADS)
    ks = [slice(hd * GLA_DK, (hd + 1) * GLA_DK) for hd in heads]
    vs = [slice(hd * GLA_DV, (hd + 1) * GLA_DV) for hd in heads]
    bcs = []
    for (_, _, _, l_ref, _, _, g, row0, tri, _, _) in streams:
        la_hi, la_lo = _split(l_ref[g, pl.ds(row0, c), :])
        bcs.append(_dot(tri, la_hi) + _dot(tri, la_lo))
    ops = []
    for (q_ref, k_ref, v_ref, _, _, st_ref, g, row0, _, _, last_row), bc in zip(streams, bcs):
        bl = bc[last_row:last_row + 1, :]
        q = q_ref[g, pl.ds(row0, c), :]
        k = k_ref[g, pl.ds(row0, c), :]
        st = st_ref[g]
        ops.append(dict(qc=(q * jnp.exp(bc)).astype(BF16), kc=(k * jnp.exp(-bc)).astype(BF16),
                        kd=(k * jnp.exp(bl - bc)).astype(BF16), vb=v_ref[g, pl.ds(row0, c), :].astype(BF16),
                        st=st, stb=st.astype(BF16), decay=jnp.exp(bl)))
    atts = [[_dot_nt(o["qc"][:, ks[hd]], o["kc"][:, ks[hd]]) for hd in heads] for o in ops]
    inters = [[_dot_nt(o["qc"][:, ks[hd]], o["stb"][:, ks[hd]]) for hd in heads] for o in ops]
    upds = [[_dot_tn(o["vb"][:, vs[hd]], o["kd"][:, ks[hd]]) for hd in heads] for o in ops]
    intras = [[_dot(jnp.where(s[9], att[hd], 0.0).astype(BF16), o["vb"][:, vs[hd]]) for hd in heads]
              for s, o, att in zip(streams, ops, atts)]
    for s, o, intra, inter, upd in zip(streams, ops, intras, inters, upds):
        o_ref, st_ref, g, row0 = s[4], s[5], s[6], s[7]
        o_ref[g, pl.ds(row0, c), :] = jnp.concatenate([a + b for a, b in zip(intra, inter)], axis=1)
        st_ref[g] = o["st"] * o["decay"] + jnp.concatenate(upd, axis=1)


def _gla_kernel(qf, kf, vf, lf, qb, kb, vb, lb, s0f, s0b, of, ob, sff, sfb, stf, stb, *, nc, gb):
    j = pl.program_id(1)

    @pl.when(j == 0)
    def _():
        stf[...] = s0f[...]
        stb[...] = s0b[...]

    c = GLA_CHUNK
    r = lax.broadcasted_iota(jnp.int32, (c, c), 0)
    cc = lax.broadcasted_iota(jnp.int32, (c, c), 1)
    lower = r >= cc
    upper = r <= cc
    tri_l = jnp.where(lower, 1.0, 0.0).astype(BF16)
    tri_u = jnp.where(upper, 1.0, 0.0).astype(BF16)

    for ci in range(nc):
        streams = []
        for g in range(gb):
            streams.append((qf, kf, vf, lf, of, stf, g, ci * c, tri_l, lower, c - 1))
            streams.append((qb, kb, vb, lb, ob, stb, g, (nc - 1 - ci) * c, tri_u, upper, 0))
        _gla_chunks(streams)

    @pl.when(j == pl.num_programs(1) - 1)
    def _():
        sff[...] = stf[...]
        sfb[...] = stb[...]


GLA_BATCH_ROWS = 2


def _gla(gq, gk, gv, laf, lab, s0f, s0b, tb):
    b, t, nqk = gq.shape
    nv = gv.shape[-1]
    nblk = t // tb
    gb = GLA_BATCH_ROWS
    fwd = lambda n: pl.BlockSpec((gb, tb, n), lambda bi, j: (bi, j, 0))
    bwd = lambda n: pl.BlockSpec((gb, tb, n), lambda bi, j: (bi, nblk - 1 - j, 0))
    st = pl.BlockSpec((gb, GLA_DV, nqk), lambda bi, j: (bi, 0, 0))
    return pl.pallas_call(
        functools.partial(_gla_kernel, nc=tb // GLA_CHUNK, gb=gb),
        grid=(b // gb, nblk),
        in_specs=[fwd(nqk), fwd(nqk), fwd(nv), fwd(nqk), bwd(nqk), bwd(nqk), bwd(nv), bwd(nqk), st, st],
        out_specs=[fwd(nv), bwd(nv), st, st],
        out_shape=[jax.ShapeDtypeStruct((b, t, nv), F32), jax.ShapeDtypeStruct((b, t, nv), F32),
                   jax.ShapeDtypeStruct((b, GLA_DV, nqk), F32), jax.ShapeDtypeStruct((b, GLA_DV, nqk), F32)],
        scratch_shapes=[pltpu.VMEM((gb, GLA_DV, nqk), F32), pltpu.VMEM((gb, GLA_DV, nqk), F32)],
        compiler_params=_cparams(("parallel", "arbitrary")),
        name="gla_scan",
    )(gq, gk, gv, laf, gq, gk, gv, lab, s0f, s0b)


ATTN_KEY_CHUNK = 1024


def _attn_kernel(*refs, n_src):
    q_ref = refs[0]
    kts = refs[1:1 + 2 * n_src:2]
    vs = refs[2:2 + 2 * n_src:2]
    o_ref = refs[1 + 2 * n_src]
    vexts = refs[2 + 2 * n_src:]

    @pl.when(pl.program_id(2) == 0)
    def _():
        for v, vext in zip(vs, vexts):
            tk = v.shape[2]
            lane = lax.broadcasted_iota(jnp.int32, (tk, LANES), 1)
            vext[:, 0:MLA_V] = v[0, 0]
            vext[:, MLA_V:MLA_V + LANES] = jnp.where(lane == 0, 1.0, 0.0).astype(BF16)

    q = q_ref[0, 0]
    bq = q.shape[0]
    m = jnp.full((bq, 1), -jnp.inf, F32)
    acc = jnp.zeros((bq, MLA_V + LANES), F32)
    for kt, vext in zip(kts, vexts):
        tk = kt.shape[3]
        ck = min(ATTN_KEY_CHUNK, tk)
        for c in range(tk // ck):
            s = _dot(q, kt[0, 0, :, c * ck:(c + 1) * ck])
            m_new = jnp.maximum(m, s.max(axis=-1, keepdims=True))
            p = jnp.exp((s - m_new).astype(BF16))
            acc = acc * jnp.exp(m - m_new) + _dot(p, vext[c * ck:(c + 1) * ck, :])
            m = m_new
    o_ref[0] = (acc[:, 0:MLA_V] / acc[:, MLA_V:MLA_V + 1]).astype(o_ref.dtype)


def _attention(q, srcs, bq):
    b, nh, t, dqk = q.shape
    in_specs = [pl.BlockSpec((1, 1, bq, dqk), lambda bi, hi, qi: (bi, hi, qi, 0))]
    args = [q]
    scratch = []
    for kt, v in srcs:
        tk = kt.shape[-1]
        in_specs.append(pl.BlockSpec((1, 1, dqk, tk), lambda bi, hi, qi: (bi, hi, 0, 0)))
        in_specs.append(pl.BlockSpec((1, 1, tk, MLA_V), lambda bi, hi, qi: (bi, hi, 0, 0)))
        args += [kt, v]
        scratch.append(pltpu.VMEM((tk, MLA_V + LANES), BF16))
    return pl.pallas_call(
        functools.partial(_attn_kernel, n_src=len(srcs)),
        grid=(b, nh, t // bq),
        in_specs=in_specs,
        out_specs=pl.BlockSpec((1, bq, MLA_V), lambda bi, hi, qi: (bi, qi, hi)),
        out_shape=jax.ShapeDtypeStruct((b, t, nh * MLA_V), BF16),
        scratch_shapes=scratch,
        compiler_params=_cparams(("parallel", "parallel", "arbitrary")),
        name="mla_attention",
    )(*args)


def _even_out_kernel(x_ref, of_ref, ob_ref, gr_ref, a_ref, gn_ref, wo1_ref, wo2_ref, gpost_ref, ga_ref,
                     gpre_ref, sh_ref, sc_ref, xo_ref, h_ref):
    o = of_ref[0] + ob_ref[0]
    parts = [_rms(o[:, hd * GLA_DV:(hd + 1) * GLA_DV], gn_ref[...]) for hd in range(GLA_HEADS)]
    fin = jnp.concatenate(parts, axis=1) * _silu(gr_ref[0])
    y = _dot(fin.astype(BF16), wo1_ref[...]) + _dot(a_ref[0], wo2_ref[...])
    xn = x_ref[0] + ga_ref[0] * _rms(y, gpost_ref[...])
    xo_ref[0] = xn
    h_ref[0] = (_rms(xn, gpre_ref[...]) * (1.0 + sc_ref[0]) + sh_ref[0]).astype(BF16)


def _even_out(tok, o_f, o_b, gr, a, gn, wo1, wo2, gpost, ga, gpre, sh, sc, tm):
    b, t, d = tok.shape
    nv = o_f.shape[-1]
    row = lambda n: pl.BlockSpec((1, tm, n), lambda bi, ti: (bi, ti, 0))
    return pl.pallas_call(
        _even_out_kernel,
        grid=(b, t // tm),
        in_specs=[row(d), row(nv), row(nv), row(nv), row(a.shape[-1]), _const_spec(gn.shape),
                  _const_spec(wo1.shape), _const_spec(wo2.shape), _const_spec((1, d)), _batch_row_spec(d),
                  _const_spec((1, d)), _batch_row_spec(d), _batch_row_spec(d)],
        out_specs=[row(d), row(d)],
        out_shape=[jax.ShapeDtypeStruct((b, t, d), F32), jax.ShapeDtypeStruct((b, t, d), BF16)],
        compiler_params=_cparams(("parallel", "parallel")),
        name="even_out_proj",
    )(tok, o_f, o_b, gr, a, gn, wo1, wo2, gpost, ga, gpre, sh, sc)


MXU_COLS = 256


def _ff_chunks(dff, max_tiles):
    n_tiles = dff // MXU_COLS
    n_chunks = -(-n_tiles // max_tiles)
    base, extra = divmod(n_tiles, n_chunks)
    bounds, start = [], 0
    for i in range(n_chunks):
        width = (base + (1 if i < extra else 0)) * MXU_COLS
        bounds.append((start, start + width))
        start += width
    return bounds


def _ffn_kernel(x_ref, h_ref, w1_ref, w3_ref, w2_ref, g_ref, ga_ref, xo_ref, *, chunks):
    h = h_ref[0]
    y = None
    for lo, hi in chunks:
        a = _dot(h, w1_ref[:, lo:hi])
        mid = (_silu(a) * _dot(h, w3_ref[:, lo:hi])).astype(BF16)
        part = _dot(mid, w2_ref[lo:hi, :])
        y = part if y is None else y + part
    xo_ref[0] = x_ref[0] + ga_ref[0] * _rms(y, g_ref[...])


def _ffn(tok, h, w1, w3, w2, g, ga, tm):
    b, t, d = tok.shape
    dff = w1.shape[-1]
    row = lambda: pl.BlockSpec((1, tm, d), lambda bi, ti: (bi, ti, 0))
    resident = lambda shape: pl.BlockSpec(shape, lambda bi, ti: (0, 0), pipeline_mode=pl.Buffered(1))
    return pl.pallas_call(
        functools.partial(_ffn_kernel, chunks=_ff_chunks(dff, 6)),
        grid=(b, t // tm),
        in_specs=[row(), row(), resident((d, dff)), resident((d, dff)), resident((dff, d)),
                  _const_spec((1, d)), _batch_row_spec(d)],
        out_specs=row(),
        out_shape=jax.ShapeDtypeStruct((b, t, d), F32),
        compiler_params=_cparams(("parallel", "parallel")),
        name="swiglu_ffn",
    )(tok, h, w1, w3, w2, g, ga)


def _dft_cos_sin(n):
    idx = (np.arange(n)[:, None] * np.arange(n)[None, :]) % n
    ang = 2.0 * np.pi * idx.astype(np.float64) / n
    return np.cos(ang), np.sin(ang)


def _fourier1_kernel(x_ref, g_ref, sh_ref, sc_ref, m1_ref, tc_ref, ts_ref, o_ref, *, nb, n1):
    x = x_ref[0]
    x = x.reshape(n1 * nb, x.shape[-1])
    hb = (_rms(x, g_ref[...]) * (1.0 + sc_ref[0]) + sh_ref[0]).astype(BF16)
    dh = o_ref.shape[-1] // 2
    half = nb * n1
    reps = dh // LANES
    tc = jnp.concatenate([tc_ref[...]] * reps, axis=1)
    ts = jnp.concatenate([ts_ref[...]] * reps, axis=1)
    for c in range(hb.shape[-1] // dh):
        a = _dot(m1_ref[...], hb[:, c * dh:(c + 1) * dh])
        ar, ai = a[:half], a[half:]
        o_ref[0, 0, :, :, c * dh:(c + 1) * dh] = (ar * tc - ai * ts).reshape(nb, n1, dh)
        o_ref[0, 1, :, :, c * dh:(c + 1) * dh] = (ar * ts + ai * tc).reshape(nb, n1, dh)


def _fourier2_kernel(b_ref, x_ref, m2_ref, cc_ref, sc_ref, wo_ref, gpost_ref, ga_ref, xo_ref, *, kb, n2):
    d = b_ref.shape[-1]
    half = n2 * kb
    u = _dot(m2_ref[...], b_ref[0].reshape(2 * half, d).astype(BF16))
    ur = u[:half].astype(BF16)
    ui = u[half:].astype(BF16)
    gw = FOURIER_GW
    f = jnp.concatenate(
        [_dot(ur[:, g * gw:(g + 1) * gw], cc_ref[...]) + _dot(ui[:, g * gw:(g + 1) * gw], sc_ref[...])
         for g in range(FOURIER_GROUPS)], axis=1)
    y = _dot(f.astype(BF16), wo_ref[...])
    xn = x_ref[0].reshape(half, d) + ga_ref[0] * _rms(y, gpost_ref[...])
    xo_ref[0] = xn.reshape(n2, kb, d)


def _fourier_x(tok, g, sh, sc, wo, gpost, ga):
    b, t, d = tok.shape
    n1, n2 = 128, t // 128
    nb, kb = min(8, n2), 8
    c1, s1 = _dft_cos_sin(n1)
    c2, s2 = _dft_cos_sin(n2)
    cg, sg = _dft_cos_sin(FOURIER_GW)
    m1 = np.zeros((2, nb, n1, n1, nb))
    for j in range(nb):
        m1[0, j, :, :, j] = c1 / np.sqrt(n1)
        m1[1, j, :, :, j] = -s1 / np.sqrt(n1)
    m1 = jnp.asarray(m1.reshape(2 * nb * n1, n1 * nb), BF16)
    f2 = np.stack([np.stack([c2, s2], axis=1), np.stack([-s2, c2], axis=1)], axis=0) / np.sqrt(n2)
    m2 = np.zeros((2, n2, kb, 2, n2, kb))
    for l in range(kb):
        m2[:, :, l, :, :, l] = f2
    m2 = jnp.asarray(m2.reshape(2 * n2 * kb, 2 * n2 * kb), BF16)
    ccg = jnp.asarray(cg / np.sqrt(FOURIER_GW), BF16)
    scg = jnp.asarray(sg / np.sqrt(FOURIER_GW), BF16)
    tw = 2.0 * np.pi * ((np.arange(n2)[:, None] * np.arange(n1)[None, :]) % t).astype(np.float64) / t
    tc = jnp.asarray(np.broadcast_to(np.cos(tw).reshape(-1, 1), (n2 * n1, LANES)), F32)
    ts = jnp.asarray(np.broadcast_to(-np.sin(tw).reshape(-1, 1), (n2 * n1, LANES)), F32)
    stage1 = pl.pallas_call(
        functools.partial(_fourier1_kernel, nb=nb, n1=n1),
        grid=(b, n2 // nb),
        in_specs=[pl.BlockSpec((1, n1, nb, d), lambda bi, ji: (bi, 0, ji, 0)),
                  _const_spec((1, d)), _batch_row_spec(d), _batch_row_spec(d),
                  pl.BlockSpec(m1.shape, lambda bi, ji: (0, 0), pipeline_mode=pl.Buffered(1)),
                  pl.BlockSpec((nb * n1, LANES), lambda bi, ji: (ji, 0)),
                  pl.BlockSpec((nb * n1, LANES), lambda bi, ji: (ji, 0))],
        out_specs=pl.BlockSpec((1, 2, nb, n1, d), lambda bi, ji: (bi, 0, ji, 0, 0)),
        out_shape=jax.ShapeDtypeStruct((b, 2, n2, n1, d), F32),
        compiler_params=_cparams(("parallel", "parallel")),
        name="fourier_stage1",
    )(tok.reshape(b, n1, n2, d), g, sh, sc, m1, tc, ts)
    out = pl.pallas_call(
        functools.partial(_fourier2_kernel, kb=kb, n2=n2),
        grid=(b, n1 // kb),
        in_specs=[pl.BlockSpec((1, 2, n2, kb, d), lambda bi, ki: (bi, 0, 0, ki, 0)),
                  pl.BlockSpec((1, n2, kb, d), lambda bi, ki: (bi, 0, ki, 0)),
                  _const_spec(m2.shape), _const_spec(ccg.shape), _const_spec(scg.shape),
                  _const_spec(wo.shape), _const_spec((1, d)), _batch_row_spec(d)],
        out_specs=pl.BlockSpec((1, n2, kb, d), lambda bi, ki: (bi, 0, ki, 0)),
        out_shape=jax.ShapeDtypeStruct((b, n2, n1, d), F32),
        compiler_params=_cparams(("parallel", "parallel")),
        name="fourier_stage2",
    )(stage1, tok.reshape(b, n2, n1, d), m2, ccg, scg, wo, gpost, ga)
    return out.reshape(b, t, d)


def _fourier_ctx_kernel(x_ref, g_ref, sh_ref, sc_ref, fl_ref, cc_ref, scg_ref, wo_ref, gpost_ref, ga_ref,
                        xo_ref):
    x = x_ref[0]
    t = x.shape[0]
    h = _rms(x, g_ref[...]) * (1.0 + sc_ref[0]) + sh_ref[0]
    u = _dot(fl_ref[...], h.astype(BF16))
    ur = u[:t].astype(BF16)
    ui = u[t:].astype(BF16)
    gw = FOURIER_GW
    f = jnp.concatenate(
        [_dot(ur[:, g * gw:(g + 1) * gw], cc_ref[...]) + _dot(ui[:, g * gw:(g + 1) * gw], scg_ref[...])
         for g in range(FOURIER_GROUPS)], axis=1)
    y = _dot(f.astype(BF16), wo_ref[...])
    xo_ref[0] = x + ga_ref[0] * _rms(y, gpost_ref[...])


def _fourier_ctx(tok, g, sh, sc, wo, gpost, ga):
    b, t, d = tok.shape
    cl, sl = _dft_cos_sin(t)
    cg, sg = _dft_cos_sin(FOURIER_GW)
    fl = jnp.asarray(np.concatenate([cl, -sl], axis=0) / np.sqrt(t), BF16)
    ccg = jnp.asarray(cg / np.sqrt(FOURIER_GW), BF16)
    scg = jnp.asarray(sg / np.sqrt(FOURIER_GW), BF16)
    row = pl.BlockSpec((1, t, d), lambda bi: (bi, 0, 0))
    return pl.pallas_call(
        _fourier_ctx_kernel,
        grid=(b,),
        in_specs=[row, _const_spec((1, d)), _batch_row_spec(d), _batch_row_spec(d), _const_spec(fl.shape),
                  _const_spec(ccg.shape), _const_spec(scg.shape), _const_spec(wo.shape),
                  _const_spec((1, d)), _batch_row_spec(d)],
        out_specs=row,
        out_shape=jax.ShapeDtypeStruct((b, t, d), F32),
        compiler_params=_cparams(("parallel",)),
        name="fourier_ctx",
    )(tok, g, sh, sc, fl, ccg, scg, wo, gpost, ga)


def _router_kernel(x_ref, g_ref, sh_ref, sc_ref, wrh_ref, wrl_ref, h_ref, gate_ref):
    h = _rms(x_ref[0], g_ref[...]) * (1.0 + sc_ref[0]) + sh_ref[0]
    h_ref[0] = h.astype(BF16)
    logits = _dot3(h, wrh_ref[...], wrl_ref[...])
    lane = lax.broadcasted_iota(jnp.int32, logits.shape, 1)
    neg = -jnp.inf
    l1 = jnp.where(lane < N_EXPERTS, logits, neg)
    m1 = l1.max(axis=-1, keepdims=True)
    i1 = jnp.where(l1 == m1, lane, LANES).min(axis=-1, keepdims=True)
    l2 = jnp.where(lane == i1, neg, l1)
    m2 = l2.max(axis=-1, keepdims=True)
    i2 = jnp.where(l2 == m2, lane, LANES).min(axis=-1, keepdims=True)
    e = jnp.exp(m2 - m1)
    g1 = 1.0 / (1.0 + e)
    g2 = e / (1.0 + e)
    gate_ref[0] = jnp.where(lane == i1, g1, jnp.where(lane == i2, g2, 0.0))


def _router(tok, g, sh, sc, wr_hi, wr_lo, tm):
    b, t, d = tok.shape
    row = lambda n: pl.BlockSpec((1, tm, n), lambda bi, ti: (bi, ti, 0))
    return pl.pallas_call(
        _router_kernel,
        grid=(b, t // tm),
        in_specs=[row(d), _const_spec((1, d)), _batch_row_spec(d), _batch_row_spec(d),
                  _const_spec(wr_hi.shape), _const_spec(wr_lo.shape)],
        out_specs=[row(d), row(LANES)],
        out_shape=[jax.ShapeDtypeStruct((b, t, d), BF16), jax.ShapeDtypeStruct((b, t, LANES), F32)],
        compiler_params=_cparams(("parallel", "parallel")),
        name="moe_router",
    )(tok, g, sh, sc, wr_hi, wr_lo)


def _moe_dense_kernel(x_ref, h_ref, gate_ref, w1_ref, w3_ref, w2_ref, g_ref, ga_ref, xo_ref, acc_ref):
    e = pl.program_id(2)
    f = pl.program_id(3)
    h = h_ref[0]
    gates = gate_ref[0]
    lane = lax.broadcasted_iota(jnp.int32, gates.shape, 1)
    ge = jnp.where(lane == e, gates, 0.0).sum(axis=-1, keepdims=True)
    a = _dot(h, w1_ref[0])
    mid = (_silu(a) * _dot(h, w3_ref[0])).astype(BF16)
    contrib = jnp.where(ge > 0.0, ge * _dot(mid, w2_ref[0]), 0.0)
    first = jnp.logical_and(e == 0, f == 0)

    @pl.when(first)
    def _():
        acc_ref[...] = contrib

    @pl.when(jnp.logical_not(first))
    def _():
        acc_ref[...] += contrib

    @pl.when(jnp.logical_and(e == pl.num_programs(2) - 1, f == pl.num_programs(3) - 1))
    def _():
        xo_ref[0] = x_ref[0] + ga_ref[0] * _rms(acc_ref[...], g_ref[...])


def _moe_dense(tok, h, gates, w1, w3, w2, g, ga, tm, fc):
    b, t, d = tok.shape
    ne, _, dff = w1.shape
    row = lambda n: pl.BlockSpec((1, tm, n), lambda bi, ti, ei, fi: (bi, ti, 0))
    return pl.pallas_call(
        _moe_dense_kernel,
        grid=(b, t // tm, ne, dff // fc),
        in_specs=[row(d), row(d), row(LANES),
                  pl.BlockSpec((1, d, fc), lambda bi, ti, ei, fi: (ei, 0, fi)),
                  pl.BlockSpec((1, d, fc), lambda bi, ti, ei, fi: (ei, 0, fi)),
                  pl.BlockSpec((1, fc, d), lambda bi, ti, ei, fi: (ei, fi, 0)),
                  _const_spec((1, d)),
                  pl.BlockSpec((1, 1, d), lambda bi, ti, ei, fi: (bi, 0, 0))],
        out_specs=row(d),
        out_shape=jax.ShapeDtypeStruct((b, t, d), F32),
        scratch_shapes=[pltpu.VMEM((tm, d), F32)],
        compiler_params=_cparams(("parallel", "parallel", "arbitrary", "arbitrary")),
        name="moe_experts",
    )(tok, h, gates, w1, w3, w2, g, ga)


PLANE_W = 256
N_PLANES = D_MODEL // (2 * PLANE_W)
SC_WINDOW = 128


def _pack_planes(h):
    out = []
    for p in range(N_PLANES):
        base = 2 * p * PLANE_W
        hi = pltpu.bitcast(h[:, base:base + PLANE_W].astype(BF16).astype(F32), jnp.uint32)
        lo = pltpu.bitcast(h[:, base + PLANE_W:base + 2 * PLANE_W].astype(BF16).astype(F32), jnp.uint32)
        out.append(hi | (lo >> 16))
    return out


def _unpack_planes(planes):
    cols = []
    for w in planes:
        cols.append(pltpu.bitcast(w & jnp.uint32(0xFFFF0000), F32))
        cols.append(pltpu.bitcast(w << 16, F32))
    return jnp.concatenate(cols, axis=1)


def _route_kernel(x_ref, g_ref, sh_ref, sc_ref, wrh_ref, wrl_ref, hp_ref, im_ref, gm_ref, cnt_ref, carry_ref):
    first = jnp.logical_and(pl.program_id(0) == 0, pl.program_id(1) == 0)

    @pl.when(first)
    def _():
        carry_ref[...] = jnp.zeros_like(carry_ref)

    h = _rms(x_ref[0], g_ref[...]) * (1.0 + sc_ref[0]) + sh_ref[0]
    for p, w in enumerate(_pack_planes(h)):
        hp_ref[p, 0] = w
    logits = _dot3(h, wrh_ref[...], wrl_ref[...])
    tm = logits.shape[0]
    lane = lax.broadcasted_iota(jnp.int32, logits.shape, 1)
    neg = -jnp.inf
    l1 = jnp.where(lane < N_EXPERTS, logits, neg)
    m1 = l1.max(axis=-1, keepdims=True)
    i1 = jnp.where(l1 == m1, lane, LANES).min(axis=-1, keepdims=True)
    l2 = jnp.where(lane == i1, neg, l1)
    m2 = l2.max(axis=-1, keepdims=True)
    i2 = jnp.where(l2 == m2, lane, LANES).min(axis=-1, keepdims=True)
    e = jnp.exp(m2 - m1)
    g1 = 1.0 / (1.0 + e)
    g2 = e / (1.0 + e)
    sel = jnp.logical_or(lane == i1, lane == i2)
    cnt = jnp.where(sel, 1.0, 0.0)
    r = lax.broadcasted_iota(jnp.int32, (tm, tm), 0)
    c = lax.broadcasted_iota(jnp.int32, (tm, tm), 1)
    below = jnp.where(r > c, 1.0, 0.0).astype(BF16)
    before = _dot(below, cnt.astype(BF16)) + carry_ref[...]
    r1 = jnp.where(lane == i1, before, 0.0).sum(axis=-1, keepdims=True).astype(jnp.int32)
    r2 = jnp.where(lane == i2, before, 0.0).sum(axis=-1, keepdims=True).astype(jnp.int32)
    im_ref[0] = jnp.where(lane == 0, i1, jnp.where(lane == 1, i2, jnp.where(lane == 2, r1, r2)))
    gm_ref[0] = jnp.where(lane == 0, g1, g2)
    total = carry_ref[...] + cnt.sum(axis=0, keepdims=True)
    carry_ref[...] = total
    cnt_ref[...] = total


def _route(tok, g, sh, sc, wr_hi, wr_lo, tm):
    b, t, d = tok.shape
    row = lambda n: pl.BlockSpec((1, tm, n), lambda bi, ti: (bi, ti, 0))
    return pl.pallas_call(
        _route_kernel,
        grid=(b, t // tm),
        in_specs=[row(d), _const_spec((1, d)), _batch_row_spec(d), _batch_row_spec(d),
                  _const_spec(wr_hi.shape), _const_spec(wr_lo.shape)],
        out_specs=[pl.BlockSpec((N_PLANES, 1, tm, PLANE_W), lambda bi, ti: (0, bi, ti, 0)),
                   row(LANES), row(LANES), _const_spec((1, LANES))],
        out_shape=[jax.ShapeDtypeStruct((N_PLANES, b, t, PLANE_W), jnp.uint32),
                   jax.ShapeDtypeStruct((b, t, LANES), jnp.int32),
                   jax.ShapeDtypeStruct((b, t, LANES), F32),
                   jax.ShapeDtypeStruct((1, LANES), F32)],
        scratch_shapes=[pltpu.VMEM((1, LANES), F32)],
        compiler_params=_cparams(("arbitrary", "arbitrary")),
        name="moe_route",
    )(tok, g, sh, sc, wr_hi, wr_lo)


def _sc_mesh():
    return plsc.VectorSubcoreMesh(core_axis_name="core", subcore_axis_name="subcore")


def _sc_gather_rows(table, idx):
    n = idx.shape[0]
    w = table.shape[1]

    @pl.kernel(out_type=jax.ShapeDtypeStruct((n, w), table.dtype), mesh=_sc_mesh())
    def gather(t_hbm, i_hbm, o_hbm):
        def body(i_vmem, o_vmem):
            pltpu.sync_copy(t_hbm.at[i_vmem.at[0]], o_vmem)

        pltpu.emit_pipeline(
            body, grid=(n // SC_WINDOW,),
            in_specs=[pl.BlockSpec((1, SC_WINDOW), index_map=lambda i: (0, i))],
            out_specs=[pl.BlockSpec((SC_WINDOW, w), index_map=lambda i: (i, 0))],
            core_axis_name=("core", "subcore"), dimension_semantics=(pltpu.PARALLEL,),
        )(i_hbm, o_hbm)

    return gather(table, idx.reshape(1, n))


def _sc_scatter_rows(src, idx, n_out):
    n = idx.shape[0]
    w = src.shape[1]
    n_src_windows = src.shape[0] // SC_WINDOW

    @pl.kernel(out_type=jax.ShapeDtypeStruct((n_out, w), src.dtype), mesh=_sc_mesh(), scratch_types=[])
    def scatter(s_hbm, i_hbm, o_hbm):
        def body(s_vmem, i_vmem):
            pltpu.sync_copy(s_vmem, o_hbm.at[i_vmem.at[0]])

        pltpu.emit_pipeline(
            body, grid=(n // SC_WINDOW,),
            in_specs=[pl.BlockSpec((SC_WINDOW, w), index_map=lambda i: (i % n_src_windows, 0)),
                      pl.BlockSpec((1, SC_WINDOW), index_map=lambda i: (0, i))],
            out_specs=[],
            core_axis_name=("core", "subcore"), dimension_semantics=(pltpu.PARALLEL,),
        )(s_hbm, i_hbm)

    return scatter(src, idx.reshape(1, n))


def _grouped_ffn_kernel(te_ref, nv_ref, xs_ref, w1_ref, w3_ref, w2_ref, y_ref, hb_ref, acc_ref):
    i = pl.program_id(0)
    f = pl.program_id(1)

    @pl.when(i < nv_ref[0])
    def _():
        @pl.when(f == 0)
        def _():
            hb_ref[...] = _unpack_planes([xs_ref[p] for p in range(N_PLANES)]).astype(BF16)

        half = hb_ref.shape[0] // 2
        for r in range(2):
            rows = slice(r * half, (r + 1) * half)
            h = hb_ref[rows, :]
            a = _dot(h, w1_ref[0])
            mid = (_silu(a) * _dot(h, w3_ref[0])).astype(BF16)
            contrib = _dot(mid, w2_ref[0])

            @pl.when(f == 0)
            def _():
                acc_ref[rows, :] = contrib

            @pl.when(f > 0)
            def _():
                acc_ref[rows, :] += contrib

        @pl.when(f == pl.num_programs(1) - 1)
        def _():
            for p, w in enumerate(_pack_planes(acc_ref[...])):
                y_ref[p] = w


def _grouped_ffn(xs, tile_expert, n_valid, w1, w3, w2, tm, fc):
    n_pad = xs.shape[1]
    ne, d, dff = w1.shape
    plane = pl.BlockSpec((N_PLANES, tm, PLANE_W), lambda i, f, te, nv: (0, i, 0))
    return pl.pallas_call(
        _grouped_ffn_kernel,
        grid_spec=pltpu.PrefetchScalarGridSpec(
            num_scalar_prefetch=2,
            grid=(n_pad // tm, dff // fc),
            in_specs=[plane,
                      pl.BlockSpec((1, d, fc), lambda i, f, te, nv: (te[i], 0, f)),
                      pl.BlockSpec((1, d, fc), lambda i, f, te, nv: (te[i], 0, f)),
                      pl.BlockSpec((1, fc, d), lambda i, f, te, nv: (te[i], f, 0))],
            out_specs=plane,
            scratch_shapes=[pltpu.VMEM((tm, d), BF16), pltpu.VMEM((tm, d), F32)]),
        out_shape=jax.ShapeDtypeStruct(xs.shape, jnp.uint32),
        compiler_params=_cparams(("arbitrary", "arbitrary")),
        name="moe_grouped_ffn",
    )(tile_expert, n_valid, xs, w1, w3, w2)


def _combine_kernel(x_ref, y_ref, gm_ref, g_ref, ga_ref, xo_ref):
    gm = gm_ref[0]
    y1 = _unpack_planes([y_ref[p, 0, 0] for p in range(N_PLANES)])
    y2 = _unpack_planes([y_ref[p, 1, 0] for p in range(N_PLANES)])
    mix = gm[:, 0:1] * y1 + gm[:, 1:2] * y2
    xo_ref[0] = x_ref[0] + ga_ref[0] * _rms(mix, g_ref[...])


def _combine(tok, yg, gm, g, ga, tm):
    b, t, d = tok.shape
    row = lambda n: pl.BlockSpec((1, tm, n), lambda bi, ti: (bi, ti, 0))
    return pl.pallas_call(
        _combine_kernel,
        grid=(b, t // tm),
        in_specs=[row(d),
                  pl.BlockSpec((N_PLANES, 2, 1, tm, PLANE_W), lambda bi, ti: (0, 0, bi, ti, 0)),
                  row(LANES), _const_spec((1, d)), _batch_row_spec(d)],
        out_specs=row(d),
        out_shape=jax.ShapeDtypeStruct((b, t, d), F32),
        compiler_params=_cparams(("parallel", "parallel")),
        name="moe_combine",
    )(tok, yg, gm, g, ga)


def _moe_sorted(tok, g_pre, sh, sc, wr_hi, wr_lo, w1, w3, w2, g_post, ga, tm, tm_e, fc):
    b, t, d = tok.shape
    n_tok = b * t
    hp, im, gm, cnt = _route(tok, g_pre, sh, sc, wr_hi, wr_lo, tm)
    counts = cnt[0, :N_EXPERTS].astype(jnp.int32)
    padded = ((counts + tm_e - 1) // tm_e) * tm_e
    ends = jnp.cumsum(padded)
    starts = ends - padded
    n_pad = 2 * n_tok + N_EXPERTS * tm_e
    n_tiles = n_pad // tm_e
    tile_expert = jnp.minimum(
        jnp.sum((jnp.arange(n_tiles, dtype=jnp.int32)[:, None] * tm_e >= ends[None, :]).astype(jnp.int32), axis=1),
        N_EXPERTS - 1).astype(jnp.int32)
    n_valid = (ends[-1:] // tm_e).astype(jnp.int32)
    im2 = im.reshape(n_tok, LANES)
    pos = jnp.stack([starts[im2[:, 0]] + im2[:, 2], starts[im2[:, 1]] + im2[:, 3]], axis=0)
    plane_off = (jnp.arange(N_PLANES, dtype=jnp.int32) * n_pad)
    idx_dispatch = (pos[:, None, :] + plane_off[None, :, None]).reshape(-1)
    idx_return = (pos[None, :, :] + plane_off[:, None, None]).reshape(-1)
    xs = _sc_scatter_rows(hp.reshape(N_PLANES * n_tok, PLANE_W), idx_dispatch, N_PLANES * n_pad)
    ys = _grouped_ffn(xs.reshape(N_PLANES, n_pad, PLANE_W), tile_expert, n_valid, w1, w3, w2, tm_e, fc)
    yg = _sc_gather_rows(ys.reshape(N_PLANES * n_pad, PLANE_W), idx_return)
    return _combine(tok, yg.reshape(N_PLANES, 2, b, t, PLANE_W), gm, g_post, ga, tm)


_ROPE_SWAP = np.concatenate([np.arange(16, 32), np.arange(0, 16), np.arange(48, 64), np.arange(32, 48)])


def _rope_tables(n_tok):
    rows = n_tok // GRID_W
    row = jnp.broadcast_to(jnp.arange(rows, dtype=F32)[:, None], (rows, GRID_W)).reshape(-1)
    col = jnp.broadcast_to(jnp.arange(GRID_W, dtype=F32)[None, :], (rows, GRID_W)).reshape(-1)
    half = MLA_ROPE // 2
    inv = 1.0 / (ROPE_BASE ** (jnp.arange(0, half, 2, dtype=F32) / half))
    cr, sr = jnp.cos(row[:, None] * inv), jnp.sin(row[:, None] * inv)
    cc, sc = jnp.cos(col[:, None] * inv), jnp.sin(col[:, None] * inv)
    cos64 = jnp.concatenate([cr, cr, cc, cc], axis=-1)
    sin64 = jnp.concatenate([-sr, sr, -sc, sc], axis=-1)
    return (jnp.tile(cos64, (1, MLA_HEADS)), jnp.tile(sin64, (1, MLA_HEADS)), cos64.T, sin64.T)


def _identity_rope_tables(n_tok):
    one = jnp.ones((n_tok, MLA_ROPE), F32)
    zero = jnp.zeros((n_tok, MLA_ROPE), F32)
    return (jnp.tile(one, (1, MLA_HEADS)), jnp.tile(zero, (1, MLA_HEADS)), one.T, zero.T)


def _even_weights(w_in, w_gate_f, b_gate_f, w_gate_b, b_gate_b, q_norm, w_uq, kv_norm, w_ukv):
    nqk = GLA_HEADS * GLA_DK
    nv = GLA_HEADS * GLA_DV
    o_z = 2 * nqk + nv
    o_r = o_z + 2 * GLA_RANK
    o_cq = o_r + nv
    o_kv = o_cq + MLA_Q_RANK
    o_kr = o_kv + MLA_KV_RANK
    d = w_in.shape[0]
    wp = jnp.concatenate([w_in[:, :o_z], w_in[:, o_r:o_kr], w_in[:, o_z:o_r],
                          jnp.zeros((d, LANES - 2 * GLA_RANK), F32)], axis=1).astype(BF16)
    kr = w_in[:, o_kr:o_kr + MLA_ROPE]
    wkr = jnp.concatenate([kr, kr[:, _ROPE_SWAP]], axis=1).T.astype(BF16)
    wg = jnp.zeros((LANES, 2 * nqk), F32)
    wg = wg.at[0:GLA_RANK, 0:nqk].set(w_gate_f).at[GLA_RANK:2 * GLA_RANK, nqk:].set(w_gate_b)
    wg_hi = wg.astype(BF16)
    wg_lo = (wg - wg_hi.astype(F32)).astype(BF16)
    bg = jnp.concatenate([b_gate_f, b_gate_b])[None, :]
    hq = np.arange(MLA_HEADS)[:, None] * MLA_QK
    nope_idx = (hq + np.arange(MLA_NOPE)[None, :]).reshape(-1)
    rope_idx = (hq + MLA_NOPE + np.arange(MLA_ROPE)[None, :]).reshape(-1)
    swap_idx = (hq + MLA_NOPE + _ROPE_SWAP[None, :]).reshape(-1)
    wuq = w_uq[:, np.concatenate([nope_idx, rope_idx, swap_idx])].astype(BF16)
    hk = np.arange(MLA_HEADS)[:, None] * (MLA_NOPE + MLA_V)
    k_idx = (hk + np.arange(MLA_NOPE)[None, :]).reshape(-1)
    v_idx = (hk + MLA_NOPE + np.arange(MLA_V)[None, :]).reshape(-1)
    return dict(wp=wp, wkr=wkr, wg_hi=wg_hi, wg_lo=wg_lo, bg=bg, qn=q_norm[None, :], wuq=wuq,
                kvn=kv_norm[None, :], wuk=w_ukv[:, k_idx].T.astype(BF16), wuv=w_ukv[:, v_idx].astype(BF16))


def _mods(m, rows, batch):
    d = D_MODEL
    if rows is None:
        return [jnp.broadcast_to(m[batch, k * d:(k + 1) * d][None, None, :], (batch, 1, d)) for k in range(6)]
    return [m[:batch, k * d:(k + 1) * d][:, None, :] for k in range(6)]


def kernel(x, c, ctx, c_ctx, w_mod, b_mod, g_mix_pre, g_mix_post, g_ffn_pre, g_ffn_post, e_w_in, e_w_gate_f, e_b_gate_f, e_w_gate_b, e_b_gate_b, e_gla_norm, e_q_norm, e_w_uq, e_kv_norm, e_w_ukv, e_w_o, e_w1, e_w3, e_w2, o_w_o, o_w_router, o_w1, o_w3, o_w2):
    batch, seq, d = x.shape
    n_ctx = ctx.shape[1]
    cond = jnp.zeros((16, d), F32).at[:batch].set(c).at[batch].set(c_ctx)
    mods = _adaln(cond, w_mod, b_mod)
    rope_x = _rope_tables(seq)
    rope_c = _identity_rope_tables(n_ctx)
    nqk = GLA_HEADS * GLA_DK
    last_read = 2 * ((DEPTH - 1) // 2)
    xs, xc = x, ctx
    for i in range(DEPTH):
        j = i // 2
        ctx_live = i <= last_read
        ctx_full = i < last_read
        mx = _mods(mods[i], 0, batch)
        mc = _mods(mods[i], None, batch)
        gpre, gpost = g_mix_pre[i][None, :], g_mix_post[i][None, :]
        fpre, fpost = g_ffn_pre[i][None, :], g_ffn_post[i][None, :]
        if i % 2 == 0:
            w = _even_weights(e_w_in[j], e_w_gate_f[j], e_b_gate_f[j], e_w_gate_b[j], e_b_gate_b[j],
                              e_q_norm[j], e_w_uq[j], e_kv_norm[j], e_w_ukv[j])
            wo1 = e_w_o[j][:GLA_HEADS * GLA_DV].astype(BF16)
            wo2 = e_w_o[j][GLA_HEADS * GLA_DV:].astype(BF16)
            gn = e_gla_norm[j][None, :]
            w1, w3, w2 = e_w1[j].astype(BF16), e_w3[j].astype(BF16), e_w2[j].astype(BF16)
            zero_state = jnp.zeros((batch, GLA_DV, nqk), F32)
            if ctx_live:
                cgq, cgk, cgv, cgr, claf, clab, cq, ckt, cv = _in_proj(xc, gpre, mc[0], mc[1], w, rope_c, n_ctx)
                co_f, co_b, s_f, s_b = _gla(cgq, cgk, cgv, claf, clab, zero_state, zero_state, n_ctx)
                srcs_c = [(ckt, cv)]
            else:
                s_f = s_b = zero_state
                srcs_c = []
            gq, gk, gv, gr, laf, lab, q, kt, v = _in_proj(xs, gpre, mx[0], mx[1], w, rope_x, min(512, seq))
            o_f, o_b, _, _ = _gla(gq, gk, gv, laf, lab, s_f, s_b, min(512, seq))
            a = _attention(q, [(kt, v)] + srcs_c, min(256, seq))
            xs, hx = _even_out(xs, o_f, o_b, gr, a, gn, wo1, wo2, gpost, mx[2], fpre, mx[3], mx[4], min(512, seq))
            xs = _ffn(xs, hx, w1, w3, w2, fpost, mx[5], min(512, seq))
            if ctx_full:
                ac = _attention(cq, srcs_c, n_ctx)
                xc, hc = _even_out(xc, co_f, co_b, cgr, ac, gn, wo1, wo2, gpost, mc[2], fpre, mc[3], mc[4], n_ctx)
                xc = _ffn(xc, hc, w1, w3, w2, fpost, mc[5], n_ctx)
        else:
            wo = o_w_o[j].astype(BF16)
            wr = jnp.zeros((d, LANES), F32).at[:, :N_EXPERTS].set(o_w_router[j])
            wr_hi = wr.astype(BF16)
            wr_lo = (wr - wr_hi.astype(F32)).astype(BF16)
            w1, w3, w2 = o_w1[j].astype(BF16), o_w3[j].astype(BF16), o_w2[j].astype(BF16)
            xs = _fourier_x(xs, gpre, mx[0], mx[1], wo, gpost, mx[2])
            xs = _moe_sorted(xs, fpre, mx[3], mx[4], wr_hi, wr_lo, w1, w3, w2, fpost, mx[5],
                             min(512, seq), min(1024, seq), 1792)
            if ctx_full:
                xc = _fourier_ctx(xc, gpre, mc[0], mc[1], wo, gpost, mc[2])
                hc, gates_c = _router(xc, fpre, mc[3], mc[4], wr_hi, wr_lo, n_ctx)
                xc = _moe_dense(xc, hc, gates_c, w1, w3, w2, fpost, mc[5], n_ctx, 1792)
    return xs
```

```python
import functools

import numpy as np
import jax
import jax.numpy as jnp
from jax import lax
from jax.experimental import pallas as pl
from jax.experimental.pallas import tpu as pltpu
from jax.experimental.pallas import tpu_sc as plsc

F32 = jnp.float32
BF16 = jnp.bfloat16

EPS = 1e-6
D_MODEL = 1024
DEPTH = 4
GRID_W = 64
GLA_HEADS = 4
GLA_DK = 64
GLA_DV = 128
GLA_RANK = 16
GLA_TAU = 16.0
GLA_CHUNK = 64
MLA_HEADS = 4
MLA_Q_RANK = 256
MLA_KV_RANK = 128
MLA_NOPE = 128
MLA_ROPE = 64
MLA_V = 128
MLA_QK = MLA_NOPE + MLA_ROPE
ROPE_BASE = 10000.0
FOURIER_GROUPS = 4
FOURIER_GW = D_MODEL // FOURIER_GROUPS
N_EXPERTS = 8
LANES = 128
VMEM_LIMIT = 48 * 1024 * 1024


def _cparams(sem):
    return pltpu.CompilerParams(dimension_semantics=sem, vmem_limit_bytes=VMEM_LIMIT)


def _dot(a, b):
    return jnp.dot(a, b, preferred_element_type=F32)


def _dot_nt(a, b):
    return lax.dot_general(a, b, (((1,), (1,)), ((), ())), preferred_element_type=F32)


def _dot_tn(a, b):
    return lax.dot_general(a, b, (((0,), (0,)), ((), ())), preferred_element_type=F32)


def _split(x):
    hi = x.astype(BF16)
    lo = (x - hi.astype(F32)).astype(BF16)
    return hi, lo


def _dot3(a, b_hi, b_lo):
    a_hi, a_lo = _split(a)
    return _dot(a_hi, b_hi) + _dot(a_lo, b_hi) + _dot(a_hi, b_lo)


def _rms(x, g):
    return x * lax.rsqrt(jnp.mean(x * x, axis=-1, keepdims=True) + EPS) * g


def _silu(x):
    return x / (1.0 + jnp.exp(-x))


def _const_spec(shape):
    nd = len(shape)
    return pl.BlockSpec(shape, lambda *_: (0,) * nd)


def _batch_row_spec(d):
    return pl.BlockSpec((1, 1, d), lambda b, *_: (b, 0, 0))


def _adaln_kernel(c_ref, w_ref, b_ref, o_ref):
    a = _silu(c_ref[...])
    w_hi, w_lo = _split(w_ref[0])
    o_ref[0] = _dot3(a, w_hi, w_lo) + b_ref[0]


def _adaln(cond, w_mod, b_mod):
    depth, d, n = w_mod.shape
    rows = cond.shape[0]
    bn = 1536
    return pl.pallas_call(
        _adaln_kernel,
        grid=(depth, n // bn),
        in_specs=[
            pl.BlockSpec((rows, d), lambda i, j: (0, 0)),
            pl.BlockSpec((1, d, bn), lambda i, j: (i, 0, j)),
            pl.BlockSpec((1, 1, bn), lambda i, j: (i, 0, j)),
        ],
        out_specs=pl.BlockSpec((1, rows, bn), lambda i, j: (i, 0, j)),
        out_shape=jax.ShapeDtypeStruct((depth, rows, n), F32),
        compiler_params=_cparams(("arbitrary", "arbitrary")),
        name="adaln",
    )(cond, w_mod, b_mod.reshape(depth, 1, n))


def _in_proj_kernel(x_ref, g_ref, sh_ref, sc_ref, wp_ref, wkr_ref, wgh_ref, wgl_ref, bg_ref,
                    qn_ref, wuq_ref, kvn_ref, wuk_ref, wuv_ref, cq_ref, sq_ref, ck_ref, sk_ref,
                    gq_ref, gk_ref, gv_ref, gr_ref, laf_ref, lab_ref, q_ref, kt_ref, v_ref):
    h = _rms(x_ref[0], g_ref[...]) * (1.0 + sc_ref[0]) + sh_ref[0]
    hb = h.astype(BF16)
    p = _dot(hb, wp_ref[...])
    nqk = GLA_HEADS * GLA_DK
    nv = GLA_HEADS * GLA_DV
    gq_ref[0] = p[:, 0:nqk] * (GLA_DK ** -0.5)
    gk_ref[0] = p[:, nqk:2 * nqk]
    gv_ref[0] = p[:, 2 * nqk:2 * nqk + nv]
    gr_ref[0] = p[:, 2 * nqk + nv:2 * nqk + 2 * nv]
    o = 2 * nqk + 2 * nv
    cq = p[:, o:o + MLA_Q_RANK]
    ckv = p[:, o + MLA_Q_RANK:o + MLA_Q_RANK + MLA_KV_RANK]
    tail = p[:, o + MLA_Q_RANK + MLA_KV_RANK:]
    pre = _dot3(tail, wgh_ref[...], wgl_ref[...]) + bg_ref[...]
    la = (jnp.minimum(pre, 0.0) - jnp.log(1.0 + jnp.exp(-jnp.abs(pre)))) * (1.0 / GLA_TAU)
    laf_ref[0] = la[:, :nqk]
    lab_ref[0] = la[:, nqk:]
    q = _dot(_rms(cq, qn_ref[...]).astype(BF16), wuq_ref[...])
    att_scale = MLA_QK ** -0.5
    nn = MLA_HEADS * MLA_NOPE
    nr = MLA_HEADS * MLA_ROPE
    q_rope = q[:, nn:nn + nr] * cq_ref[...] + q[:, nn + nr:] * sq_ref[...]
    for hd in range(MLA_HEADS):
        q_ref[0, hd, :, 0:MLA_NOPE] = (q[:, hd * MLA_NOPE:(hd + 1) * MLA_NOPE] * att_scale).astype(BF16)
        q_ref[0, hd, :, MLA_NOPE:MLA_QK] = (q_rope[:, hd * MLA_ROPE:(hd + 1) * MLA_ROPE] * att_scale).astype(BF16)
    ckvn = _rms(ckv, kvn_ref[...]).astype(BF16)
    kt = _dot_nt(wuk_ref[...], ckvn)
    v = _dot(ckvn, wuv_ref[...])
    kr2 = _dot_nt(wkr_ref[...], hb)
    kr = (kr2[:MLA_ROPE] * ck_ref[...] + kr2[MLA_ROPE:] * sk_ref[...]).astype(BF16)
    for hd in range(MLA_HEADS):
        kt_ref[0, hd, 0:MLA_NOPE, :] = kt[hd * MLA_NOPE:(hd + 1) * MLA_NOPE].astype(BF16)
        kt_ref[0, hd, MLA_NOPE:MLA_QK, :] = kr
        v_ref[0, hd] = v[:, hd * MLA_V:(hd + 1) * MLA_V].astype(BF16)


def _in_proj(tok, g, sh, sc, w, tabs, tm):
    b, t, d = tok.shape
    cq, sq, ck, sk = tabs
    nqk = GLA_HEADS * GLA_DK
    nv = GLA_HEADS * GLA_DV
    row = lambda n: pl.BlockSpec((1, tm, n), lambda bi, ti: (bi, ti, 0))
    weights = (w["wp"], w["wkr"], w["wg_hi"], w["wg_lo"], w["bg"], w["qn"], w["wuq"], w["kvn"],
               w["wuk"], w["wuv"])
    return pl.pallas_call(
        _in_proj_kernel,
        grid=(b, t // tm),
        in_specs=[row(d), _const_spec((1, d)), _batch_row_spec(d), _batch_row_spec(d)]
        + [_const_spec(a.shape) for a in weights]
        + [pl.BlockSpec((tm, MLA_HEADS * MLA_ROPE), lambda bi, ti: (ti, 0)),
           pl.BlockSpec((tm, MLA_HEADS * MLA_ROPE), lambda bi, ti: (ti, 0)),
           pl.BlockSpec((MLA_ROPE, tm), lambda bi, ti: (0, ti)),
           pl.BlockSpec((MLA_ROPE, tm), lambda bi, ti: (0, ti))],
        out_specs=[row(nqk), row(nqk), row(nv), row(nv), row(nqk), row(nqk),
                   pl.BlockSpec((1, MLA_HEADS, tm, MLA_QK), lambda bi, ti: (bi, 0, ti, 0)),
                   pl.BlockSpec((1, MLA_HEADS, MLA_QK, tm), lambda bi, ti: (bi, 0, 0, ti)),
                   pl.BlockSpec((1, MLA_HEADS, tm, MLA_V), lambda bi, ti: (bi, 0, ti, 0))],
        out_shape=[jax.ShapeDtypeStruct((b, t, nqk), F32), jax.ShapeDtypeStruct((b, t, nqk), F32),
                   jax.ShapeDtypeStruct((b, t, nv), F32), jax.ShapeDtypeStruct((b, t, nv), F32),
                   jax.ShapeDtypeStruct((b, t, nqk), F32), jax.ShapeDtypeStruct((b, t, nqk), F32),
                   jax.ShapeDtypeStruct((b, MLA_HEADS, t, MLA_QK), BF16),
                   jax.ShapeDtypeStruct((b, MLA_HEADS, MLA_QK, t), BF16),
                   jax.ShapeDtypeStruct((b, MLA_HEADS, t, MLA_V), BF16)],
        compiler_params=_cparams(("parallel", "parallel")),
        name="even_in_proj",
    )(tok, g, sh, sc, *weights, cq, sq, ck, sk)


def _gla_chunks(streams):
    c = GLA_CHUNK
    heads = range(GLA_HEADS)
    ks = [slice(hd * GLA_DK, (hd + 1) * GLA_DK) for hd in heads]
    vs = [slice(hd * GLA_DV, (hd + 1) * GLA_DV) for hd in heads]
    bcs = []
    for (_, _, _, l_ref, _, _, g, row0, tri, _, _) in streams:
        la_hi, la_lo = _split(l_ref[g, pl.ds(row0, c), :])
        bcs.append(_dot(tri, la_hi) + _dot(tri, la_lo))
    ops = []
    for (q_ref, k_ref, v_ref, _, _, st_ref, g, row0, _, _, last_row), bc in zip(streams, bcs):
        bl = bc[last_row:last_row + 1, :]
        q = q_ref[g, pl.ds(row0, c), :]
        k = k_ref[g, pl.ds(row0, c), :]
        st = st_ref[g]
        ops.append(dict(qc=(q * jnp.exp(bc)).astype(BF16), kc=(k * jnp.exp(-bc)).astype(BF16),
                        kd=(k * jnp.exp(bl - bc)).astype(BF16), vb=v_ref[g, pl.ds(row0, c), :].astype(BF16),
                        st=st, stb=st.astype(BF16), decay=jnp.exp(bl)))
    atts = [[_dot_nt(o["qc"][:, ks[hd]], o["kc"][:, ks[hd]]) for hd in heads] for o in ops]
    inters = [[_dot_nt(o["qc"][:, ks[hd]], o["stb"][:, ks[hd]]) for hd in heads] for o in ops]
    upds = [[_dot_tn(o["vb"][:, vs[hd]], o["kd"][:, ks[hd]]) for hd in heads] for o in ops]
    intras = [[_dot(jnp.where(s[9], att[hd], 0.0).astype(BF16), o["vb"][:, vs[hd]]) for hd in heads]
              for s, o, att in zip(streams, ops, atts)]
    for s, o, intra, inter, upd in zip(streams, ops, intras, inters, upds):
        o_ref, st_ref, g, row0 = s[4], s[5], s[6], s[7]
        o_ref[g, pl.ds(row0, c), :] = jnp.concatenate([a + b for a, b in zip(intra, inter)], axis=1)
        st_ref[g] = o["st"] * o["decay"] + jnp.concatenate(upd, axis=1)


def _gla_kernel(qf, kf, vf, lf, qb, kb, vb, lb, s0f, s0b, of, ob, sff, sfb, stf, stb, *, nc, gb):
    j = pl.program_id(1)

    @pl.when(j == 0)
    def _():
        stf[...] = s0f[...]
        stb[...] = s0b[...]

    c = GLA_CHUNK
    r = lax.broadcasted_iota(jnp.int32, (c, c), 0)
    cc = lax.broadcasted_iota(jnp.int32, (c, c), 1)
    lower = r >= cc
    upper = r <= cc
    tri_l = jnp.where(lower, 1.0, 0.0).astype(BF16)
    tri_u = jnp.where(upper, 1.0, 0.0).astype(BF16)

    for ci in range(nc):
        streams = []
        for g in range(gb):
            streams.append((qf, kf, vf, lf, of, stf, g, ci * c, tri_l, lower, c - 1))
            streams.append((qb, kb, vb, lb, ob, stb, g, (nc - 1 - ci) * c, tri_u, upper, 0))
        _gla_chunks(streams)

    @pl.when(j == pl.num_programs(1) - 1)
    def _():
        sff[...] = stf[...]
        sfb[...] = stb[...]


GLA_BATCH_ROWS = 2


def _gla(gq, gk, gv, laf, lab, s0f, s0b, tb):
    b, t, nqk = gq.shape
    nv = gv.shape[-1]
    nblk = t // tb
    gb = GLA_BATCH_ROWS
    fwd = lambda n: pl.BlockSpec((gb, tb, n), lambda bi, j: (bi, j, 0))
    bwd = lambda n: pl.BlockSpec((gb, tb, n), lambda bi, j: (bi, nblk - 1 - j, 0))
    st = pl.BlockSpec((gb, GLA_DV, nqk), lambda bi, j: (bi, 0, 0))
    return pl.pallas_call(
        functools.partial(_gla_kernel, nc=tb // GLA_CHUNK, gb=gb),
        grid=(b // gb, nblk),
        in_specs=[fwd(nqk), fwd(nqk), fwd(nv), fwd(nqk), bwd(nqk), bwd(nqk), bwd(nv), bwd(nqk), st, st],
        out_specs=[fwd(nv), bwd(nv), st, st],
        out_shape=[jax.ShapeDtypeStruct((b, t, nv), F32), jax.ShapeDtypeStruct((b, t, nv), F32),
                   jax.ShapeDtypeStruct((b, GLA_DV, nqk), F32), jax.ShapeDtypeStruct((b, GLA_DV, nqk), F32)],
        scratch_shapes=[pltpu.VMEM((gb, GLA_DV, nqk), F32), pltpu.VMEM((gb, GLA_DV, nqk), F32)],
        compiler_params=_cparams(("parallel", "arbitrary")),
        name="gla_scan",
    )(gq, gk, gv, laf, gq, gk, gv, lab, s0f, s0b)


ATTN_KEY_CHUNK = 1024


def _attn_kernel(*refs, n_src):
    q_ref = refs[0]
    kts = refs[1:1 + 2 * n_src:2]
    vs = refs[2:2 + 2 * n_src:2]
    o_ref = refs[1 + 2 * n_src]
    vexts = refs[2 + 2 * n_src:]

    @pl.when(pl.program_id(2) == 0)
    def _():
        for v, vext in zip(vs, vexts):
            tk = v.shape[2]
            lane = lax.broadcasted_iota(jnp.int32, (tk, LANES), 1)
            vext[:, 0:MLA_V] = v[0, 0]
            vext[:, MLA_V:MLA_V + LANES] = jnp.where(lane == 0, 1.0, 0.0).astype(BF16)

    q = q_ref[0, 0]
    bq = q.shape[0]
    m = jnp.full((bq, 1), -jnp.inf, F32)
    acc = jnp.zeros((bq, MLA_V + LANES), F32)
    chunks = []
    for kt, vext in zip(kts, vexts):
        tk = kt.shape[3]
        ck = min(ATTN_KEY_CHUNK, tk)
        chunks += [(kt, vext, c * ck, (c + 1) * ck) for c in range(tk // ck)]
    s_next = _dot(q, chunks[0][0][0, 0, :, chunks[0][2]:chunks[0][3]])
    for i, (kt, vext, lo, hi) in enumerate(chunks):
        s = s_next
        if i + 1 < len(chunks):
            kt_n, _, lo_n, hi_n = chunks[i + 1]
            s_next = _dot(q, kt_n[0, 0, :, lo_n:hi_n])
        m_new = jnp.maximum(m, s.max(axis=-1, keepdims=True))
        p = jnp.exp((s - m_new).astype(BF16))
        acc = acc * jnp.exp(m - m_new) + _dot(p, vext[lo:hi, :])
        m = m_new
    o_ref[0] = (acc[:, 0:MLA_V] / acc[:, MLA_V:MLA_V + 1]).astype(o_ref.dtype)


def _attention(q, srcs, bq):
    b, nh, t, dqk = q.shape
    in_specs = [pl.BlockSpec((1, 1, bq, dqk), lambda bi, hi, qi: (bi, hi, qi, 0))]
    args = [q]
    scratch = []
    for kt, v in srcs:
        tk = kt.shape[-1]
        in_specs.append(pl.BlockSpec((1, 1, dqk, tk), lambda bi, hi, qi: (bi, hi, 0, 0)))
        in_specs.append(pl.BlockSpec((1, 1, tk, MLA_V), lambda bi, hi, qi: (bi, hi, 0, 0)))
        args += [kt, v]
        scratch.append(pltpu.VMEM((tk, MLA_V + LANES), BF16))
    return pl.pallas_call(
        functools.partial(_attn_kernel, n_src=len(srcs)),
        grid=(b, nh, t // bq),
        in_specs=in_specs,
        out_specs=pl.BlockSpec((1, bq, MLA_V), lambda bi, hi, qi: (bi, qi, hi)),
        out_shape=jax.ShapeDtypeStruct((b, t, nh * MLA_V), BF16),
        scratch_shapes=scratch,
        compiler_params=_cparams(("parallel", "parallel", "arbitrary")),
        name="mla_attention",
    )(*args)


def _even_out_kernel(x_ref, of_ref, ob_ref, gr_ref, a_ref, gn_ref, wo1_ref, wo2_ref, gpost_ref, ga_ref,
                     gpre_ref, sh_ref, sc_ref, xo_ref, h_ref):
    o = of_ref[0] + ob_ref[0]
    parts = [_rms(o[:, hd * GLA_DV:(hd + 1) * GLA_DV], gn_ref[...]) for hd in range(GLA_HEADS)]
    fin = jnp.concatenate(parts, axis=1) * _silu(gr_ref[0])
    y = _dot(fin.astype(BF16), wo1_ref[...]) + _dot(a_ref[0], wo2_ref[...])
    xn = x_ref[0] + ga_ref[0] * _rms(y, gpost_ref[...])
    xo_ref[0] = xn
    h_ref[0] = (_rms(xn, gpre_ref[...]) * (1.0 + sc_ref[0]) + sh_ref[0]).astype(BF16)


def _even_out(tok, o_f, o_b, gr, a, gn, wo1, wo2, gpost, ga, gpre, sh, sc, tm):
    b, t, d = tok.shape
    nv = o_f.shape[-1]
    row = lambda n: pl.BlockSpec((1, tm, n), lambda bi, ti: (bi, ti, 0))
    return pl.pallas_call(
        _even_out_kernel,
        grid=(b, t // tm),
        in_specs=[row(d), row(nv), row(nv), row(nv), row(a.shape[-1]), _const_spec(gn.shape),
                  _const_spec(wo1.shape), _const_spec(wo2.shape), _const_spec((1, d)), _batch_row_spec(d),
                  _const_spec((1, d)), _batch_row_spec(d), _batch_row_spec(d)],
        out_specs=[row(d), row(d)],
        out_shape=[jax.ShapeDtypeStruct((b, t, d), F32), jax.ShapeDtypeStruct((b, t, d), BF16)],
        compiler_params=_cparams(("parallel", "parallel")),
        name="even_out_proj",
    )(tok, o_f, o_b, gr, a, gn, wo1, wo2, gpost, ga, gpre, sh, sc)


MXU_COLS = 256


def _ff_chunks(dff, max_tiles):
    n_tiles = dff // MXU_COLS
    n_chunks = -(-n_tiles // max_tiles)
    base, extra = divmod(n_tiles, n_chunks)
    bounds, start = [], 0
    for i in range(n_chunks):
        width = (base + (1 if i < extra else 0)) * MXU_COLS
        bounds.append((start, start + width))
        start += width
    return bounds


def _ffn_kernel(x_ref, h_ref, w1_ref, w3_ref, w2_ref, g_ref, ga_ref, xo_ref, *, chunks):
    h = h_ref[0]
    y = None
    for lo, hi in chunks:
        a = _dot(h, w1_ref[:, lo:hi])
        mid = (_silu(a) * _dot(h, w3_ref[:, lo:hi])).astype(BF16)
        part = _dot(mid, w2_ref[lo:hi, :])
        y = part if y is None else y + part
    xo_ref[0] = x_ref[0] + ga_ref[0] * _rms(y, g_ref[...])


def _ffn(tok, h, w1, w3, w2, g, ga, tm):
    b, t, d = tok.shape
    dff = w1.shape[-1]
    row = lambda: pl.BlockSpec((1, tm, d), lambda bi, ti: (bi, ti, 0))
    resident = lambda shape: pl.BlockSpec(shape, lambda bi, ti: (0, 0), pipeline_mode=pl.Buffered(1))
    return pl.pallas_call(
        functools.partial(_ffn_kernel, chunks=_ff_chunks(dff, 6)),
        grid=(b, t // tm),
        in_specs=[row(), row(), resident((d, dff)), resident((d, dff)), resident((dff, d)),
                  _const_spec((1, d)), _batch_row_spec(d)],
        out_specs=row(),
        out_shape=jax.ShapeDtypeStruct((b, t, d), F32),
        compiler_params=_cparams(("parallel", "parallel")),
        name="swiglu_ffn",
    )(tok, h, w1, w3, w2, g, ga)


def _dft_cos_sin(n):
    idx = (np.arange(n)[:, None] * np.arange(n)[None, :]) % n
    ang = 2.0 * np.pi * idx.astype(np.float64) / n
    return np.cos(ang), np.sin(ang)


def _fourier1_kernel(x_ref, g_ref, sh_ref, sc_ref, m1_ref, tc_ref, ts_ref, o_ref, *, nb, n1):
    x = x_ref[0]
    x = x.reshape(n1 * nb, x.shape[-1])
    hb = (_rms(x, g_ref[...]) * (1.0 + sc_ref[0]) + sh_ref[0]).astype(BF16)
    dh = o_ref.shape[-1] // 2
    half = nb * n1
    reps = dh // LANES
    tc = jnp.concatenate([tc_ref[...]] * reps, axis=1)
    ts = jnp.concatenate([ts_ref[...]] * reps, axis=1)
    for c in range(hb.shape[-1] // dh):
        a = _dot(m1_ref[...], hb[:, c * dh:(c + 1) * dh])
        ar, ai = a[:half], a[half:]
        o_ref[0, 0, :, :, c * dh:(c + 1) * dh] = (ar * tc - ai * ts).reshape(nb, n1, dh)
        o_ref[0, 1, :, :, c * dh:(c + 1) * dh] = (ar * ts + ai * tc).reshape(nb, n1, dh)


def _fourier2_kernel(b_ref, x_ref, m2_ref, cc_ref, sc_ref, wo_ref, gpost_ref, ga_ref, xo_ref, *, kb, n2):
    d = b_ref.shape[-1]
    half = n2 * kb
    u = _dot(m2_ref[...], b_ref[0].reshape(2 * half, d).astype(BF16))
    ur = u[:half].astype(BF16)
    ui = u[half:].astype(BF16)
    gw = FOURIER_GW
    f = jnp.concatenate(
        [_dot(ur[:, g * gw:(g + 1) * gw], cc_ref[...]) + _dot(ui[:, g * gw:(g + 1) * gw], sc_ref[...])
         for g in range(FOURIER_GROUPS)], axis=1)
    y = _dot(f.astype(BF16), wo_ref[...])
    xn = x_ref[0].reshape(half, d) + ga_ref[0] * _rms(y, gpost_ref[...])
    xo_ref[0] = xn.reshape(n2, kb, d)


def _fourier_x(tok, g, sh, sc, wo, gpost, ga):
    b, t, d = tok.shape
    n1, n2 = 128, t // 128
    nb, kb = min(8, n2), 8
    c1, s1 = _dft_cos_sin(n1)
    c2, s2 = _dft_cos_sin(n2)
    cg, sg = _dft_cos_sin(FOURIER_GW)
    m1 = np.zeros((2, nb, n1, n1, nb))
    for j in range(nb):
        m1[0, j, :, :, j] = c1 / np.sqrt(n1)
        m1[1, j, :, :, j] = -s1 / np.sqrt(n1)
    m1 = jnp.asarray(m1.reshape(2 * nb * n1, n1 * nb), BF16)
    f2 = np.stack([np.stack([c2, s2], axis=1), np.stack([-s2, c2], axis=1)], axis=0) / np.sqrt(n2)
    m2 = np.zeros((2, n2, kb, 2, n2, kb))
    for l in range(kb):
        m2[:, :, l, :, :, l] = f2
    m2 = jnp.asarray(m2.reshape(2 * n2 * kb, 2 * n2 * kb), BF16)
    ccg = jnp.asarray(cg / np.sqrt(FOURIER_GW), BF16)
    scg = jnp.asarray(sg / np.sqrt(FOURIER_GW), BF16)
    tw = 2.0 * np.pi * ((np.arange(n2)[:, None] * np.arange(n1)[None, :]) % t).astype(np.float64) / t
    tc = jnp.asarray(np.broadcast_to(np.cos(tw).reshape(-1, 1), (n2 * n1, LANES)), F32)
    ts = jnp.asarray(np.broadcast_to(-np.sin(tw).reshape(-1, 1), (n2 * n1, LANES)), F32)
    stage1 = pl.pallas_call(
        functools.partial(_fourier1_kernel, nb=nb, n1=n1),
        grid=(b, n2 // nb),
        in_specs=[pl.BlockSpec((1, n1, nb, d), lambda bi, ji: (bi, 0, ji, 0)),
                  _const_spec((1, d)), _batch_row_spec(d), _batch_row_spec(d),
                  pl.BlockSpec(m1.shape, lambda bi, ji: (0, 0), pipeline_mode=pl.Buffered(1)),
                  pl.BlockSpec((nb * n1, LANES), lambda bi, ji: (ji, 0)),
                  pl.BlockSpec((nb * n1, LANES), lambda bi, ji: (ji, 0))],
        out_specs=pl.BlockSpec((1, 2, nb, n1, d), lambda bi, ji: (bi, 0, ji, 0, 0)),
        out_shape=jax.ShapeDtypeStruct((b, 2, n2, n1, d), F32),
        compiler_params=_cparams(("parallel", "parallel")),
        name="fourier_stage1",
    )(tok.reshape(b, n1, n2, d), g, sh, sc, m1, tc, ts)
    out = pl.pallas_call(
        functools.partial(_fourier2_kernel, kb=kb, n2=n2),
        grid=(b, n1 // kb),
        in_specs=[pl.BlockSpec((1, 2, n2, kb, d), lambda bi, ki: (bi, 0, 0, ki, 0)),
                  pl.BlockSpec((1, n2, kb, d), lambda bi, ki: (bi, 0, ki, 0)),
                  _const_spec(m2.shape), _const_spec(ccg.shape), _const_spec(scg.shape),
                  _const_spec(wo.shape), _const_spec((1, d)), _batch_row_spec(d)],
        out_specs=pl.BlockSpec((1, n2, kb, d), lambda bi, ki: (bi, 0, ki, 0)),
        out_shape=jax.ShapeDtypeStruct((b, n2, n1, d), F32),
        compiler_params=_cparams(("parallel", "parallel")),
        name="fourier_stage2",
    )(stage1, tok.reshape(b, n2, n1, d), m2, ccg, scg, wo, gpost, ga)
    return out.reshape(b, t, d)


def _fourier_ctx_kernel(x_ref, g_ref, sh_ref, sc_ref, fl_ref, cc_ref, scg_ref, wo_ref, gpost_ref, ga_ref,
                        xo_ref):
    x = x_ref[0]
    t = x.shape[0]
    h = _rms(x, g_ref[...]) * (1.0 + sc_ref[0]) + sh_ref[0]
    u = _dot(fl_ref[...], h.astype(BF16))
    ur = u[:t].astype(BF16)
    ui = u[t:].astype(BF16)
    gw = FOURIER_GW
    f = jnp.concatenate(
        [_dot(ur[:, g * gw:(g + 1) * gw], cc_ref[...]) + _dot(ui[:, g * gw:(g + 1) * gw], scg_ref[...])
         for g in range(FOURIER_GROUPS)], axis=1)
    y = _dot(f.astype(BF16), wo_ref[...])
    xo_ref[0] = x + ga_ref[0] * _rms(y, gpost_ref[...])


def _fourier_ctx(tok, g, sh, sc, wo, gpost, ga):
    b, t, d = tok.shape
    cl, sl = _dft_cos_sin(t)
    cg, sg = _dft_cos_sin(FOURIER_GW)
    fl = jnp.asarray(np.concatenate([cl, -sl], axis=0) / np.sqrt(t), BF16)
    ccg = jnp.asarray(cg / np.sqrt(FOURIER_GW), BF16)
    scg = jnp.asarray(sg / np.sqrt(FOURIER_GW), BF16)
    row = pl.BlockSpec((1, t, d), lambda bi: (bi, 0, 0))
    return pl.pallas_call(
        _fourier_ctx_kernel,
        grid=(b,),
        in_specs=[row, _const_spec((1, d)), _batch_row_spec(d), _batch_row_spec(d), _const_spec(fl.shape),
                  _const_spec(ccg.shape), _const_spec(scg.shape), _const_spec(wo.shape),
                  _const_spec((1, d)), _batch_row_spec(d)],
        out_specs=row,
        out_shape=jax.ShapeDtypeStruct((b, t, d), F32),
        compiler_params=_cparams(("parallel",)),
        name="fourier_ctx",
    )(tok, g, sh, sc, fl, ccg, scg, wo, gpost, ga)


PLANE_W = 256
N_PLANES = D_MODEL // (2 * PLANE_W)
SC_WINDOW = 128


def _pack_planes(h):
    out = []
    for p in range(N_PLANES):
        base = 2 * p * PLANE_W
        hi = pltpu.bitcast(h[:, base:base + PLANE_W].astype(BF16).astype(F32), jnp.uint32)
        lo = pltpu.bitcast(h[:, base + PLANE_W:base + 2 * PLANE_W].astype(BF16).astype(F32), jnp.uint32)
        out.append(hi | (lo >> 16))
    return out


def _unpack_planes(planes):
    cols = []
    for w in planes:
        cols.append(pltpu.bitcast(w & jnp.uint32(0xFFFF0000), F32))
        cols.append(pltpu.bitcast(w << 16, F32))
    return jnp.concatenate(cols, axis=1)


def _route_kernel(x_ref, g_ref, sh_ref, sc_ref, wrh_ref, wrl_ref, hp_ref, im_ref, gm_ref, cnt_ref, carry_ref):
    first = jnp.logical_and(pl.program_id(0) == 0, pl.program_id(1) == 0)

    @pl.when(first)
    def _():
        carry_ref[...] = jnp.zeros_like(carry_ref)

    h = _rms(x_ref[0], g_ref[...]) * (1.0 + sc_ref[0]) + sh_ref[0]
    for p, w in enumerate(_pack_planes(h)):
        hp_ref[p, 0] = w
    logits = _dot3(h, wrh_ref[...], wrl_ref[...])
    tm = logits.shape[0]
    lane = lax.broadcasted_iota(jnp.int32, logits.shape, 1)
    neg = -jnp.inf
    l1 = jnp.where(lane < N_EXPERTS, logits, neg)
    m1 = l1.max(axis=-1, keepdims=True)
    i1 = jnp.where(l1 == m1, lane, LANES).min(axis=-1, keepdims=True)
    l2 = jnp.where(lane == i1, neg, l1)
    m2 = l2.max(axis=-1, keepdims=True)
    i2 = jnp.where(l2 == m2, lane, LANES).min(axis=-1, keepdims=True)
    e = jnp.exp(m2 - m1)
    g1 = 1.0 / (1.0 + e)
    g2 = e / (1.0 + e)
    sel = jnp.logical_or(lane == i1, lane == i2)
    cnt = jnp.where(sel, 1.0, 0.0)
    r = lax.broadcasted_iota(jnp.int32, (tm, tm), 0)
    c = lax.broadcasted_iota(jnp.int32, (tm, tm), 1)
    below = jnp.where(r > c, 1.0, 0.0).astype(BF16)
    before = _dot(below, cnt.astype(BF16)) + carry_ref[...]
    r1 = jnp.where(lane == i1, before, 0.0).sum(axis=-1, keepdims=True).astype(jnp.int32)
    r2 = jnp.where(lane == i2, before, 0.0).sum(axis=-1, keepdims=True).astype(jnp.int32)
    im_ref[0] = jnp.where(lane == 0, i1, jnp.where(lane == 1, i2, jnp.where(lane == 2, r1, r2)))
    gm_ref[0] = jnp.where(lane == 0, g1, g2)
    total = carry_ref[...] + cnt.sum(axis=0, keepdims=True)
    carry_ref[...] = total
    cnt_ref[...] = total


def _route(tok, g, sh, sc, wr_hi, wr_lo, tm):
    b, t, d = tok.shape
    row = lambda n: pl.BlockSpec((1, tm, n), lambda bi, ti: (bi, ti, 0))
    return pl.pallas_call(
        _route_kernel,
        grid=(b, t // tm),
        in_specs=[row(d), _const_spec((1, d)), _batch_row_spec(d), _batch_row_spec(d),
                  _const_spec(wr_hi.shape), _const_spec(wr_lo.shape)],
        out_specs=[pl.BlockSpec((N_PLANES, 1, tm, PLANE_W), lambda bi, ti: (0, bi, ti, 0)),
                   row(LANES), row(LANES), _const_spec((1, LANES))],
        out_shape=[jax.ShapeDtypeStruct((N_PLANES, b, t, PLANE_W), jnp.uint32),
                   jax.ShapeDtypeStruct((b, t, LANES), jnp.int32),
                   jax.ShapeDtypeStruct((b, t, LANES), F32),
                   jax.ShapeDtypeStruct((1, LANES), F32)],
        scratch_shapes=[pltpu.VMEM((1, LANES), F32)],
        compiler_params=_cparams(("arbitrary", "arbitrary")),
        name="moe_route",
    )(tok, g, sh, sc, wr_hi, wr_lo)


def _sc_mesh():
    return plsc.VectorSubcoreMesh(core_axis_name="core", subcore_axis_name="subcore")


def _sc_gather_rows(table, idx):
    n = idx.shape[0]
    w = table.shape[1]

    @pl.kernel(out_type=jax.ShapeDtypeStruct((n, w), table.dtype), mesh=_sc_mesh())
    def gather(t_hbm, i_hbm, o_hbm):
        def body(i_vmem, o_vmem):
            pltpu.sync_copy(t_hbm.at[i_vmem.at[0]], o_vmem)

        pltpu.emit_pipeline(
            body, grid=(n // SC_WINDOW,),
            in_specs=[pl.BlockSpec((1, SC_WINDOW), index_map=lambda i: (0, i))],
            out_specs=[pl.BlockSpec((SC_WINDOW, w), index_map=lambda i: (i, 0))],
            core_axis_name=("core", "subcore"), dimension_semantics=(pltpu.PARALLEL,),
        )(i_hbm, o_hbm)

    return gather(table, idx.reshape(1, n))


def _sc_scatter_rows(src, idx, n_out):
    n = idx.shape[0]
    w = src.shape[1]
    n_src_windows = src.shape[0] // SC_WINDOW

    @pl.kernel(out_type=jax.ShapeDtypeStruct((n_out, w), src.dtype), mesh=_sc_mesh(), scratch_types=[])
    def scatter(s_hbm, i_hbm, o_hbm):
        def body(s_vmem, i_vmem):
            pltpu.sync_copy(s_vmem, o_hbm.at[i_vmem.at[0]])

        pltpu.emit_pipeline(
            body, grid=(n // SC_WINDOW,),
            in_specs=[pl.BlockSpec((SC_WINDOW, w), index_map=lambda i: (i % n_src_windows, 0)),
                      pl.BlockSpec((1, SC_WINDOW), index_map=lambda i: (0, i))],
            out_specs=[],
            core_axis_name=("core", "subcore"), dimension_semantics=(pltpu.PARALLEL,),
        )(s_hbm, i_hbm)

    return scatter(src, idx.reshape(1, n))


def _grouped_ffn_kernel(te_ref, nv_ref, xs_ref, w1_ref, w3_ref, w2_ref, y_ref, hb_ref, acc_ref):
    i = pl.program_id(0)
    f = pl.program_id(1)

    @pl.when(i < nv_ref[0])
    def _():
        @pl.when(f == 0)
        def _():
            hb_ref[...] = _unpack_planes([xs_ref[p] for p in range(N_PLANES)]).astype(BF16)

        half = hb_ref.shape[0] // 2
        for r in range(2):
            rows = slice(r * half, (r + 1) * half)
            h = hb_ref[rows, :]
            a = _dot(h, w1_ref[0])
            mid = (_silu(a) * _dot(h, w3_ref[0])).astype(BF16)
            contrib = _dot(mid, w2_ref[0])

            @pl.when(f == 0)
            def _():
                acc_ref[rows, :] = contrib

            @pl.when(f > 0)
            def _():
                acc_ref[rows, :] += contrib

        @pl.when(f == pl.num_programs(1) - 1)
        def _():
            for p, w in enumerate(_pack_planes(acc_ref[...])):
                y_ref[p] = w


def _grouped_ffn(xs, tile_expert, n_valid, w1, w3, w2, tm, fc):
    n_pad = xs.shape[1]
    ne, d, dff = w1.shape
    plane = pl.BlockSpec((N_PLANES, tm, PLANE_W), lambda i, f, te, nv: (0, i, 0))
    return pl.pallas_call(
        _grouped_ffn_kernel,
        grid_spec=pltpu.PrefetchScalarGridSpec(
            num_scalar_prefetch=2,
            grid=(n_pad // tm, dff // fc),
            in_specs=[plane,
                      pl.BlockSpec((1, d, fc), lambda i, f, te, nv: (te[i], 0, f)),
                      pl.BlockSpec((1, d, fc), lambda i, f, te, nv: (te[i], 0, f)),
                      pl.BlockSpec((1, fc, d), lambda i, f, te, nv: (te[i], f, 0))],
            out_specs=plane,
            scratch_shapes=[pltpu.VMEM((tm, d), BF16), pltpu.VMEM((tm, d), F32)]),
        out_shape=jax.ShapeDtypeStruct(xs.shape, jnp.uint32),
        compiler_params=_cparams(("arbitrary", "arbitrary")),
        name="moe_grouped_ffn",
    )(tile_expert, n_valid, xs, w1, w3, w2)


def _combine_kernel(x_ref, y_ref, gm_ref, g_ref, ga_ref, xo_ref):
    gm = gm_ref[0]
    y1 = _unpack_planes([y_ref[p, 0, 0] for p in range(N_PLANES)])
    y2 = _unpack_planes([y_ref[p, 1, 0] for p in range(N_PLANES)])
    mix = gm[:, 0:1] * y1 + gm[:, 1:2] * y2
    xo_ref[0] = x_ref[0] + ga_ref[0] * _rms(mix, g_ref[...])


def _combine(tok, yg, gm, g, ga, tm):
    b, t, d = tok.shape
    row = lambda n: pl.BlockSpec((1, tm, n), lambda bi, ti: (bi, ti, 0))
    return pl.pallas_call(
        _combine_kernel,
        grid=(b, t // tm),
        in_specs=[row(d),
                  pl.BlockSpec((N_PLANES, 2, 1, tm, PLANE_W), lambda bi, ti: (0, 0, bi, ti, 0)),
                  row(LANES), _const_spec((1, d)), _batch_row_spec(d)],
        out_specs=row(d),
        out_shape=jax.ShapeDtypeStruct((b, t, d), F32),
        compiler_params=_cparams(("parallel", "parallel")),
        name="moe_combine",
    )(tok, yg, gm, g, ga)


def _moe_sorted(tok, g_pre, sh, sc, wr_hi, wr_lo, w1, w3, w2, g_post, ga, tm, tm_e, fc):
    b, t, d = tok.shape
    n_tok = b * t
    hp, im, gm, cnt = _route(tok, g_pre, sh, sc, wr_hi, wr_lo, tm)
    counts = cnt[0, :N_EXPERTS].astype(jnp.int32)
    padded = ((counts + tm_e - 1) // tm_e) * tm_e
    ends = jnp.cumsum(padded)
    starts = ends - padded
    n_pad = 2 * n_tok + N_EXPERTS * tm_e
    n_tiles = n_pad // tm_e
    tile_expert = jnp.minimum(
        jnp.sum((jnp.arange(n_tiles, dtype=jnp.int32)[:, None] * tm_e >= ends[None, :]).astype(jnp.int32), axis=1),
        N_EXPERTS - 1).astype(jnp.int32)
    n_valid = (ends[-1:] // tm_e).astype(jnp.int32)
    im2 = im.reshape(n_tok, LANES)
    pos = jnp.stack([starts[im2[:, 0]] + im2[:, 2], starts[im2[:, 1]] + im2[:, 3]], axis=0)
    plane_off = (jnp.arange(N_PLANES, dtype=jnp.int32) * n_pad)
    idx_dispatch = (pos[:, None, :] + plane_off[None, :, None]).reshape(-1)
    idx_return = (pos[None, :, :] + plane_off[:, None, None]).reshape(-1)
    xs = _sc_scatter_rows(hp.reshape(N_PLANES * n_tok, PLANE_W), idx_dispatch, N_PLANES * n_pad)
    ys = _grouped_ffn(xs.reshape(N_PLANES, n_pad, PLANE_W), tile_expert, n_valid, w1, w3, w2, tm_e, fc)
    yg = _sc_gather_rows(ys.reshape(N_PLANES * n_pad, PLANE_W), idx_return)
    return _combine(tok, yg.reshape(N_PLANES, 2, b, t, PLANE_W), gm, g_post, ga, tm)


_ROPE_SWAP = np.concatenate([np.arange(16, 32), np.arange(0, 16), np.arange(48, 64), np.arange(32, 48)])


def _rope_tables(n_tok):
    rows = n_tok // GRID_W
    row = jnp.broadcast_to(jnp.arange(rows, dtype=F32)[:, None], (rows, GRID_W)).reshape(-1)
    col = jnp.broadcast_to(jnp.arange(GRID_W, dtype=F32)[None, :], (rows, GRID_W)).reshape(-1)
    half = MLA_ROPE // 2
    inv = 1.0 / (ROPE_BASE ** (jnp.arange(0, half, 2, dtype=F32) / half))
    cr, sr = jnp.cos(row[:, None] * inv), jnp.sin(row[:, None] * inv)
    cc, sc = jnp.cos(col[:, None] * inv), jnp.sin(col[:, None] * inv)
    cos64 = jnp.concatenate([cr, cr, cc, cc], axis=-1)
    sin64 = jnp.concatenate([-sr, sr, -sc, sc], axis=-1)
    return (jnp.tile(cos64, (1, MLA_HEADS)), jnp.tile(sin64, (1, MLA_HEADS)), cos64.T, sin64.T)


def _identity_rope_tables(n_tok):
    one = jnp.ones((n_tok, MLA_ROPE), F32)
    zero = jnp.zeros((n_tok, MLA_ROPE), F32)
    return (jnp.tile(one, (1, MLA_HEADS)), jnp.tile(zero, (1, MLA_HEADS)), one.T, zero.T)


def _even_weights(w_in, w_gate_f, b_gate_f, w_gate_b, b_gate_b, q_norm, w_uq, kv_norm, w_ukv):
    nqk = GLA_HEADS * GLA_DK
    nv = GLA_HEADS * GLA_DV
    o_z = 2 * nqk + nv
    o_r = o_z + 2 * GLA_RANK
    o_cq = o_r + nv
    o_kv = o_cq + MLA_Q_RANK
    o_kr = o_kv + MLA_KV_RANK
    d = w_in.shape[0]
    wp = jnp.concatenate([w_in[:, :o_z], w_in[:, o_r:o_kr], w_in[:, o_z:o_r],
                          jnp.zeros((d, LANES - 2 * GLA_RANK), F32)], axis=1).astype(BF16)
    kr = w_in[:, o_kr:o_kr + MLA_ROPE]
    wkr = jnp.concatenate([kr, kr[:, _ROPE_SWAP]], axis=1).T.astype(BF16)
    wg = jnp.zeros((LANES, 2 * nqk), F32)
    wg = wg.at[0:GLA_RANK, 0:nqk].set(w_gate_f).at[GLA_RANK:2 * GLA_RANK, nqk:].set(w_gate_b)
    wg_hi = wg.astype(BF16)
    wg_lo = (wg - wg_hi.astype(F32)).astype(BF16)
    bg = jnp.concatenate([b_gate_f, b_gate_b])[None, :]
    hq = np.arange(MLA_HEADS)[:, None] * MLA_QK
    nope_idx = (hq + np.arange(MLA_NOPE)[None, :]).reshape(-1)
    rope_idx = (hq + MLA_NOPE + np.arange(MLA_ROPE)[None, :]).reshape(-1)
    swap_idx = (hq + MLA_NOPE + _ROPE_SWAP[None, :]).reshape(-1)
    wuq = w_uq[:, np.concatenate([nope_idx, rope_idx, swap_idx])].astype(BF16)
    hk = np.arange(MLA_HEADS)[:, None] * (MLA_NOPE + MLA_V)
    k_idx = (hk + np.arange(MLA_NOPE)[None, :]).reshape(-1)
    v_idx = (hk + MLA_NOPE + np.arange(MLA_V)[None, :]).reshape(-1)
    return dict(wp=wp, wkr=wkr, wg_hi=wg_hi, wg_lo=wg_lo, bg=bg, qn=q_norm[None, :], wuq=wuq,
                kvn=kv_norm[None, :], wuk=w_ukv[:, k_idx].T.astype(BF16), wuv=w_ukv[:, v_idx].astype(BF16))


def _mods(m, rows, batch):
    d = D_MODEL
    if rows is None:
        return [jnp.broadcast_to(m[batch, k * d:(k + 1) * d][None, None, :], (batch, 1, d)) for k in range(6)]
    return [m[:batch, k * d:(k + 1) * d][:, None, :] for k in range(6)]


def kernel(x, c, ctx, c_ctx, w_mod, b_mod, g_mix_pre, g_mix_post, g_ffn_pre, g_ffn_post, e_w_in, e_w_gate_f, e_b_gate_f, e_w_gate_b, e_b_gate_b, e_gla_norm, e_q_norm, e_w_uq, e_kv_norm, e_w_ukv, e_w_o, e_w1, e_w3, e_w2, o_w_o, o_w_router, o_w1, o_w3, o_w2):
    batch, seq, d = x.shape
    n_ctx = ctx.shape[1]
    cond = jnp.zeros((16, d), F32).at[:batch].set(c).at[batch].set(c_ctx)
    mods = _adaln(cond, w_mod, b_mod)
    rope_x = _rope_tables(seq)
    rope_c = _identity_rope_tables(n_ctx)
    nqk = GLA_HEADS * GLA_DK
    last_read = 2 * ((DEPTH - 1) // 2)
    xs, xc = x, ctx
    for i in range(DEPTH):
        j = i // 2
        ctx_live = i <= last_read
        ctx_full = i < last_read
        mx = _mods(mods[i], 0, batch)
        mc = _mods(mods[i], None, batch)
        gpre, gpost = g_mix_pre[i][None, :], g_mix_post[i][None, :]
        fpre, fpost = g_ffn_pre[i][None, :], g_ffn_post[i][None, :]
        if i % 2 == 0:
            w = _even_weights(e_w_in[j], e_w_gate_f[j], e_b_gate_f[j], e_w_gate_b[j], e_b_gate_b[j],
                              e_q_norm[j], e_w_uq[j], e_kv_norm[j], e_w_ukv[j])
            wo1 = e_w_o[j][:GLA_HEADS * GLA_DV].astype(BF16)
            wo2 = e_w_o[j][GLA_HEADS * GLA_DV:].astype(BF16)
            gn = e_gla_norm[j][None, :]
            w1, w3, w2 = e_w1[j].astype(BF16), e_w3[j].astype(BF16), e_w2[j].astype(BF16)
            zero_state = jnp.zeros((batch, GLA_DV, nqk), F32)
            if ctx_live:
                cgq, cgk, cgv, cgr, claf, clab, cq, ckt, cv = _in_proj(xc, gpre, mc[0], mc[1], w, rope_c, n_ctx)
                co_f, co_b, s_f, s_b = _gla(cgq, cgk, cgv, claf, clab, zero_state, zero_state, n_ctx)
                srcs_c = [(ckt, cv)]
            else:
                s_f = s_b = zero_state
                srcs_c = []
            gq, gk, gv, gr, laf, lab, q, kt, v = _in_proj(xs, gpre, mx[0], mx[1], w, rope_x, min(512, seq))
            o_f, o_b, _, _ = _gla(gq, gk, gv, laf, lab, s_f, s_b, min(512, seq))
            a = _attention(q, [(kt, v)] + srcs_c, min(256, seq))
            xs, hx = _even_out(xs, o_f, o_b, gr, a, gn, wo1, wo2, gpost, mx[2], fpre, mx[3], mx[4], min(512, seq))
            xs = _ffn(xs, hx, w1, w3, w2, fpost, mx[5], min(512, seq))
            if ctx_full:
                ac = _attention(cq, srcs_c, n_ctx)
                xc, hc = _even_out(xc, co_f, co_b, cgr, ac, gn, wo1, wo2, gpost, mc[2], fpre, mc[3], mc[4], n_ctx)
                xc = _ffn(xc, hc, w1, w3, w2, fpost, mc[5], n_ctx)
        else:
            wo = o_w_o[j].astype(BF16)
            wr = jnp.zeros((d, LANES), F32).at[:, :N_EXPERTS].set(o_w_router[j])
            wr_hi = wr.astype(BF16)
            wr_lo = (wr - wr_hi.astype(F32)).astype(BF16)
            w1, w3, w2 = o_w1[j].astype(BF16), o_w3[j].astype(BF16), o_w2[j].astype(BF16)
            xs = _fourier_x(xs, gpre, mx[0], mx[1], wo, gpost, mx[2])
            xs = _moe_sorted(xs, fpre, mx[3], mx[4], wr_hi, wr_lo, w1, w3, w2, fpost, mx[5],
                             min(512, seq), min(1024, seq), 1792)
            if ctx_full:
                xc = _fourier_ctx(xc, gpre, mc[0], mc[1], wo, gpost, mc[2])
                xc = _moe_sorted(xc, fpre, mc[3], mc[4], wr_hi, wr_lo, w1, w3, w2, fpost, mc[5],
                                 n_ctx, n_ctx, 1792)
    return xs
```

```python
import functools

import numpy as np
import jax
import jax.numpy as jnp
from jax import lax
from jax.experimental import pallas as pl
from jax.experimental.pallas import tpu as pltpu
from jax.experimental.pallas import tpu_sc as plsc

F32 = jnp.float32
BF16 = jnp.bfloat16

EPS = 1e-6
D_MODEL = 1024
DEPTH = 4
GRID_W = 64
GLA_HEADS = 4
GLA_DK = 64
GLA_DV = 128
GLA_RANK = 16
GLA_TAU = 16.0
GLA_CHUNK = 64
MLA_HEADS = 4
MLA_Q_RANK = 256
MLA_KV_RANK = 128
MLA_NOPE = 128
MLA_ROPE = 64
MLA_V = 128
MLA_QK = MLA_NOPE + MLA_ROPE
ROPE_BASE = 10000.0
FOURIER_GROUPS = 4
FOURIER_GW = D_MODEL // FOURIER_GROUPS
N_EXPERTS = 8
LANES = 128
VMEM_LIMIT = 48 * 1024 * 1024


def _cparams(sem):
    return pltpu.CompilerParams(dimension_semantics=sem, vmem_limit_bytes=VMEM_LIMIT)


def _dot(a, b):
    return jnp.dot(a, b, preferred_element_type=F32)


def _dot_nt(a, b):
    return lax.dot_general(a, b, (((1,), (1,)), ((), ())), preferred_element_type=F32)


def _dot_tn(a, b):
    return lax.dot_general(a, b, (((0,), (0,)), ((), ())), preferred_element_type=F32)


def _split(x):
    hi = x.astype(BF16)
    lo = (x - hi.astype(F32)).astype(BF16)
    return hi, lo


def _dot3(a, b_hi, b_lo):
    a_hi, a_lo = _split(a)
    return _dot(a_hi, b_hi) + _dot(a_lo, b_hi) + _dot(a_hi, b_lo)


def _rms(x, g):
    return x * lax.rsqrt(jnp.mean(x * x, axis=-1, keepdims=True) + EPS) * g


def _silu(x):
    return x / (1.0 + jnp.exp(-x))


def _const_spec(shape):
    nd = len(shape)
    return pl.BlockSpec(shape, lambda *_: (0,) * nd)


def _batch_row_spec(d):
    return pl.BlockSpec((1, 1, d), lambda b, *_: (b, 0, 0))


def _adaln_kernel(c_ref, w_ref, b_ref, o_ref):
    a = _silu(c_ref[...])
    w_hi, w_lo = _split(w_ref[0])
    o_ref[0] = _dot3(a, w_hi, w_lo) + b_ref[0]


def _adaln(cond, w_mod, b_mod):
    depth, d, n = w_mod.shape
    rows = cond.shape[0]
    bn = 1536
    return pl.pallas_call(
        _adaln_kernel,
        grid=(depth, n // bn),
        in_specs=[
            pl.BlockSpec((rows, d), lambda i, j: (0, 0)),
            pl.BlockSpec((1, d, bn), lambda i, j: (i, 0, j)),
            pl.BlockSpec((1, 1, bn), lambda i, j: (i, 0, j)),
        ],
        out_specs=pl.BlockSpec((1, rows, bn), lambda i, j: (i, 0, j)),
        out_shape=jax.ShapeDtypeStruct((depth, rows, n), F32),
        compiler_params=_cparams(("arbitrary", "arbitrary")),
        name="adaln",
    )(cond, w_mod, b_mod.reshape(depth, 1, n))


def _in_proj_kernel(x_ref, g_ref, sh_ref, sc_ref, wp_ref, wkr_ref, wgh_ref, wgl_ref, bg_ref,
                    qn_ref, wuq_ref, kvn_ref, wuk_ref, wuv_ref, cq_ref, sq_ref, ck_ref, sk_ref,
                    gq_ref, gk_ref, gv_ref, gr_ref, laf_ref, lab_ref, q_ref, kt_ref, v_ref):
    h = _rms(x_ref[0], g_ref[...]) * (1.0 + sc_ref[0]) + sh_ref[0]
    hb = h.astype(BF16)
    p = _dot(hb, wp_ref[...])
    nqk = GLA_HEADS * GLA_DK
    nv = GLA_HEADS * GLA_DV
    gq_ref[0] = p[:, 0:nqk] * (GLA_DK ** -0.5)
    gk_ref[0] = p[:, nqk:2 * nqk]
    gv_ref[0] = p[:, 2 * nqk:2 * nqk + nv]
    gr_ref[0] = p[:, 2 * nqk + nv:2 * nqk + 2 * nv]
    o = 2 * nqk + 2 * nv
    cq = p[:, o:o + MLA_Q_RANK]
    ckv = p[:, o + MLA_Q_RANK:o + MLA_Q_RANK + MLA_KV_RANK]
    tail = p[:, o + MLA_Q_RANK + MLA_KV_RANK:]
    pre = _dot3(tail, wgh_ref[...], wgl_ref[...]) + bg_ref[...]
    la = (jnp.minimum(pre, 0.0) - jnp.log(1.0 + jnp.exp(-jnp.abs(pre)))) * (1.0 / GLA_TAU)
    laf_ref[0] = la[:, :nqk]
    lab_ref[0] = la[:, nqk:]
    q = _dot(_rms(cq, qn_ref[...]).astype(BF16), wuq_ref[...])
    att_scale = MLA_QK ** -0.5
    nn = MLA_HEADS * MLA_NOPE
    nr = MLA_HEADS * MLA_ROPE
    q_rope = q[:, nn:nn + nr] * cq_ref[...] + q[:, nn + nr:] * sq_ref[...]
    for hd in range(MLA_HEADS):
        q_ref[0, hd, :, 0:MLA_NOPE] = (q[:, hd * MLA_NOPE:(hd + 1) * MLA_NOPE] * att_scale).astype(BF16)
        q_ref[0, hd, :, MLA_NOPE:MLA_QK] = (q_rope[:, hd * MLA_ROPE:(hd + 1) * MLA_ROPE] * att_scale).astype(BF16)
    ckvn = _rms(ckv, kvn_ref[...]).astype(BF16)
    kt = _dot_nt(wuk_ref[...], ckvn)
    v = _dot(ckvn, wuv_ref[...])
    kr2 = _dot_nt(wkr_ref[...], hb)
    kr = (kr2[:MLA_ROPE] * ck_ref[...] + kr2[MLA_ROPE:] * sk_ref[...]).astype(BF16)
    for hd in range(MLA_HEADS):
        kt_ref[0, hd, 0:MLA_NOPE, :] = kt[hd * MLA_NOPE:(hd + 1) * MLA_NOPE].astype(BF16)
        kt_ref[0, hd, MLA_NOPE:MLA_QK, :] = kr
        v_ref[0, hd] = v[:, hd * MLA_V:(hd + 1) * MLA_V].astype(BF16)


def _in_proj(tok, g, sh, sc, w, tabs, tm):
    b, t, d = tok.shape
    cq, sq, ck, sk = tabs
    nqk = GLA_HEADS * GLA_DK
    nv = GLA_HEADS * GLA_DV
    row = lambda n: pl.BlockSpec((1, tm, n), lambda bi, ti: (bi, ti, 0))
    weights = (w["wp"], w["wkr"], w["wg_hi"], w["wg_lo"], w["bg"], w["qn"], w["wuq"], w["kvn"],
               w["wuk"], w["wuv"])
    return pl.pallas_call(
        _in_proj_kernel,
        grid=(b, t // tm),
        in_specs=[row(d), _const_spec((1, d)), _batch_row_spec(d), _batch_row_spec(d)]
        + [_const_spec(a.shape) for a in weights]
        + [pl.BlockSpec((tm, MLA_HEADS * MLA_ROPE), lambda bi, ti: (ti, 0)),
           pl.BlockSpec((tm, MLA_HEADS * MLA_ROPE), lambda bi, ti: (ti, 0)),
           pl.BlockSpec((MLA_ROPE, tm), lambda bi, ti: (0, ti)),
           pl.BlockSpec((MLA_ROPE, tm), lambda bi, ti: (0, ti))],
        out_specs=[row(nqk), row(nqk), row(nv), row(nv), row(nqk), row(nqk),
                   pl.BlockSpec((1, MLA_HEADS, tm, MLA_QK), lambda bi, ti: (bi, 0, ti, 0)),
                   pl.BlockSpec((1, MLA_HEADS, MLA_QK, tm), lambda bi, ti: (bi, 0, 0, ti)),
                   pl.BlockSpec((1, MLA_HEADS, tm, MLA_V), lambda bi, ti: (bi, 0, ti, 0))],
        out_shape=[jax.ShapeDtypeStruct((b, t, nqk), F32), jax.ShapeDtypeStruct((b, t, nqk), F32),
                   jax.ShapeDtypeStruct((b, t, nv), F32), jax.ShapeDtypeStruct((b, t, nv), F32),
                   jax.ShapeDtypeStruct((b, t, nqk), F32), jax.ShapeDtypeStruct((b, t, nqk), F32),
                   jax.ShapeDtypeStruct((b, MLA_HEADS, t, MLA_QK), BF16),
                   jax.ShapeDtypeStruct((b, MLA_HEADS, MLA_QK, t), BF16),
                   jax.ShapeDtypeStruct((b, MLA_HEADS, t, MLA_V), BF16)],
        compiler_params=_cparams(("parallel", "parallel")),
        name="even_in_proj",
    )(tok, g, sh, sc, *weights, cq, sq, ck, sk)


def _gla_chunks(streams):
    c = GLA_CHUNK
    heads = range(GLA_HEADS)
    ks = [slice(hd * GLA_DK, (hd + 1) * GLA_DK) for hd in heads]
    vs = [slice(hd * GLA_DV, (hd + 1) * GLA_DV) for hd in heads]
    bcs = []
    for (_, _, _, l_ref, _, _, g, row0, tri, _, _) in streams:
        la_hi, la_lo = _split(l_ref[g, pl.ds(row0, c), :])
        bcs.append(_dot(tri, la_hi) + _dot(tri, la_lo))
    ops = []
    for (q_ref, k_ref, v_ref, _, _, st_ref, g, row0, _, _, last_row), bc in zip(streams, bcs):
        bl = bc[last_row:last_row + 1, :]
        q = q_ref[g, pl.ds(row0, c), :]
        k = k_ref[g, pl.ds(row0, c), :]
        st = st_ref[g]
        ops.append(dict(qc=(q * jnp.exp(bc)).astype(BF16), kc=(k * jnp.exp(-bc)).astype(BF16),
                        kd=(k * jnp.exp(bl - bc)).astype(BF16), vb=v_ref[g, pl.ds(row0, c), :].astype(BF16),
                        st=st, stb=st.astype(BF16), decay=jnp.exp(bl)))
    atts = [[_dot_nt(o["qc"][:, ks[hd]], o["kc"][:, ks[hd]]) for hd in heads] for o in ops]
    inters = [[_dot_nt(o["qc"][:, ks[hd]], o["stb"][:, ks[hd]]) for hd in heads] for o in ops]
    upds = [[_dot_tn(o["vb"][:, vs[hd]], o["kd"][:, ks[hd]]) for hd in heads] for o in ops]
    intras = [[_dot(jnp.where(s[9], att[hd], 0.0).astype(BF16), o["vb"][:, vs[hd]]) for hd in heads]
              for s, o, att in zip(streams, ops, atts)]
    for s, o, intra, inter, upd in zip(streams, ops, intras, inters, upds):
        o_ref, st_ref, g, row0 = s[4], s[5], s[6], s[7]
        o_ref[g, pl.ds(row0, c), :] = jnp.concatenate([a + b for a, b in zip(intra, inter)], axis=1)
        st_ref[g] = o["st"] * o["decay"] + jnp.concatenate(upd, axis=1)


def _gla_kernel(qf, kf, vf, lf, qb, kb, vb, lb, s0f, s0b, of, ob, sff, sfb, stf, stb, *, nc, gb):
    j = pl.program_id(1)

    @pl.when(j == 0)
    def _():
        stf[...] = s0f[...]
        stb[...] = s0b[...]

    c = GLA_CHUNK
    r = lax.broadcasted_iota(jnp.int32, (c, c), 0)
    cc = lax.broadcasted_iota(jnp.int32, (c, c), 1)
    lower = r >= cc
    upper = r <= cc
    tri_l = jnp.where(lower, 1.0, 0.0).astype(BF16)
    tri_u = jnp.where(upper, 1.0, 0.0).astype(BF16)

    for ci in range(nc):
        streams = []
        for g in range(gb):
            streams.append((qf, kf, vf, lf, of, stf, g, ci * c, tri_l, lower, c - 1))
            streams.append((qb, kb, vb, lb, ob, stb, g, (nc - 1 - ci) * c, tri_u, upper, 0))
        _gla_chunks(streams)

    @pl.when(j == pl.num_programs(1) - 1)
    def _():
        sff[...] = stf[...]
        sfb[...] = stb[...]


GLA_BATCH_ROWS = 4


def _gla(gq, gk, gv, laf, lab, s0f, s0b, tb):
    b, t, nqk = gq.shape
    nv = gv.shape[-1]
    nblk = t // tb
    gb = min(GLA_BATCH_ROWS, b)
    fwd = lambda n: pl.BlockSpec((gb, tb, n), lambda bi, j: (bi, j, 0))
    bwd = lambda n: pl.BlockSpec((gb, tb, n), lambda bi, j: (bi, nblk - 1 - j, 0))
    st = pl.BlockSpec((gb, GLA_DV, nqk), lambda bi, j: (bi, 0, 0))
    return pl.pallas_call(
        functools.partial(_gla_kernel, nc=tb // GLA_CHUNK, gb=gb),
        grid=(b // gb, nblk),
        in_specs=[fwd(nqk), fwd(nqk), fwd(nv), fwd(nqk), bwd(nqk), bwd(nqk), bwd(nv), bwd(nqk), st, st],
        out_specs=[fwd(nv), bwd(nv), st, st],
        out_shape=[jax.ShapeDtypeStruct((b, t, nv), F32), jax.ShapeDtypeStruct((b, t, nv), F32),
                   jax.ShapeDtypeStruct((b, GLA_DV, nqk), F32), jax.ShapeDtypeStruct((b, GLA_DV, nqk), F32)],
        scratch_shapes=[pltpu.VMEM((gb, GLA_DV, nqk), F32), pltpu.VMEM((gb, GLA_DV, nqk), F32)],
        compiler_params=_cparams(("parallel", "arbitrary")),
        name="gla_scan",
    )(gq, gk, gv, laf, gq, gk, gv, lab, s0f, s0b)


ATTN_KEY_CHUNK = 1024


def _attn_kernel(*refs, n_src):
    q_ref = refs[0]
    kts = refs[1:1 + 2 * n_src:2]
    vs = refs[2:2 + 2 * n_src:2]
    o_ref = refs[1 + 2 * n_src]
    vexts = refs[2 + 2 * n_src:]

    @pl.when(pl.program_id(2) == 0)
    def _():
        for v, vext in zip(vs, vexts):
            tk = v.shape[2]
            lane = lax.broadcasted_iota(jnp.int32, (tk, LANES), 1)
            vext[:, 0:MLA_V] = v[0, 0]
            vext[:, MLA_V:MLA_V + LANES] = jnp.where(lane == 0, 1.0, 0.0).astype(BF16)

    q = q_ref[0, 0]
    bq = q.shape[0]
    m = jnp.full((bq, 1), -jnp.inf, F32)
    acc = jnp.zeros((bq, MLA_V + LANES), F32)
    chunks = []
    for kt, vext in zip(kts, vexts):
        tk = kt.shape[3]
        ck = min(ATTN_KEY_CHUNK, tk)
        chunks += [(kt, vext, c * ck, (c + 1) * ck) for c in range(tk // ck)]
    s_next = _dot(q, chunks[0][0][0, 0, :, chunks[0][2]:chunks[0][3]])
    for i, (kt, vext, lo, hi) in enumerate(chunks):
        s = s_next
        if i + 1 < len(chunks):
            kt_n, _, lo_n, hi_n = chunks[i + 1]
            s_next = _dot(q, kt_n[0, 0, :, lo_n:hi_n])
        m_new = jnp.maximum(m, s.max(axis=-1, keepdims=True))
        p = jnp.exp((s - m_new).astype(BF16))
        acc = acc * jnp.exp(m - m_new) + _dot(p, vext[lo:hi, :])
        m = m_new
    o_ref[0] = (acc[:, 0:MLA_V] / acc[:, MLA_V:MLA_V + 1]).astype(o_ref.dtype)


def _attention(q, srcs, bq):
    b, nh, t, dqk = q.shape
    in_specs = [pl.BlockSpec((1, 1, bq, dqk), lambda bi, hi, qi: (bi, hi, qi, 0))]
    args = [q]
    scratch = []
    for kt, v in srcs:
        tk = kt.shape[-1]
        in_specs.append(pl.BlockSpec((1, 1, dqk, tk), lambda bi, hi, qi: (bi, hi, 0, 0)))
        in_specs.append(pl.BlockSpec((1, 1, tk, MLA_V), lambda bi, hi, qi: (bi, hi, 0, 0)))
        args += [kt, v]
        scratch.append(pltpu.VMEM((tk, MLA_V + LANES), BF16))
    return pl.pallas_call(
        functools.partial(_attn_kernel, n_src=len(srcs)),
        grid=(b, nh, t // bq),
        in_specs=in_specs,
        out_specs=pl.BlockSpec((1, bq, MLA_V), lambda bi, hi, qi: (bi, qi, hi)),
        out_shape=jax.ShapeDtypeStruct((b, t, nh * MLA_V), BF16),
        scratch_shapes=scratch,
        compiler_params=_cparams(("parallel", "parallel", "arbitrary")),
        name="mla_attention",
    )(*args)


MXU_COLS = 256


def _ff_chunks(dff, max_tiles):
    n_tiles = dff // MXU_COLS
    n_chunks = -(-n_tiles // max_tiles)
    base, extra = divmod(n_tiles, n_chunks)
    bounds, start = [], 0
    for i in range(n_chunks):
        width = (base + (1 if i < extra else 0)) * MXU_COLS
        bounds.append((start, start + width))
        start += width
    return bounds


def _even_tail_kernel(x_ref, of_ref, ob_ref, gr_ref, a_ref, gn_ref, wo1_ref, wo2_ref, gpost_ref, ga_ref,
                      gpre_ref, sh_ref, sc_ref, w1_ref, w3_ref, w2_ref, fpost_ref, gaf_ref, xo_ref, *, chunks):
    o = of_ref[0] + ob_ref[0]
    parts = [_rms(o[:, hd * GLA_DV:(hd + 1) * GLA_DV], gn_ref[...]) for hd in range(GLA_HEADS)]
    fin = jnp.concatenate(parts, axis=1) * _silu(gr_ref[0])
    y = _dot(fin.astype(BF16), wo1_ref[...]) + _dot(a_ref[0], wo2_ref[...])
    xn = x_ref[0] + ga_ref[0] * _rms(y, gpost_ref[...])
    h = (_rms(xn, gpre_ref[...]) * (1.0 + sc_ref[0]) + sh_ref[0]).astype(BF16)
    z = None
    for lo, hi in chunks:
        a = _dot(h, w1_ref[:, lo:hi])
        mid = (_silu(a) * _dot(h, w3_ref[:, lo:hi])).astype(BF16)
        part = _dot(mid, w2_ref[lo:hi, :])
        z = part if z is None else z + part
    xo_ref[0] = xn + gaf_ref[0] * _rms(z, fpost_ref[...])


def _even_tail(tok, o_f, o_b, gr, a, gn, wo1, wo2, gpost, ga, gpre, sh, sc, w1, w3, w2, fpost, gaf, tm):
    b, t, d = tok.shape
    nv = o_f.shape[-1]
    dff = w1.shape[-1]
    row = lambda n: pl.BlockSpec((1, tm, n), lambda bi, ti: (bi, ti, 0))
    resident = lambda shape: pl.BlockSpec(shape, lambda bi, ti: (0, 0), pipeline_mode=pl.Buffered(1))
    return pl.pallas_call(
        functools.partial(_even_tail_kernel, chunks=_ff_chunks(dff, 6)),
        grid=(b, t // tm),
        in_specs=[row(d), row(nv), row(nv), row(nv), row(a.shape[-1]), _const_spec(gn.shape),
                  resident(wo1.shape), resident(wo2.shape), _const_spec((1, d)), _batch_row_spec(d),
                  _const_spec((1, d)), _batch_row_spec(d), _batch_row_spec(d),
                  resident((d, dff)), resident((d, dff)), resident((dff, d)),
                  _const_spec((1, d)), _batch_row_spec(d)],
        out_specs=row(d),
        out_shape=jax.ShapeDtypeStruct((b, t, d), F32),
        compiler_params=_cparams(("parallel", "parallel")),
        name="even_tail",
    )(tok, o_f, o_b, gr, a, gn, wo1, wo2, gpost, ga, gpre, sh, sc, w1, w3, w2, fpost, gaf)


def _dft_cos_sin(n):
    idx = (np.arange(n)[:, None] * np.arange(n)[None, :]) % n
    ang = 2.0 * np.pi * idx.astype(np.float64) / n
    return np.cos(ang), np.sin(ang)


def _fourier1_kernel(x_ref, g_ref, sh_ref, sc_ref, m1_ref, tc_ref, ts_ref, o_ref, *, nb, n1):
    x = x_ref[0]
    x = x.reshape(n1 * nb, x.shape[-1])
    hb = (_rms(x, g_ref[...]) * (1.0 + sc_ref[0]) + sh_ref[0]).astype(BF16)
    dh = o_ref.shape[-1] // 2
    half = nb * n1
    reps = dh // LANES
    tc = jnp.concatenate([tc_ref[...]] * reps, axis=1)
    ts = jnp.concatenate([ts_ref[...]] * reps, axis=1)
    for c in range(hb.shape[-1] // dh):
        a = _dot(m1_ref[...], hb[:, c * dh:(c + 1) * dh])
        ar, ai = a[:half], a[half:]
        o_ref[0, 0, :, :, c * dh:(c + 1) * dh] = (ar * tc - ai * ts).reshape(nb, n1, dh)
        o_ref[0, 1, :, :, c * dh:(c + 1) * dh] = (ar * ts + ai * tc).reshape(nb, n1, dh)


def _fourier2_kernel(b_ref, x_ref, m2_ref, cc_ref, sc_ref, wo_ref, gpost_ref, ga_ref, xo_ref, *, kb, n2):
    d = b_ref.shape[-1]
    half = n2 * kb
    u = _dot(m2_ref[...], b_ref[0].reshape(2 * half, d).astype(BF16))
    ur = u[:half].astype(BF16)
    ui = u[half:].astype(BF16)
    gw = FOURIER_GW
    f = jnp.concatenate(
        [_dot(ur[:, g * gw:(g + 1) * gw], cc_ref[...]) + _dot(ui[:, g * gw:(g + 1) * gw], sc_ref[...])
         for g in range(FOURIER_GROUPS)], axis=1)
    y = _dot(f.astype(BF16), wo_ref[...])
    xn = x_ref[0].reshape(half, d) + ga_ref[0] * _rms(y, gpost_ref[...])
    xo_ref[0] = xn.reshape(n2, kb, d)


def _fourier_x(tok, g, sh, sc, wo, gpost, ga):
    b, t, d = tok.shape
    n1, n2 = 128, t // 128
    nb, kb = min(8, n2), 8
    c1, s1 = _dft_cos_sin(n1)
    c2, s2 = _dft_cos_sin(n2)
    cg, sg = _dft_cos_sin(FOURIER_GW)
    m1 = np.zeros((2, nb, n1, n1, nb))
    for j in range(nb):
        m1[0, j, :, :, j] = c1 / np.sqrt(n1)
        m1[1, j, :, :, j] = -s1 / np.sqrt(n1)
    m1 = jnp.asarray(m1.reshape(2 * nb * n1, n1 * nb), BF16)
    f2 = np.stack([np.stack([c2, s2], axis=1), np.stack([-s2, c2], axis=1)], axis=0) / np.sqrt(n2)
    m2 = np.zeros((2, n2, kb, 2, n2, kb))
    for l in range(kb):
        m2[:, :, l, :, :, l] = f2
    m2 = jnp.asarray(m2.reshape(2 * n2 * kb, 2 * n2 * kb), BF16)
    ccg = jnp.asarray(cg / np.sqrt(FOURIER_GW), BF16)
    scg = jnp.asarray(sg / np.sqrt(FOURIER_GW), BF16)
    tw = 2.0 * np.pi * ((np.arange(n2)[:, None] * np.arange(n1)[None, :]) % t).astype(np.float64) / t
    tc = jnp.asarray(np.broadcast_to(np.cos(tw).reshape(-1, 1), (n2 * n1, LANES)), F32)
    ts = jnp.asarray(np.broadcast_to(-np.sin(tw).reshape(-1, 1), (n2 * n1, LANES)), F32)
    stage1 = pl.pallas_call(
        functools.partial(_fourier1_kernel, nb=nb, n1=n1),
        grid=(b, n2 // nb),
        in_specs=[pl.BlockSpec((1, n1, nb, d), lambda bi, ji: (bi, 0, ji, 0)),
                  _const_spec((1, d)), _batch_row_spec(d), _batch_row_spec(d),
                  pl.BlockSpec(m1.shape, lambda bi, ji: (0, 0), pipeline_mode=pl.Buffered(1)),
                  pl.BlockSpec((nb * n1, LANES), lambda bi, ji: (ji, 0)),
                  pl.BlockSpec((nb * n1, LANES), lambda bi, ji: (ji, 0))],
        out_specs=pl.BlockSpec((1, 2, nb, n1, d), lambda bi, ji: (bi, 0, ji, 0, 0)),
        out_shape=jax.ShapeDtypeStruct((b, 2, n2, n1, d), F32),
        compiler_params=_cparams(("parallel", "parallel")),
        name="fourier_stage1",
    )(tok.reshape(b, n1, n2, d), g, sh, sc, m1, tc, ts)
    freq = lambda n: pl.BlockSpec((1, n2, kb, n), lambda bi, ki: (bi, 0, ki, 0))
    out = pl.pallas_call(
        functools.partial(_fourier2_kernel, kb=kb, n2=n2),
        grid=(b, n1 // kb),
        in_specs=[pl.BlockSpec((1, 2, n2, kb, d), lambda bi, ki: (bi, 0, 0, ki, 0)),
                  freq(d),
                  _const_spec(m2.shape), _const_spec(ccg.shape), _const_spec(scg.shape),
                  _const_spec(wo.shape), _const_spec((1, d)), _batch_row_spec(d)],
        out_specs=freq(d),
        out_shape=jax.ShapeDtypeStruct((b, n2, n1, d), F32),
        compiler_params=_cparams(("parallel", "parallel")),
        name="fourier_stage2",
    )(stage1, tok.reshape(b, n2, n1, d), m2, ccg, scg, wo, gpost, ga)
    return out.reshape(b, t, d)


def _fourier_ctx_kernel(x_ref, g_ref, sh_ref, sc_ref, fl_ref, cc_ref, scg_ref, wo_ref, gpost_ref, ga_ref,
                        xo_ref):
    x = x_ref[0]
    t = x.shape[0]
    h = _rms(x, g_ref[...]) * (1.0 + sc_ref[0]) + sh_ref[0]
    u = _dot(fl_ref[...], h.astype(BF16))
    ur = u[:t].astype(BF16)
    ui = u[t:].astype(BF16)
    gw = FOURIER_GW
    f = jnp.concatenate(
        [_dot(ur[:, g * gw:(g + 1) * gw], cc_ref[...]) + _dot(ui[:, g * gw:(g + 1) * gw], scg_ref[...])
         for g in range(FOURIER_GROUPS)], axis=1)
    y = _dot(f.astype(BF16), wo_ref[...])
    xo_ref[0] = x + ga_ref[0] * _rms(y, gpost_ref[...])


def _fourier_ctx(tok, g, sh, sc, wo, gpost, ga):
    b, t, d = tok.shape
    cl, sl = _dft_cos_sin(t)
    cg, sg = _dft_cos_sin(FOURIER_GW)
    fl = jnp.asarray(np.concatenate([cl, -sl], axis=0) / np.sqrt(t), BF16)
    ccg = jnp.asarray(cg / np.sqrt(FOURIER_GW), BF16)
    scg = jnp.asarray(sg / np.sqrt(FOURIER_GW), BF16)
    row = pl.BlockSpec((1, t, d), lambda bi: (bi, 0, 0))
    return pl.pallas_call(
        _fourier_ctx_kernel,
        grid=(b,),
        in_specs=[row, _const_spec((1, d)), _batch_row_spec(d), _batch_row_spec(d), _const_spec(fl.shape),
                  _const_spec(ccg.shape), _const_spec(scg.shape), _const_spec(wo.shape),
                  _const_spec((1, d)), _batch_row_spec(d)],
        out_specs=row,
        out_shape=jax.ShapeDtypeStruct((b, t, d), F32),
        compiler_params=_cparams(("parallel",)),
        name="fourier_ctx",
    )(tok, g, sh, sc, fl, ccg, scg, wo, gpost, ga)


PLANE_W = 256
N_PLANES = D_MODEL // (2 * PLANE_W)
SC_WINDOW = 128


def _pack_planes(h):
    out = []
    for p in range(N_PLANES):
        base = 2 * p * PLANE_W
        hi = pltpu.bitcast(h[:, base:base + PLANE_W].astype(BF16).astype(F32), jnp.uint32)
        lo = pltpu.bitcast(h[:, base + PLANE_W:base + 2 * PLANE_W].astype(BF16).astype(F32), jnp.uint32)
        out.append(hi | (lo >> 16))
    return out


def _unpack_planes(planes):
    cols = []
    for w in planes:
        cols.append(pltpu.bitcast(w & jnp.uint32(0xFFFF0000), F32))
        cols.append(pltpu.bitcast(w << 16, F32))
    return jnp.concatenate(cols, axis=1)


def _route_rows(h, wrh_ref, wrl_ref, carry_ref):
    logits = _dot3(h, wrh_ref[...], wrl_ref[...])
    tm = logits.shape[0]
    lane = lax.broadcasted_iota(jnp.int32, logits.shape, 1)
    neg = -jnp.inf
    l1 = jnp.where(lane < N_EXPERTS, logits, neg)
    m1 = l1.max(axis=-1, keepdims=True)
    i1 = jnp.where(l1 == m1, lane, LANES).min(axis=-1, keepdims=True)
    l2 = jnp.where(lane == i1, neg, l1)
    m2 = l2.max(axis=-1, keepdims=True)
    i2 = jnp.where(l2 == m2, lane, LANES).min(axis=-1, keepdims=True)
    e = jnp.exp(m2 - m1)
    g1 = 1.0 / (1.0 + e)
    g2 = e / (1.0 + e)
    sel = jnp.logical_or(lane == i1, lane == i2)
    cnt = jnp.where(sel, 1.0, 0.0)
    r = lax.broadcasted_iota(jnp.int32, (tm, tm), 0)
    c = lax.broadcasted_iota(jnp.int32, (tm, tm), 1)
    below = jnp.where(r > c, 1.0, 0.0).astype(BF16)
    before = _dot(below, cnt.astype(BF16)) + carry_ref[...]
    r1 = jnp.where(lane == i1, before, 0.0).sum(axis=-1, keepdims=True).astype(jnp.int32)
    r2 = jnp.where(lane == i2, before, 0.0).sum(axis=-1, keepdims=True).astype(jnp.int32)
    carry_ref[...] = carry_ref[...] + cnt.sum(axis=0, keepdims=True)
    im = jnp.where(lane == 0, i1, jnp.where(lane == 1, i2, jnp.where(lane == 2, r1, r2)))
    return im, jnp.where(lane == 0, g1, g2)


def _route_kernel(x_ref, g_ref, sh_ref, sc_ref, wrh_ref, wrl_ref, hp_ref, im_ref, gm_ref, cnt_ref, carry_ref):
    first = jnp.logical_and(pl.program_id(0) == 0, pl.program_id(1) == 0)

    @pl.when(first)
    def _():
        carry_ref[...] = jnp.zeros_like(carry_ref)

    h = _rms(x_ref[0], g_ref[...]) * (1.0 + sc_ref[0]) + sh_ref[0]
    for p, w in enumerate(_pack_planes(h)):
        hp_ref[p, 0] = w
    im_ref[0], gm_ref[0] = _route_rows(h, wrh_ref, wrl_ref, carry_ref)
    cnt_ref[...] = carry_ref[...]


def _route(tok, g, sh, sc, wr_hi, wr_lo, tm):
    b, t, d = tok.shape
    row = lambda n: pl.BlockSpec((1, tm, n), lambda bi, ti: (bi, ti, 0))
    return pl.pallas_call(
        _route_kernel,
        grid=(b, t // tm),
        in_specs=[row(d), _const_spec((1, d)), _batch_row_spec(d), _batch_row_spec(d),
                  _const_spec(wr_hi.shape), _const_spec(wr_lo.shape)],
        out_specs=[pl.BlockSpec((N_PLANES, 1, tm, PLANE_W), lambda bi, ti: (0, bi, ti, 0)),
                   row(LANES), row(LANES), _const_spec((1, LANES))],
        out_shape=[jax.ShapeDtypeStruct((N_PLANES, b, t, PLANE_W), jnp.uint32),
                   jax.ShapeDtypeStruct((b, t, LANES), jnp.int32),
                   jax.ShapeDtypeStruct((b, t, LANES), F32),
                   jax.ShapeDtypeStruct((1, LANES), F32)],
        scratch_shapes=[pltpu.VMEM((1, LANES), F32)],
        compiler_params=_cparams(("arbitrary", "arbitrary")),
        name="moe_route",
    )(tok, g, sh, sc, wr_hi, wr_lo)


def _sc_mesh():
    return plsc.VectorSubcoreMesh(core_axis_name="core", subcore_axis_name="subcore")


def _sc_gather_rows(table, idx):
    n = idx.shape[0]
    w = table.shape[1]

    @pl.kernel(out_type=jax.ShapeDtypeStruct((n, w), table.dtype), mesh=_sc_mesh())
    def gather(t_hbm, i_hbm, o_hbm):
        def body(i_vmem, o_vmem):
            pltpu.sync_copy(t_hbm.at[i_vmem.at[0]], o_vmem)

        pltpu.emit_pipeline(
            body, grid=(n // SC_WINDOW,),
            in_specs=[pl.BlockSpec((1, SC_WINDOW), index_map=lambda i: (0, i))],
            out_specs=[pl.BlockSpec((SC_WINDOW, w), index_map=lambda i: (i, 0))],
            core_axis_name=("core", "subcore"), dimension_semantics=(pltpu.PARALLEL,),
        )(i_hbm, o_hbm)

    return gather(table, idx.reshape(1, n))


def _sc_scatter_rows(src, idx, n_out):
    n = idx.shape[0]
    w = src.shape[1]
    n_src_windows = src.shape[0] // SC_WINDOW

    @pl.kernel(out_type=jax.ShapeDtypeStruct((n_out, w), src.dtype), mesh=_sc_mesh(), scratch_types=[])
    def scatter(s_hbm, i_hbm, o_hbm):
        def body(s_vmem, i_vmem):
            pltpu.sync_copy(s_vmem, o_hbm.at[i_vmem.at[0]])

        pltpu.emit_pipeline(
            body, grid=(n // SC_WINDOW,),
            in_specs=[pl.BlockSpec((SC_WINDOW, w), index_map=lambda i: (i % n_src_windows, 0)),
                      pl.BlockSpec((1, SC_WINDOW), index_map=lambda i: (0, i))],
            out_specs=[],
            core_axis_name=("core", "subcore"), dimension_semantics=(pltpu.PARALLEL,),
        )(s_hbm, i_hbm)

    return scatter(src, idx.reshape(1, n))


def _grouped_ffn_kernel(te_ref, nv_ref, xs_ref, w1_ref, w3_ref, w2_ref, y_ref, hb_ref, acc_ref):
    i = pl.program_id(0)
    f = pl.program_id(1)

    @pl.when(i < nv_ref[0])
    def _():
        @pl.when(f == 0)
        def _():
            hb_ref[...] = _unpack_planes([xs_ref[p] for p in range(N_PLANES)]).astype(BF16)

        half = hb_ref.shape[0] // 2
        for r in range(2):
            rows = slice(r * half, (r + 1) * half)
            h = hb_ref[rows, :]
            a = _dot(h, w1_ref[0])
            mid = (_silu(a) * _dot(h, w3_ref[0])).astype(BF16)
            contrib = _dot(mid, w2_ref[0])

            @pl.when(f == 0)
            def _():
                acc_ref[rows, :] = contrib

            @pl.when(f > 0)
            def _():
                acc_ref[rows, :] += contrib

        @pl.when(f == pl.num_programs(1) - 1)
        def _():
            for p, w in enumerate(_pack_planes(acc_ref[...])):
                y_ref[p] = w


def _grouped_ffn(xs, tile_expert, n_valid, w1, w3, w2, tm, fc):
    n_pad = xs.shape[1]
    ne, d, dff = w1.shape
    plane = pl.BlockSpec((N_PLANES, tm, PLANE_W), lambda i, f, te, nv: (0, i, 0))
    return pl.pallas_call(
        _grouped_ffn_kernel,
        grid_spec=pltpu.PrefetchScalarGridSpec(
            num_scalar_prefetch=2,
            grid=(n_pad // tm, dff // fc),
            in_specs=[plane,
                      pl.BlockSpec((1, d, fc), lambda i, f, te, nv: (te[i], 0, f)),
                      pl.BlockSpec((1, d, fc), lambda i, f, te, nv: (te[i], 0, f)),
                      pl.BlockSpec((1, fc, d), lambda i, f, te, nv: (te[i], f, 0))],
            out_specs=plane,
            scratch_shapes=[pltpu.VMEM((tm, d), BF16), pltpu.VMEM((tm, d), F32)]),
        out_shape=jax.ShapeDtypeStruct(xs.shape, jnp.uint32),
        compiler_params=_cparams(("arbitrary", "arbitrary")),
        name="moe_grouped_ffn",
    )(tile_expert, n_valid, xs, w1, w3, w2)


def _combine_kernel(x_ref, y_ref, gm_ref, g_ref, ga_ref, xo_ref):
    gm = gm_ref[0]
    y1 = _unpack_planes([y_ref[p, 0, 0] for p in range(N_PLANES)])
    y2 = _unpack_planes([y_ref[p, 1, 0] for p in range(N_PLANES)])
    mix = gm[:, 0:1] * y1 + gm[:, 1:2] * y2
    xo_ref[0] = x_ref[0] + ga_ref[0] * _rms(mix, g_ref[...])


def _combine(tok, yg, gm, g, ga, tm):
    b, t, d = tok.shape
    row = lambda n: pl.BlockSpec((1, tm, n), lambda bi, ti: (bi, ti, 0))
    return pl.pallas_call(
        _combine_kernel,
        grid=(b, t // tm),
        in_specs=[row(d),
                  pl.BlockSpec((N_PLANES, 2, 1, tm, PLANE_W), lambda bi, ti: (0, 0, bi, ti, 0)),
                  row(LANES), _const_spec((1, d)), _batch_row_spec(d)],
        out_specs=row(d),
        out_shape=jax.ShapeDtypeStruct((b, t, d), F32),
        compiler_params=_cparams(("parallel", "parallel")),
        name="moe_combine",
    )(tok, yg, gm, g, ga)


def _moe_sorted(tok, g_pre, sh, sc, wr_hi, wr_lo, w1, w3, w2, g_post, ga, tm, tm_e, fc):
    routed = _route(tok, g_pre, sh, sc, wr_hi, wr_lo, tm)
    return _moe_experts(tok, routed, w1, w3, w2, g_post, ga, tm, tm_e, fc)


def _moe_experts(tok, routed, w1, w3, w2, g_post, ga, tm, tm_e, fc):
    b, t, d = tok.shape
    n_tok = b * t
    hp, im, gm, cnt = routed
    counts = cnt[0, :N_EXPERTS].astype(jnp.int32)
    padded = ((counts + tm_e - 1) // tm_e) * tm_e
    ends = jnp.cumsum(padded)
    starts = ends - padded
    n_pad = 2 * n_tok + N_EXPERTS * tm_e
    n_tiles = n_pad // tm_e
    tile_expert = jnp.minimum(
        jnp.sum((jnp.arange(n_tiles, dtype=jnp.int32)[:, None] * tm_e >= ends[None, :]).astype(jnp.int32), axis=1),
        N_EXPERTS - 1).astype(jnp.int32)
    n_valid = (ends[-1:] // tm_e).astype(jnp.int32)
    im2 = im.reshape(n_tok, LANES)
    pos = jnp.stack([starts[im2[:, 0]] + im2[:, 2], starts[im2[:, 1]] + im2[:, 3]], axis=0)
    plane_off = (jnp.arange(N_PLANES, dtype=jnp.int32) * n_pad)
    idx_dispatch = (pos[:, None, :] + plane_off[None, :, None]).reshape(-1)
    idx_return = (pos[None, :, :] + plane_off[:, None, None]).reshape(-1)
    xs = _sc_scatter_rows(hp.reshape(N_PLANES * n_tok, PLANE_W), idx_dispatch, N_PLANES * n_pad)
    ys = _grouped_ffn(xs.reshape(N_PLANES, n_pad, PLANE_W), tile_expert, n_valid, w1, w3, w2, tm_e, fc)
    yg = _sc_gather_rows(ys.reshape(N_PLANES * n_pad, PLANE_W), idx_return)
    return _combine(tok, yg.reshape(N_PLANES, 2, b, t, PLANE_W), gm, g_post, ga, tm)


_ROPE_SWAP = np.concatenate([np.arange(16, 32), np.arange(0, 16), np.arange(48, 64), np.arange(32, 48)])


def _rope_tables(n_tok):
    rows = n_tok // GRID_W
    row = jnp.broadcast_to(jnp.arange(rows, dtype=F32)[:, None], (rows, GRID_W)).reshape(-1)
    col = jnp.broadcast_to(jnp.arange(GRID_W, dtype=F32)[None, :], (rows, GRID_W)).reshape(-1)
    half = MLA_ROPE // 2
    inv = 1.0 / (ROPE_BASE ** (jnp.arange(0, half, 2, dtype=F32) / half))
    cr, sr = jnp.cos(row[:, None] * inv), jnp.sin(row[:, None] * inv)
    cc, sc = jnp.cos(col[:, None] * inv), jnp.sin(col[:, None] * inv)
    cos64 = jnp.concatenate([cr, cr, cc, cc], axis=-1)
    sin64 = jnp.concatenate([-sr, sr, -sc, sc], axis=-1)
    return (jnp.tile(cos64, (1, MLA_HEADS)), jnp.tile(sin64, (1, MLA_HEADS)), cos64.T, sin64.T)


def _identity_rope_tables(n_tok):
    one = jnp.ones((n_tok, MLA_ROPE), F32)
    zero = jnp.zeros((n_tok, MLA_ROPE), F32)
    return (jnp.tile(one, (1, MLA_HEADS)), jnp.tile(zero, (1, MLA_HEADS)), one.T, zero.T)


def _even_weights(w_in, w_gate_f, b_gate_f, w_gate_b, b_gate_b, q_norm, w_uq, kv_norm, w_ukv):
    nqk = GLA_HEADS * GLA_DK
    nv = GLA_HEADS * GLA_DV
    o_z = 2 * nqk + nv
    o_r = o_z + 2 * GLA_RANK
    o_cq = o_r + nv
    o_kv = o_cq + MLA_Q_RANK
    o_kr = o_kv + MLA_KV_RANK
    d = w_in.shape[0]
    wp = jnp.concatenate([w_in[:, :o_z], w_in[:, o_r:o_kr], w_in[:, o_z:o_r],
                          jnp.zeros((d, LANES - 2 * GLA_RANK), F32)], axis=1).astype(BF16)
    kr = w_in[:, o_kr:o_kr + MLA_ROPE]
    wkr = jnp.concatenate([kr, kr[:, _ROPE_SWAP]], axis=1).T.astype(BF16)
    wg = jnp.zeros((LANES, 2 * nqk), F32)
    wg = wg.at[0:GLA_RANK, 0:nqk].set(w_gate_f).at[GLA_RANK:2 * GLA_RANK, nqk:].set(w_gate_b)
    wg_hi = wg.astype(BF16)
    wg_lo = (wg - wg_hi.astype(F32)).astype(BF16)
    bg = jnp.concatenate([b_gate_f, b_gate_b])[None, :]
    hq = np.arange(MLA_HEADS)[:, None] * MLA_QK
    nope_idx = (hq + np.arange(MLA_NOPE)[None, :]).reshape(-1)
    rope_idx = (hq + MLA_NOPE + np.arange(MLA_ROPE)[None, :]).reshape(-1)
    swap_idx = (hq + MLA_NOPE + _ROPE_SWAP[None, :]).reshape(-1)
    wuq = w_uq[:, np.concatenate([nope_idx, rope_idx, swap_idx])].astype(BF16)
    hk = np.arange(MLA_HEADS)[:, None] * (MLA_NOPE + MLA_V)
    k_idx = (hk + np.arange(MLA_NOPE)[None, :]).reshape(-1)
    v_idx = (hk + MLA_NOPE + np.arange(MLA_V)[None, :]).reshape(-1)
    return dict(wp=wp, wkr=wkr, wg_hi=wg_hi, wg_lo=wg_lo, bg=bg, qn=q_norm[None, :], wuq=wuq,
                kvn=kv_norm[None, :], wuk=w_ukv[:, k_idx].T.astype(BF16), wuv=w_ukv[:, v_idx].astype(BF16))


def _mods(m, rows, batch):
    d = D_MODEL
    if rows is None:
        return [jnp.broadcast_to(m[batch, k * d:(k + 1) * d][None, None, :], (batch, 1, d)) for k in range(6)]
    return [m[:batch, k * d:(k + 1) * d][:, None, :] for k in range(6)]


def kernel(x, c, ctx, c_ctx, w_mod, b_mod, g_mix_pre, g_mix_post, g_ffn_pre, g_ffn_post, e_w_in, e_w_gate_f, e_b_gate_f, e_w_gate_b, e_b_gate_b, e_gla_norm, e_q_norm, e_w_uq, e_kv_norm, e_w_ukv, e_w_o, e_w1, e_w3, e_w2, o_w_o, o_w_router, o_w1, o_w3, o_w2):
    batch, seq, d = x.shape
    n_ctx = ctx.shape[1]
    cond = jnp.zeros((16, d), F32).at[:batch].set(c).at[batch].set(c_ctx)
    mods = _adaln(cond, w_mod, b_mod)
    rope_x = _rope_tables(seq)
    rope_c = _identity_rope_tables(n_ctx)
    nqk = GLA_HEADS * GLA_DK
    last_read = 2 * ((DEPTH - 1) // 2)
    xs, xc = x, ctx
    for i in range(DEPTH):
        j = i // 2
        ctx_live = i <= last_read
        ctx_full = i < last_read
        mx = _mods(mods[i], 0, batch)
        mc = _mods(mods[i], None, batch)
        gpre, gpost = g_mix_pre[i][None, :], g_mix_post[i][None, :]
        fpre, fpost = g_ffn_pre[i][None, :], g_ffn_post[i][None, :]
        if i % 2 == 0:
            w = _even_weights(e_w_in[j], e_w_gate_f[j], e_b_gate_f[j], e_w_gate_b[j], e_b_gate_b[j],
                              e_q_norm[j], e_w_uq[j], e_kv_norm[j], e_w_ukv[j])
            wo1 = e_w_o[j][:GLA_HEADS * GLA_DV].astype(BF16)
            wo2 = e_w_o[j][GLA_HEADS * GLA_DV:].astype(BF16)
            gn = e_gla_norm[j][None, :]
            w1, w3, w2 = e_w1[j].astype(BF16), e_w3[j].astype(BF16), e_w2[j].astype(BF16)
            zero_state = jnp.zeros((batch, GLA_DV, nqk), F32)
            if ctx_live:
                cgq, cgk, cgv, cgr, claf, clab, cq, ckt, cv = _in_proj(xc, gpre, mc[0], mc[1], w, rope_c, n_ctx)
                co_f, co_b, s_f, s_b = _gla(cgq, cgk, cgv, claf, clab, zero_state, zero_state, n_ctx)
                srcs_c = [(ckt, cv)]
            else:
                s_f = s_b = zero_state
                srcs_c = []
            gq, gk, gv, gr, laf, lab, q, kt, v = _in_proj(xs, gpre, mx[0], mx[1], w, rope_x, min(512, seq))
            o_f, o_b, _, _ = _gla(gq, gk, gv, laf, lab, s_f, s_b, min(256, seq))
            a = _attention(q, [(kt, v)] + srcs_c, min(1024, seq))
            xs = _even_tail(xs, o_f, o_b, gr, a, gn, wo1, wo2, gpost, mx[2], fpre, mx[3], mx[4],
                            w1, w3, w2, fpost, mx[5], min(512, seq))
            if ctx_full:
                ac = _attention(cq, srcs_c, n_ctx)
                xc = _even_tail(xc, co_f, co_b, cgr, ac, gn, wo1, wo2, gpost, mc[2], fpre, mc[3], mc[4],
                                w1, w3, w2, fpost, mc[5], n_ctx)
        else:
            wo = o_w_o[j].astype(BF16)
            wr = jnp.zeros((d, LANES), F32).at[:, :N_EXPERTS].set(o_w_router[j])
            wr_hi = wr.astype(BF16)
            wr_lo = (wr - wr_hi.astype(F32)).astype(BF16)
            w1, w3, w2 = o_w1[j].astype(BF16), o_w3[j].astype(BF16), o_w2[j].astype(BF16)
            xs = _fourier_x(xs, gpre, mx[0], mx[1], wo, gpost, mx[2])
            xs = _moe_sorted(xs, fpre, mx[3], mx[4], wr_hi, wr_lo, w1, w3, w2, fpost, mx[5],
                             min(512, seq), min(1024, seq), 1792)
            if ctx_full:
                xc = _fourier_ctx(xc, gpre, mc[0], mc[1], wo, gpost, mc[2])
                xc = _moe_sorted(xc, fpre, mc[3], mc[4], wr_hi, wr_lo, w1, w3, w2, fpost, mc[5],
                                 n_ctx, n_ctx, 1792)
    return xs
```

```python
import functools

import numpy as np
import jax
import jax.numpy as jnp
from jax import lax
from jax.experimental import pallas as pl
from jax.experimental.pallas import tpu as pltpu
from jax.experimental.pallas import tpu_sc as plsc

F32 = jnp.float32
BF16 = jnp.bfloat16

EPS = 1e-6
D_MODEL = 1024
DEPTH = 4
GRID_W = 64
GLA_HEADS = 4
GLA_DK = 64
GLA_DV = 128
GLA_RANK = 16
GLA_TAU = 16.0
GLA_CHUNK = 64
MLA_HEADS = 4
MLA_Q_RANK = 256
MLA_KV_RANK = 128
MLA_NOPE = 128
MLA_ROPE = 64
MLA_V = 128
MLA_QK = MLA_NOPE + MLA_ROPE
ROPE_BASE = 10000.0
FOURIER_GROUPS = 4
FOURIER_GW = D_MODEL // FOURIER_GROUPS
N_EXPERTS = 8
LANES = 128
VMEM_LIMIT = 48 * 1024 * 1024


def _cparams(sem):
    return pltpu.CompilerParams(dimension_semantics=sem, vmem_limit_bytes=VMEM_LIMIT)


def _dot(a, b):
    return jnp.dot(a, b, preferred_element_type=F32)


def _dot_nt(a, b):
    return lax.dot_general(a, b, (((1,), (1,)), ((), ())), preferred_element_type=F32)


def _dot_tn(a, b):
    return lax.dot_general(a, b, (((0,), (0,)), ((), ())), preferred_element_type=F32)


def _split(x):
    hi = x.astype(BF16)
    lo = (x - hi.astype(F32)).astype(BF16)
    return hi, lo


def _dot3(a, b_hi, b_lo):
    a_hi, a_lo = _split(a)
    return _dot(a_hi, b_hi) + _dot(a_lo, b_hi) + _dot(a_hi, b_lo)


def _rms(x, g):
    return x * lax.rsqrt(jnp.mean(x * x, axis=-1, keepdims=True) + EPS) * g


def _silu(x):
    return x / (1.0 + jnp.exp(-x))


def _const_spec(shape):
    nd = len(shape)
    return pl.BlockSpec(shape, lambda *_: (0,) * nd)


def _batch_row_spec(d):
    return pl.BlockSpec((1, 1, d), lambda b, *_: (b, 0, 0))


def _adaln_kernel(c_ref, w_ref, b_ref, o_ref):
    a = _silu(c_ref[...])
    w_hi, w_lo = _split(w_ref[0])
    o_ref[0] = _dot3(a, w_hi, w_lo) + b_ref[0]


def _adaln(cond, w_mod, b_mod):
    depth, d, n = w_mod.shape
    rows = cond.shape[0]
    bn = 1536
    return pl.pallas_call(
        _adaln_kernel,
        grid=(depth, n // bn),
        in_specs=[
            pl.BlockSpec((rows, d), lambda i, j: (0, 0)),
            pl.BlockSpec((1, d, bn), lambda i, j: (i, 0, j)),
            pl.BlockSpec((1, 1, bn), lambda i, j: (i, 0, j)),
        ],
        out_specs=pl.BlockSpec((1, rows, bn), lambda i, j: (i, 0, j)),
        out_shape=jax.ShapeDtypeStruct((depth, rows, n), F32),
        compiler_params=_cparams(("arbitrary", "arbitrary")),
        name="adaln",
    )(cond, w_mod, b_mod.reshape(depth, 1, n))


def _in_proj_kernel(x_ref, g_ref, sh_ref, sc_ref, wp_ref, wkr_ref, wgh_ref, wgl_ref, bg_ref,
                    qn_ref, wuq_ref, kvn_ref, wuk_ref, wuv_ref, cq_ref, sq_ref, ck_ref, sk_ref,
                    gq_ref, gk_ref, gv_ref, gr_ref, laf_ref, lab_ref, q_ref, kt_ref, v_ref):
    h = _rms(x_ref[0], g_ref[...]) * (1.0 + sc_ref[0]) + sh_ref[0]
    hb = h.astype(BF16)
    p = _dot(hb, wp_ref[...])
    nqk = GLA_HEADS * GLA_DK
    nv = GLA_HEADS * GLA_DV
    gq_ref[0] = p[:, 0:nqk] * (GLA_DK ** -0.5)
    gk_ref[0] = p[:, nqk:2 * nqk]
    gv_ref[0] = p[:, 2 * nqk:2 * nqk + nv]
    gr_ref[0] = p[:, 2 * nqk + nv:2 * nqk + 2 * nv]
    o = 2 * nqk + 2 * nv
    cq = p[:, o:o + MLA_Q_RANK]
    ckv = p[:, o + MLA_Q_RANK:o + MLA_Q_RANK + MLA_KV_RANK]
    tail = p[:, o + MLA_Q_RANK + MLA_KV_RANK:]
    pre = _dot3(tail, wgh_ref[...], wgl_ref[...]) + bg_ref[...]
    la = (jnp.minimum(pre, 0.0) - jnp.log(1.0 + jnp.exp(-jnp.abs(pre)))) * (1.0 / GLA_TAU)
    laf_ref[0] = la[:, :nqk]
    lab_ref[0] = la[:, nqk:]
    q = _dot(_rms(cq, qn_ref[...]).astype(BF16), wuq_ref[...])
    att_scale = MLA_QK ** -0.5
    nn = MLA_HEADS * MLA_NOPE
    nr = MLA_HEADS * MLA_ROPE
    q_rope = q[:, nn:nn + nr] * cq_ref[...] + q[:, nn + nr:] * sq_ref[...]
    for hd in range(MLA_HEADS):
        q_ref[0, hd, :, 0:MLA_NOPE] = (q[:, hd * MLA_NOPE:(hd + 1) * MLA_NOPE] * att_scale).astype(BF16)
        q_ref[0, hd, :, MLA_NOPE:MLA_QK] = (q_rope[:, hd * MLA_ROPE:(hd + 1) * MLA_ROPE] * att_scale).astype(BF16)
    ckvn = _rms(ckv, kvn_ref[...]).astype(BF16)
    kt = _dot_nt(wuk_ref[...], ckvn)
    v = _dot(ckvn, wuv_ref[...])
    kr2 = _dot_nt(wkr_ref[...], hb)
    kr = (kr2[:MLA_ROPE] * ck_ref[...] + kr2[MLA_ROPE:] * sk_ref[...]).astype(BF16)
    for hd in range(MLA_HEADS):
        kt_ref[0, hd, 0:MLA_NOPE, :] = kt[hd * MLA_NOPE:(hd + 1) * MLA_NOPE].astype(BF16)
        kt_ref[0, hd, MLA_NOPE:MLA_QK, :] = kr
        v_ref[0, hd] = v[:, hd * MLA_V:(hd + 1) * MLA_V].astype(BF16)


def _in_proj(tok, g, sh, sc, w, tabs, tm):
    b, t, d = tok.shape
    cq, sq, ck, sk = tabs
    nqk = GLA_HEADS * GLA_DK
    nv = GLA_HEADS * GLA_DV
    row = lambda n: pl.BlockSpec((1, tm, n), lambda bi, ti: (bi, ti, 0))
    weights = (w["wp"], w["wkr"], w["wg_hi"], w["wg_lo"], w["bg"], w["qn"], w["wuq"], w["kvn"],
               w["wuk"], w["wuv"])
    return pl.pallas_call(
        _in_proj_kernel,
        grid=(b, t // tm),
        in_specs=[row(d), _const_spec((1, d)), _batch_row_spec(d), _batch_row_spec(d)]
        + [_const_spec(a.shape) for a in weights]
        + [pl.BlockSpec((tm, MLA_HEADS * MLA_ROPE), lambda bi, ti: (ti, 0)),
           pl.BlockSpec((tm, MLA_HEADS * MLA_ROPE), lambda bi, ti: (ti, 0)),
           pl.BlockSpec((MLA_ROPE, tm), lambda bi, ti: (0, ti)),
           pl.BlockSpec((MLA_ROPE, tm), lambda bi, ti: (0, ti))],
        out_specs=[row(nqk), row(nqk), row(nv), row(nv), row(nqk), row(nqk),
                   pl.BlockSpec((1, MLA_HEADS, tm, MLA_QK), lambda bi, ti: (bi, 0, ti, 0)),
                   pl.BlockSpec((1, MLA_HEADS, MLA_QK, tm), lambda bi, ti: (bi, 0, 0, ti)),
                   pl.BlockSpec((1, MLA_HEADS, tm, MLA_V), lambda bi, ti: (bi, 0, ti, 0))],
        out_shape=[jax.ShapeDtypeStruct((b, t, nqk), F32), jax.ShapeDtypeStruct((b, t, nqk), F32),
                   jax.ShapeDtypeStruct((b, t, nv), F32), jax.ShapeDtypeStruct((b, t, nv), F32),
                   jax.ShapeDtypeStruct((b, t, nqk), F32), jax.ShapeDtypeStruct((b, t, nqk), F32),
                   jax.ShapeDtypeStruct((b, MLA_HEADS, t, MLA_QK), BF16),
                   jax.ShapeDtypeStruct((b, MLA_HEADS, MLA_QK, t), BF16),
                   jax.ShapeDtypeStruct((b, MLA_HEADS, t, MLA_V), BF16)],
        compiler_params=_cparams(("parallel", "parallel")),
        name="even_in_proj",
    )(tok, g, sh, sc, *weights, cq, sq, ck, sk)


def _gla_chunks(streams):
    c = GLA_CHUNK
    heads = range(GLA_HEADS)
    ks = [slice(hd * GLA_DK, (hd + 1) * GLA_DK) for hd in heads]
    vs = [slice(hd * GLA_DV, (hd + 1) * GLA_DV) for hd in heads]
    bcs = []
    for (_, _, _, l_ref, _, _, g, row0, tri, _, _) in streams:
        la_hi, la_lo = _split(l_ref[g, pl.ds(row0, c), :])
        bcs.append(_dot(tri, la_hi) + _dot(tri, la_lo))
    ops = []
    for (q_ref, k_ref, v_ref, _, _, st_ref, g, row0, _, _, last_row), bc in zip(streams, bcs):
        bl = bc[last_row:last_row + 1, :]
        q = q_ref[g, pl.ds(row0, c), :]
        k = k_ref[g, pl.ds(row0, c), :]
        st = st_ref[g]
        ops.append(dict(qc=(q * jnp.exp(bc)).astype(BF16), kc=(k * jnp.exp(-bc)).astype(BF16),
                        kd=(k * jnp.exp(bl - bc)).astype(BF16), vb=v_ref[g, pl.ds(row0, c), :].astype(BF16),
                        st=st, stb=st.astype(BF16), decay=jnp.exp(bl)))
    atts = [[_dot_nt(o["qc"][:, ks[hd]], o["kc"][:, ks[hd]]) for hd in heads] for o in ops]
    inters = [[_dot_nt(o["qc"][:, ks[hd]], o["stb"][:, ks[hd]]) for hd in heads] for o in ops]
    upds = [[_dot_tn(o["vb"][:, vs[hd]], o["kd"][:, ks[hd]]) for hd in heads] for o in ops]
    intras = [[_dot(jnp.where(s[9], att[hd], 0.0).astype(BF16), o["vb"][:, vs[hd]]) for hd in heads]
              for s, o, att in zip(streams, ops, atts)]
    for s, o, intra, inter, upd in zip(streams, ops, intras, inters, upds):
        o_ref, st_ref, g, row0 = s[4], s[5], s[6], s[7]
        o_ref[g, pl.ds(row0, c), :] = jnp.concatenate([a + b for a, b in zip(intra, inter)], axis=1)
        st_ref[g] = o["st"] * o["decay"] + jnp.concatenate(upd, axis=1)


def _gla_kernel(qf, kf, vf, lf, qb, kb, vb, lb, s0f, s0b, of, ob, sff, sfb, stf, stb, *, nc, gb):
    j = pl.program_id(1)

    @pl.when(j == 0)
    def _():
        stf[...] = s0f[...]
        stb[...] = s0b[...]

    c = GLA_CHUNK
    r = lax.broadcasted_iota(jnp.int32, (c, c), 0)
    cc = lax.broadcasted_iota(jnp.int32, (c, c), 1)
    lower = r >= cc
    upper = r <= cc
    tri_l = jnp.where(lower, 1.0, 0.0).astype(BF16)
    tri_u = jnp.where(upper, 1.0, 0.0).astype(BF16)

    for ci in range(nc):
        streams = []
        for g in range(gb):
            streams.append((qf, kf, vf, lf, of, stf, g, ci * c, tri_l, lower, c - 1))
            streams.append((qb, kb, vb, lb, ob, stb, g, (nc - 1 - ci) * c, tri_u, upper, 0))
        _gla_chunks(streams)

    @pl.when(j == pl.num_programs(1) - 1)
    def _():
        sff[...] = stf[...]
        sfb[...] = stb[...]


GLA_BATCH_ROWS = 4


def _gla(gq, gk, gv, laf, lab, s0f, s0b, tb):
    b, t, nqk = gq.shape
    nv = gv.shape[-1]
    nblk = t // tb
    gb = min(GLA_BATCH_ROWS, b)
    fwd = lambda n: pl.BlockSpec((gb, tb, n), lambda bi, j: (bi, j, 0))
    bwd = lambda n: pl.BlockSpec((gb, tb, n), lambda bi, j: (bi, nblk - 1 - j, 0))
    st = pl.BlockSpec((gb, GLA_DV, nqk), lambda bi, j: (bi, 0, 0))
    return pl.pallas_call(
        functools.partial(_gla_kernel, nc=tb // GLA_CHUNK, gb=gb),
        grid=(b // gb, nblk),
        in_specs=[fwd(nqk), fwd(nqk), fwd(nv), fwd(nqk), bwd(nqk), bwd(nqk), bwd(nv), bwd(nqk), st, st],
        out_specs=[fwd(nv), bwd(nv), st, st],
        out_shape=[jax.ShapeDtypeStruct((b, t, nv), F32), jax.ShapeDtypeStruct((b, t, nv), F32),
                   jax.ShapeDtypeStruct((b, GLA_DV, nqk), F32), jax.ShapeDtypeStruct((b, GLA_DV, nqk), F32)],
        scratch_shapes=[pltpu.VMEM((gb, GLA_DV, nqk), F32), pltpu.VMEM((gb, GLA_DV, nqk), F32)],
        compiler_params=_cparams(("parallel", "arbitrary")),
        name="gla_scan",
    )(gq, gk, gv, laf, gq, gk, gv, lab, s0f, s0b)


ATTN_KEY_CHUNK = 1024


def _attn_kernel(*refs, n_src):
    q_ref = refs[0]
    kts = refs[1:1 + 2 * n_src:2]
    vs = refs[2:2 + 2 * n_src:2]
    o_ref = refs[1 + 2 * n_src]
    vexts = refs[2 + 2 * n_src:]

    @pl.when(pl.program_id(2) == 0)
    def _():
        for v, vext in zip(vs, vexts):
            tk = v.shape[2]
            lane = lax.broadcasted_iota(jnp.int32, (tk, LANES), 1)
            vext[:, 0:MLA_V] = v[0, 0]
            vext[:, MLA_V:MLA_V + LANES] = jnp.where(lane == 0, 1.0, 0.0).astype(BF16)

    q = q_ref[0, 0]
    bq = q.shape[0]
    m = jnp.full((bq, 1), -jnp.inf, F32)
    acc = jnp.zeros((bq, MLA_V + LANES), F32)
    chunks = []
    for kt, vext in zip(kts, vexts):
        tk = kt.shape[3]
        ck = min(ATTN_KEY_CHUNK, tk)
        chunks += [(kt, vext, c * ck, (c + 1) * ck) for c in range(tk // ck)]
    s_next = _dot(q, chunks[0][0][0, 0, :, chunks[0][2]:chunks[0][3]])
    for i, (kt, vext, lo, hi) in enumerate(chunks):
        s = s_next
        if i + 1 < len(chunks):
            kt_n, _, lo_n, hi_n = chunks[i + 1]
            s_next = _dot(q, kt_n[0, 0, :, lo_n:hi_n])
        m_new = jnp.maximum(m, s.max(axis=-1, keepdims=True))
        p = jnp.exp((s - m_new).astype(BF16))
        acc = acc * jnp.exp(m - m_new) + _dot(p, vext[lo:hi, :])
        m = m_new
    o_ref[0] = (acc[:, 0:MLA_V] / acc[:, MLA_V:MLA_V + 1]).astype(o_ref.dtype)


def _attention(q, srcs, bq):
    b, nh, t, dqk = q.shape
    in_specs = [pl.BlockSpec((1, 1, bq, dqk), lambda bi, hi, qi: (bi, hi, qi, 0))]
    args = [q]
    scratch = []
    for kt, v in srcs:
        tk = kt.shape[-1]
        in_specs.append(pl.BlockSpec((1, 1, dqk, tk), lambda bi, hi, qi: (bi, hi, 0, 0)))
        in_specs.append(pl.BlockSpec((1, 1, tk, MLA_V), lambda bi, hi, qi: (bi, hi, 0, 0)))
        args += [kt, v]
        scratch.append(pltpu.VMEM((tk, MLA_V + LANES), BF16))
    return pl.pallas_call(
        functools.partial(_attn_kernel, n_src=len(srcs)),
        grid=(b, nh, t // bq),
        in_specs=in_specs,
        out_specs=pl.BlockSpec((1, bq, MLA_V), lambda bi, hi, qi: (bi, qi, hi)),
        out_shape=jax.ShapeDtypeStruct((b, t, nh * MLA_V), BF16),
        scratch_shapes=scratch,
        compiler_params=_cparams(("parallel", "parallel", "arbitrary")),
        name="mla_attention",
    )(*args)


MXU_COLS = 256


def _ff_chunks(dff, max_tiles):
    n_tiles = dff // MXU_COLS
    n_chunks = -(-n_tiles // max_tiles)
    base, extra = divmod(n_tiles, n_chunks)
    bounds, start = [], 0
    for i in range(n_chunks):
        width = (base + (1 if i < extra else 0)) * MXU_COLS
        bounds.append((start, start + width))
        start += width
    return bounds


def _even_tail_kernel(x_ref, of_ref, ob_ref, gr_ref, a_ref, gn_ref, wo1_ref, wo2_ref, gpost_ref, ga_ref,
                      gpre_ref, sh_ref, sc_ref, w1_ref, w3_ref, w2_ref, fpost_ref, gaf_ref, xo_ref, *, chunks):
    o = of_ref[0] + ob_ref[0]
    parts = [_rms(o[:, hd * GLA_DV:(hd + 1) * GLA_DV], gn_ref[...]) for hd in range(GLA_HEADS)]
    fin = jnp.concatenate(parts, axis=1) * _silu(gr_ref[0])
    y = _dot(fin.astype(BF16), wo1_ref[...]) + _dot(a_ref[0], wo2_ref[...])
    xn = x_ref[0] + ga_ref[0] * _rms(y, gpost_ref[...])
    h = (_rms(xn, gpre_ref[...]) * (1.0 + sc_ref[0]) + sh_ref[0]).astype(BF16)
    z = None
    for lo, hi in chunks:
        a = _dot(h, w1_ref[:, lo:hi])
        mid = (_silu(a) * _dot(h, w3_ref[:, lo:hi])).astype(BF16)
        part = _dot(mid, w2_ref[lo:hi, :])
        z = part if z is None else z + part
    xo_ref[0] = xn + gaf_ref[0] * _rms(z, fpost_ref[...])


def _even_tail(tok, o_f, o_b, gr, a, gn, wo1, wo2, gpost, ga, gpre, sh, sc, w1, w3, w2, fpost, gaf, tm):
    b, t, d = tok.shape
    nv = o_f.shape[-1]
    dff = w1.shape[-1]
    row = lambda n: pl.BlockSpec((1, tm, n), lambda bi, ti: (bi, ti, 0))
    resident = lambda shape: pl.BlockSpec(shape, lambda bi, ti: (0, 0), pipeline_mode=pl.Buffered(1))
    return pl.pallas_call(
        functools.partial(_even_tail_kernel, chunks=_ff_chunks(dff, 6)),
        grid=(b, t // tm),
        in_specs=[row(d), row(nv), row(nv), row(nv), row(a.shape[-1]), _const_spec(gn.shape),
                  resident(wo1.shape), resident(wo2.shape), _const_spec((1, d)), _batch_row_spec(d),
                  _const_spec((1, d)), _batch_row_spec(d), _batch_row_spec(d),
                  resident((d, dff)), resident((d, dff)), resident((dff, d)),
                  _const_spec((1, d)), _batch_row_spec(d)],
        out_specs=row(d),
        out_shape=jax.ShapeDtypeStruct((b, t, d), F32),
        compiler_params=_cparams(("parallel", "parallel")),
        name="even_tail",
    )(tok, o_f, o_b, gr, a, gn, wo1, wo2, gpost, ga, gpre, sh, sc, w1, w3, w2, fpost, gaf)


def _dft_cos_sin(n):
    idx = (np.arange(n)[:, None] * np.arange(n)[None, :]) % n
    ang = 2.0 * np.pi * idx.astype(np.float64) / n
    return np.cos(ang), np.sin(ang)


def _fourier1_kernel(x_ref, g_ref, sh_ref, sc_ref, f1_ref, tc_ref, ts_ref, o_ref, *, nb, n1):
    d = x_ref.shape[-1]
    xt = jnp.swapaxes(x_ref[0], 0, 1)
    for j in range(nb):
        hb = (_rms(xt[j], g_ref[...]) * (1.0 + sc_ref[0]) + sh_ref[0]).astype(BF16)
        a = _dot(f1_ref[...], hb)
        ar, ai = a[:n1], a[n1:]
        tc = jnp.concatenate([tc_ref[j * n1:(j + 1) * n1, :]] * (d // LANES), axis=1)
        ts = jnp.concatenate([ts_ref[j * n1:(j + 1) * n1, :]] * (d // LANES), axis=1)
        o_ref[0, 0, j] = ar * tc - ai * ts
        o_ref[0, 1, j] = ar * ts + ai * tc


def _fourier2_kernel(b_ref, x_ref, f2_ref, cc_ref, sc_ref, wo_ref, gpost_ref, ga_ref, xo_ref, *, kb, n2):
    d = b_ref.shape[-1]
    br = jnp.swapaxes(b_ref[0, 0], 0, 1)
    bi = jnp.swapaxes(b_ref[0, 1], 0, 1)
    urs, uis = [], []
    for j in range(kb):
        u = _dot(f2_ref[...], jnp.concatenate([br[j], bi[j]], axis=0).astype(BF16))
        urs.append(u[:n2])
        uis.append(u[n2:])
    ur = jnp.concatenate(urs, axis=0).astype(BF16)
    ui = jnp.concatenate(uis, axis=0).astype(BF16)
    gw = FOURIER_GW
    f = jnp.concatenate(
        [_dot(ur[:, g * gw:(g + 1) * gw], cc_ref[...]) + _dot(ui[:, g * gw:(g + 1) * gw], sc_ref[...])
         for g in range(FOURIER_GROUPS)], axis=1)
    y = _dot(f.astype(BF16), wo_ref[...])
    yn = ga_ref[0] * _rms(y, gpost_ref[...])
    xo_ref[0] = x_ref[0] + jnp.swapaxes(yn.reshape(kb, n2, d), 0, 1)


def _fourier_x(tok, g, sh, sc, wo, gpost, ga):
    b, t, d = tok.shape
    n1, n2 = 128, t // 128
    nb, kb = min(8, n2), 8
    c1, s1 = _dft_cos_sin(n1)
    c2, s2 = _dft_cos_sin(n2)
    cg, sg = _dft_cos_sin(FOURIER_GW)
    f1 = jnp.asarray(np.concatenate([c1, -s1], axis=0) / np.sqrt(n1), BF16)
    f2 = jnp.asarray(np.block([[c2, s2], [-s2, c2]]) / np.sqrt(n2), BF16)
    ccg = jnp.asarray(cg / np.sqrt(FOURIER_GW), BF16)
    scg = jnp.asarray(sg / np.sqrt(FOURIER_GW), BF16)
    tw = 2.0 * np.pi * ((np.arange(n2)[:, None] * np.arange(n1)[None, :]) % t).astype(np.float64) / t
    tc = jnp.asarray(np.broadcast_to(np.cos(tw).reshape(-1, 1), (n2 * n1, LANES)), F32)
    ts = jnp.asarray(np.broadcast_to(-np.sin(tw).reshape(-1, 1), (n2 * n1, LANES)), F32)
    stage1 = pl.pallas_call(
        functools.partial(_fourier1_kernel, nb=nb, n1=n1),
        grid=(b, n2 // nb),
        in_specs=[pl.BlockSpec((1, n1, nb, d), lambda bi, ji: (bi, 0, ji, 0)),
                  _const_spec((1, d)), _batch_row_spec(d), _batch_row_spec(d),
                  _const_spec(f1.shape),
                  pl.BlockSpec((nb * n1, LANES), lambda bi, ji: (ji, 0)),
                  pl.BlockSpec((nb * n1, LANES), lambda bi, ji: (ji, 0))],
        out_specs=pl.BlockSpec((1, 2, nb, n1, d), lambda bi, ji: (bi, 0, ji, 0, 0)),
        out_shape=jax.ShapeDtypeStruct((b, 2, n2, n1, d), F32),
        compiler_params=_cparams(("parallel", "parallel")),
        name="fourier_stage1",
    )(tok.reshape(b, n1, n2, d), g, sh, sc, f1, tc, ts)
    freq = lambda n: pl.BlockSpec((1, n2, kb, n), lambda bi, ki: (bi, 0, ki, 0))
    out = pl.pallas_call(
        functools.partial(_fourier2_kernel, kb=kb, n2=n2),
        grid=(b, n1 // kb),
        in_specs=[pl.BlockSpec((1, 2, n2, kb, d), lambda bi, ki: (bi, 0, 0, ki, 0)),
                  freq(d),
                  _const_spec(f2.shape), _const_spec(ccg.shape), _const_spec(scg.shape),
                  _const_spec(wo.shape), _const_spec((1, d)), _batch_row_spec(d)],
        out_specs=freq(d),
        out_shape=jax.ShapeDtypeStruct((b, n2, n1, d), F32),
        compiler_params=_cparams(("parallel", "parallel")),
        name="fourier_stage2",
    )(stage1, tok.reshape(b, n2, n1, d), f2, ccg, scg, wo, gpost, ga)
    return out.reshape(b, t, d)


def _fourier_ctx_kernel(x_ref, g_ref, sh_ref, sc_ref, fl_ref, cc_ref, scg_ref, wo_ref, gpost_ref, ga_ref,
                        xo_ref):
    x = x_ref[0]
    t = x.shape[0]
    h = _rms(x, g_ref[...]) * (1.0 + sc_ref[0]) + sh_ref[0]
    u = _dot(fl_ref[...], h.astype(BF16))
    ur = u[:t].astype(BF16)
    ui = u[t:].astype(BF16)
    gw = FOURIER_GW
    f = jnp.concatenate(
        [_dot(ur[:, g * gw:(g + 1) * gw], cc_ref[...]) + _dot(ui[:, g * gw:(g + 1) * gw], scg_ref[...])
         for g in range(FOURIER_GROUPS)], axis=1)
    y = _dot(f.astype(BF16), wo_ref[...])
    xo_ref[0] = x + ga_ref[0] * _rms(y, gpost_ref[...])


def _fourier_ctx(tok, g, sh, sc, wo, gpost, ga):
    b, t, d = tok.shape
    cl, sl = _dft_cos_sin(t)
    cg, sg = _dft_cos_sin(FOURIER_GW)
    fl = jnp.asarray(np.concatenate([cl, -sl], axis=0) / np.sqrt(t), BF16)
    ccg = jnp.asarray(cg / np.sqrt(FOURIER_GW), BF16)
    scg = jnp.asarray(sg / np.sqrt(FOURIER_GW), BF16)
    row = pl.BlockSpec((1, t, d), lambda bi: (bi, 0, 0))
    return pl.pallas_call(
        _fourier_ctx_kernel,
        grid=(b,),
        in_specs=[row, _const_spec((1, d)), _batch_row_spec(d), _batch_row_spec(d), _const_spec(fl.shape),
                  _const_spec(ccg.shape), _const_spec(scg.shape), _const_spec(wo.shape),
                  _const_spec((1, d)), _batch_row_spec(d)],
        out_specs=row,
        out_shape=jax.ShapeDtypeStruct((b, t, d), F32),
        compiler_params=_cparams(("parallel",)),
        name="fourier_ctx",
    )(tok, g, sh, sc, fl, ccg, scg, wo, gpost, ga)


PLANE_W = 256
N_PLANES = D_MODEL // (2 * PLANE_W)
SC_WINDOW = 128


def _pack_planes(h):
    out = []
    for p in range(N_PLANES):
        base = 2 * p * PLANE_W
        hi = pltpu.bitcast(h[:, base:base + PLANE_W].astype(BF16).astype(F32), jnp.uint32)
        lo = pltpu.bitcast(h[:, base + PLANE_W:base + 2 * PLANE_W].astype(BF16).astype(F32), jnp.uint32)
        out.append(hi | (lo >> 16))
    return out


def _unpack_planes(planes):
    cols = []
    for w in planes:
        cols.append(pltpu.bitcast(w & jnp.uint32(0xFFFF0000), F32))
        cols.append(pltpu.bitcast(w << 16, F32))
    return jnp.concatenate(cols, axis=1)


def _route_rows(h, wrh_ref, wrl_ref, carry_ref):
    logits = _dot3(h, wrh_ref[...], wrl_ref[...])
    tm = logits.shape[0]
    lane = lax.broadcasted_iota(jnp.int32, logits.shape, 1)
    neg = -jnp.inf
    l1 = jnp.where(lane < N_EXPERTS, logits, neg)
    m1 = l1.max(axis=-1, keepdims=True)
    i1 = jnp.where(l1 == m1, lane, LANES).min(axis=-1, keepdims=True)
    l2 = jnp.where(lane == i1, neg, l1)
    m2 = l2.max(axis=-1, keepdims=True)
    i2 = jnp.where(l2 == m2, lane, LANES).min(axis=-1, keepdims=True)
    e = jnp.exp(m2 - m1)
    g1 = 1.0 / (1.0 + e)
    g2 = e / (1.0 + e)
    sel = jnp.logical_or(lane == i1, lane == i2)
    cnt = jnp.where(sel, 1.0, 0.0)
    r = lax.broadcasted_iota(jnp.int32, (tm, tm), 0)
    c = lax.broadcasted_iota(jnp.int32, (tm, tm), 1)
    below = jnp.where(r > c, 1.0, 0.0).astype(BF16)
    before = _dot(below, cnt.astype(BF16)) + carry_ref[...]
    r1 = jnp.where(lane == i1, before, 0.0).sum(axis=-1, keepdims=True).astype(jnp.int32)
    r2 = jnp.where(lane == i2, before, 0.0).sum(axis=-1, keepdims=True).astype(jnp.int32)
    carry_ref[...] = carry_ref[...] + cnt.sum(axis=0, keepdims=True)
    im = jnp.where(lane == 0, i1, jnp.where(lane == 1, i2, jnp.where(lane == 2, r1, r2)))
    return im, jnp.where(lane == 0, g1, g2)


def _route_kernel(x_ref, g_ref, sh_ref, sc_ref, wrh_ref, wrl_ref, hp_ref, im_ref, gm_ref, cnt_ref, carry_ref):
    first = jnp.logical_and(pl.program_id(0) == 0, pl.program_id(1) == 0)

    @pl.when(first)
    def _():
        carry_ref[...] = jnp.zeros_like(carry_ref)

    h = _rms(x_ref[0], g_ref[...]) * (1.0 + sc_ref[0]) + sh_ref[0]
    for p, w in enumerate(_pack_planes(h)):
        hp_ref[p, 0] = w
    im_ref[0], gm_ref[0] = _route_rows(h, wrh_ref, wrl_ref, carry_ref)
    cnt_ref[...] = carry_ref[...]


def _route(tok, g, sh, sc, wr_hi, wr_lo, tm):
    b, t, d = tok.shape
    row = lambda n: pl.BlockSpec((1, tm, n), lambda bi, ti: (bi, ti, 0))
    return pl.pallas_call(
        _route_kernel,
        grid=(b, t // tm),
        in_specs=[row(d), _const_spec((1, d)), _batch_row_spec(d), _batch_row_spec(d),
                  _const_spec(wr_hi.shape), _const_spec(wr_lo.shape)],
        out_specs=[pl.BlockSpec((N_PLANES, 1, tm, PLANE_W), lambda bi, ti: (0, bi, ti, 0)),
                   row(LANES), row(LANES), _const_spec((1, LANES))],
        out_shape=[jax.ShapeDtypeStruct((N_PLANES, b, t, PLANE_W), jnp.uint32),
                   jax.ShapeDtypeStruct((b, t, LANES), jnp.int32),
                   jax.ShapeDtypeStruct((b, t, LANES), F32),
                   jax.ShapeDtypeStruct((1, LANES), F32)],
        scratch_shapes=[pltpu.VMEM((1, LANES), F32)],
        compiler_params=_cparams(("arbitrary", "arbitrary")),
        name="moe_route",
    )(tok, g, sh, sc, wr_hi, wr_lo)


def _sc_mesh():
    return plsc.VectorSubcoreMesh(core_axis_name="core", subcore_axis_name="subcore")


def _sc_gather_rows(table, idx):
    n = idx.shape[0]
    w = table.shape[1]

    @pl.kernel(out_type=jax.ShapeDtypeStruct((n, w), table.dtype), mesh=_sc_mesh())
    def gather(t_hbm, i_hbm, o_hbm):
        def body(i_vmem, o_vmem):
            pltpu.sync_copy(t_hbm.at[i_vmem.at[0]], o_vmem)

        pltpu.emit_pipeline(
            body, grid=(n // SC_WINDOW,),
            in_specs=[pl.BlockSpec((1, SC_WINDOW), index_map=lambda i: (0, i))],
            out_specs=[pl.BlockSpec((SC_WINDOW, w), index_map=lambda i: (i, 0))],
            core_axis_name=("core", "subcore"), dimension_semantics=(pltpu.PARALLEL,),
        )(i_hbm, o_hbm)

    return gather(table, idx.reshape(1, n))


def _sc_scatter_rows(src, idx, n_out):
    n = idx.shape[0]
    w = src.shape[1]
    n_src_windows = src.shape[0] // SC_WINDOW

    @pl.kernel(out_type=jax.ShapeDtypeStruct((n_out, w), src.dtype), mesh=_sc_mesh(), scratch_types=[])
    def scatter(s_hbm, i_hbm, o_hbm):
        def body(s_vmem, i_vmem):
            pltpu.sync_copy(s_vmem, o_hbm.at[i_vmem.at[0]])

        pltpu.emit_pipeline(
            body, grid=(n // SC_WINDOW,),
            in_specs=[pl.BlockSpec((SC_WINDOW, w), index_map=lambda i: (i % n_src_windows, 0)),
                      pl.BlockSpec((1, SC_WINDOW), index_map=lambda i: (0, i))],
            out_specs=[],
            core_axis_name=("core", "subcore"), dimension_semantics=(pltpu.PARALLEL,),
        )(s_hbm, i_hbm)

    return scatter(src, idx.reshape(1, n))


def _grouped_ffn_kernel(te_ref, nv_ref, xs_ref, w1_ref, w3_ref, w2_ref, y_ref, hb_ref, acc_ref):
    i = pl.program_id(0)
    f = pl.program_id(1)

    @pl.when(i < nv_ref[0])
    def _():
        @pl.when(f == 0)
        def _():
            hb_ref[...] = _unpack_planes([xs_ref[p] for p in range(N_PLANES)]).astype(BF16)

        half = hb_ref.shape[0] // 2
        for r in range(2):
            rows = slice(r * half, (r + 1) * half)
            h = hb_ref[rows, :]
            a = _dot(h, w1_ref[0])
            mid = (_silu(a) * _dot(h, w3_ref[0])).astype(BF16)
            contrib = _dot(mid, w2_ref[0])

            @pl.when(f == 0)
            def _():
                acc_ref[rows, :] = contrib

            @pl.when(f > 0)
            def _():
                acc_ref[rows, :] += contrib

        @pl.when(f == pl.num_programs(1) - 1)
        def _():
            for p, w in enumerate(_pack_planes(acc_ref[...])):
                y_ref[p] = w


def _grouped_ffn(xs, tile_expert, n_valid, w1, w3, w2, tm, fc):
    n_pad = xs.shape[1]
    ne, d, dff = w1.shape
    plane = pl.BlockSpec((N_PLANES, tm, PLANE_W), lambda i, f, te, nv: (0, i, 0))
    return pl.pallas_call(
        _grouped_ffn_kernel,
        grid_spec=pltpu.PrefetchScalarGridSpec(
            num_scalar_prefetch=2,
            grid=(n_pad // tm, dff // fc),
            in_specs=[plane,
                      pl.BlockSpec((1, d, fc), lambda i, f, te, nv: (te[i], 0, f)),
                      pl.BlockSpec((1, d, fc), lambda i, f, te, nv: (te[i], 0, f)),
                      pl.BlockSpec((1, fc, d), lambda i, f, te, nv: (te[i], f, 0))],
            out_specs=plane,
            scratch_shapes=[pltpu.VMEM((tm, d), BF16), pltpu.VMEM((tm, d), F32)]),
        out_shape=jax.ShapeDtypeStruct(xs.shape, jnp.uint32),
        compiler_params=_cparams(("arbitrary", "arbitrary")),
        name="moe_grouped_ffn",
    )(tile_expert, n_valid, xs, w1, w3, w2)


def _combine_kernel(x_ref, y_ref, gm_ref, g_ref, ga_ref, xo_ref):
    gm = gm_ref[0]
    y1 = _unpack_planes([y_ref[p, 0, 0] for p in range(N_PLANES)])
    y2 = _unpack_planes([y_ref[p, 1, 0] for p in range(N_PLANES)])
    mix = gm[:, 0:1] * y1 + gm[:, 1:2] * y2
    xo_ref[0] = x_ref[0] + ga_ref[0] * _rms(mix, g_ref[...])


def _combine(tok, yg, gm, g, ga, tm):
    b, t, d = tok.shape
    row = lambda n: pl.BlockSpec((1, tm, n), lambda bi, ti: (bi, ti, 0))
    return pl.pallas_call(
        _combine_kernel,
        grid=(b, t // tm),
        in_specs=[row(d),
                  pl.BlockSpec((N_PLANES, 2, 1, tm, PLANE_W), lambda bi, ti: (0, 0, bi, ti, 0)),
                  row(LANES), _const_spec((1, d)), _batch_row_spec(d)],
        out_specs=row(d),
        out_shape=jax.ShapeDtypeStruct((b, t, d), F32),
        compiler_params=_cparams(("parallel", "parallel")),
        name="moe_combine",
    )(tok, yg, gm, g, ga)


def _moe_sorted(tok, g_pre, sh, sc, wr_hi, wr_lo, w1, w3, w2, g_post, ga, tm, tm_e, fc):
    routed = _route(tok, g_pre, sh, sc, wr_hi, wr_lo, tm)
    return _moe_experts(tok, routed, w1, w3, w2, g_post, ga, tm, tm_e, fc)


def _moe_experts(tok, routed, w1, w3, w2, g_post, ga, tm, tm_e, fc):
    b, t, d = tok.shape
    n_tok = b * t
    hp, im, gm, cnt = routed
    counts = cnt[0, :N_EXPERTS].astype(jnp.int32)
    padded = ((counts + tm_e - 1) // tm_e) * tm_e
    ends = jnp.cumsum(padded)
    starts = ends - padded
    n_pad = 2 * n_tok + N_EXPERTS * tm_e
    n_tiles = n_pad // tm_e
    tile_expert = jnp.minimum(
        jnp.sum((jnp.arange(n_tiles, dtype=jnp.int32)[:, None] * tm_e >= ends[None, :]).astype(jnp.int32), axis=1),
        N_EXPERTS - 1).astype(jnp.int32)
    n_valid = (ends[-1:] // tm_e).astype(jnp.int32)
    im2 = im.reshape(n_tok, LANES)
    pos = jnp.stack([starts[im2[:, 0]] + im2[:, 2], starts[im2[:, 1]] + im2[:, 3]], axis=0)
    plane_off = (jnp.arange(N_PLANES, dtype=jnp.int32) * n_pad)
    idx_dispatch = (pos[:, None, :] + plane_off[None, :, None]).reshape(-1)
    idx_return = (pos[None, :, :] + plane_off[:, None, None]).reshape(-1)
    xs = _sc_scatter_rows(hp.reshape(N_PLANES * n_tok, PLANE_W), idx_dispatch, N_PLANES * n_pad)
    ys = _grouped_ffn(xs.reshape(N_PLANES, n_pad, PLANE_W), tile_expert, n_valid, w1, w3, w2, tm_e, fc)
    yg = _sc_gather_rows(ys.reshape(N_PLANES * n_pad, PLANE_W), idx_return)
    return _combine(tok, yg.reshape(N_PLANES, 2, b, t, PLANE_W), gm, g_post, ga, tm)


_ROPE_SWAP = np.concatenate([np.arange(16, 32), np.arange(0, 16), np.arange(48, 64), np.arange(32, 48)])


def _rope_tables(n_tok):
    rows = n_tok // GRID_W
    row = jnp.broadcast_to(jnp.arange(rows, dtype=F32)[:, None], (rows, GRID_W)).reshape(-1)
    col = jnp.broadcast_to(jnp.arange(GRID_W, dtype=F32)[None, :], (rows, GRID_W)).reshape(-1)
    half = MLA_ROPE // 2
    inv = 1.0 / (ROPE_BASE ** (jnp.arange(0, half, 2, dtype=F32) / half))
    cr, sr = jnp.cos(row[:, None] * inv), jnp.sin(row[:, None] * inv)
    cc, sc = jnp.cos(col[:, None] * inv), jnp.sin(col[:, None] * inv)
    cos64 = jnp.concatenate([cr, cr, cc, cc], axis=-1)
    sin64 = jnp.concatenate([-sr, sr, -sc, sc], axis=-1)
    return (jnp.tile(cos64, (1, MLA_HEADS)), jnp.tile(sin64, (1, MLA_HEADS)), cos64.T, sin64.T)


def _identity_rope_tables(n_tok):
    one = jnp.ones((n_tok, MLA_ROPE), F32)
    zero = jnp.zeros((n_tok, MLA_ROPE), F32)
    return (jnp.tile(one, (1, MLA_HEADS)), jnp.tile(zero, (1, MLA_HEADS)), one.T, zero.T)


def _even_weights(w_in, w_gate_f, b_gate_f, w_gate_b, b_gate_b, q_norm, w_uq, kv_norm, w_ukv):
    nqk = GLA_HEADS * GLA_DK
    nv = GLA_HEADS * GLA_DV
    o_z = 2 * nqk + nv
    o_r = o_z + 2 * GLA_RANK
    o_cq = o_r + nv
    o_kv = o_cq + MLA_Q_RANK
    o_kr = o_kv + MLA_KV_RANK
    d = w_in.shape[0]
    wp = jnp.concatenate([w_in[:, :o_z], w_in[:, o_r:o_kr], w_in[:, o_z:o_r],
                          jnp.zeros((d, LANES - 2 * GLA_RANK), F32)], axis=1).astype(BF16)
    kr = w_in[:, o_kr:o_kr + MLA_ROPE]
    wkr = jnp.concatenate([kr, kr[:, _ROPE_SWAP]], axis=1).T.astype(BF16)
    wg = jnp.zeros((LANES, 2 * nqk), F32)
    wg = wg.at[0:GLA_RANK, 0:nqk].set(w_gate_f).at[GLA_RANK:2 * GLA_RANK, nqk:].set(w_gate_b)
    wg_hi = wg.astype(BF16)
    wg_lo = (wg - wg_hi.astype(F32)).astype(BF16)
    bg = jnp.concatenate([b_gate_f, b_gate_b])[None, :]
    hq = np.arange(MLA_HEADS)[:, None] * MLA_QK
    nope_idx = (hq + np.arange(MLA_NOPE)[None, :]).reshape(-1)
    rope_idx = (hq + MLA_NOPE + np.arange(MLA_ROPE)[None, :]).reshape(-1)
    swap_idx = (hq + MLA_NOPE + _ROPE_SWAP[None, :]).reshape(-1)
    wuq = w_uq[:, np.concatenate([nope_idx, rope_idx, swap_idx])].astype(BF16)
    hk = np.arange(MLA_HEADS)[:, None] * (MLA_NOPE + MLA_V)
    k_idx = (hk + np.arange(MLA_NOPE)[None, :]).reshape(-1)
    v_idx = (hk + MLA_NOPE + np.arange(MLA_V)[None, :]).reshape(-1)
    return dict(wp=wp, wkr=wkr, wg_hi=wg_hi, wg_lo=wg_lo, bg=bg, qn=q_norm[None, :], wuq=wuq,
                kvn=kv_norm[None, :], wuk=w_ukv[:, k_idx].T.astype(BF16), wuv=w_ukv[:, v_idx].astype(BF16))


def _mods(m, rows, batch):
    d = D_MODEL
    if rows is None:
        return [jnp.broadcast_to(m[batch, k * d:(k + 1) * d][None, None, :], (batch, 1, d)) for k in range(6)]
    return [m[:batch, k * d:(k + 1) * d][:, None, :] for k in range(6)]


def kernel(x, c, ctx, c_ctx, w_mod, b_mod, g_mix_pre, g_mix_post, g_ffn_pre, g_ffn_post, e_w_in, e_w_gate_f, e_b_gate_f, e_w_gate_b, e_b_gate_b, e_gla_norm, e_q_norm, e_w_uq, e_kv_norm, e_w_ukv, e_w_o, e_w1, e_w3, e_w2, o_w_o, o_w_router, o_w1, o_w3, o_w2):
    batch, seq, d = x.shape
    n_ctx = ctx.shape[1]
    cond = jnp.zeros((16, d), F32).at[:batch].set(c).at[batch].set(c_ctx)
    mods = _adaln(cond, w_mod, b_mod)
    rope_x = _rope_tables(seq)
    rope_c = _identity_rope_tables(n_ctx)
    nqk = GLA_HEADS * GLA_DK
    last_read = 2 * ((DEPTH - 1) // 2)
    xs, xc = x, ctx
    for i in range(DEPTH):
        j = i // 2
        ctx_live = i <= last_read
        ctx_full = i < last_read
        mx = _mods(mods[i], 0, batch)
        mc = _mods(mods[i], None, batch)
        gpre, gpost = g_mix_pre[i][None, :], g_mix_post[i][None, :]
        fpre, fpost = g_ffn_pre[i][None, :], g_ffn_post[i][None, :]
        if i % 2 == 0:
            w = _even_weights(e_w_in[j], e_w_gate_f[j], e_b_gate_f[j], e_w_gate_b[j], e_b_gate_b[j],
                              e_q_norm[j], e_w_uq[j], e_kv_norm[j], e_w_ukv[j])
            wo1 = e_w_o[j][:GLA_HEADS * GLA_DV].astype(BF16)
            wo2 = e_w_o[j][GLA_HEADS * GLA_DV:].astype(BF16)
            gn = e_gla_norm[j][None, :]
            w1, w3, w2 = e_w1[j].astype(BF16), e_w3[j].astype(BF16), e_w2[j].astype(BF16)
            zero_state = jnp.zeros((batch, GLA_DV, nqk), F32)
            if ctx_live:
                cgq, cgk, cgv, cgr, claf, clab, cq, ckt, cv = _in_proj(xc, gpre, mc[0], mc[1], w, rope_c, n_ctx)
                co_f, co_b, s_f, s_b = _gla(cgq, cgk, cgv, claf, clab, zero_state, zero_state, n_ctx)
                srcs_c = [(ckt, cv)]
            else:
                s_f = s_b = zero_state
                srcs_c = []
            gq, gk, gv, gr, laf, lab, q, kt, v = _in_proj(xs, gpre, mx[0], mx[1], w, rope_x, min(512, seq))
            o_f, o_b, _, _ = _gla(gq, gk, gv, laf, lab, s_f, s_b, min(256, seq))
            a = _attention(q, [(kt, v)] + srcs_c, min(1024, seq))
            xs = _even_tail(xs, o_f, o_b, gr, a, gn, wo1, wo2, gpost, mx[2], fpre, mx[3], mx[4],
                            w1, w3, w2, fpost, mx[5], min(512, seq))
            if ctx_full:
                ac = _attention(cq, srcs_c, n_ctx)
                xc = _even_tail(xc, co_f, co_b, cgr, ac, gn, wo1, wo2, gpost, mc[2], fpre, mc[3], mc[4],
                                w1, w3, w2, fpost, mc[5], n_ctx)
        else:
            wo = o_w_o[j].astype(BF16)
            wr = jnp.zeros((d, LANES), F32).at[:, :N_EXPERTS].set(o_w_router[j])
            wr_hi = wr.astype(BF16)
            wr_lo = (wr - wr_hi.astype(F32)).astype(BF16)
            w1, w3, w2 = o_w1[j].astype(BF16), o_w3[j].astype(BF16), o_w2[j].astype(BF16)
            xs = _fourier_x(xs, gpre, mx[0], mx[1], wo, gpost, mx[2])
            xs = _moe_sorted(xs, fpre, mx[3], mx[4], wr_hi, wr_lo, w1, w3, w2, fpost, mx[5],
                             min(512, seq), min(1024, seq), 1792)
            if ctx_full:
                xc = _fourier_ctx(xc, gpre, mc[0], mc[1], wo, gpost, mc[2])
                xc = _moe_sorted(xc, fpre, mc[3], mc[4], wr_hi, wr_lo, w1, w3, w2, fpost, mc[5],
                                 n_ctx, n_ctx, 1792)
    return xs
```

```python
import functools

import numpy as np
import jax
import jax.numpy as jnp
from jax import lax
from jax.experimental import pallas as pl
from jax.experimental.pallas import tpu as pltpu
from jax.experimental.pallas import tpu_sc as plsc

F32 = jnp.float32
BF16 = jnp.bfloat16

EPS = 1e-6
D_MODEL = 1024
DEPTH = 4
GRID_W = 64
GLA_HEADS = 4
GLA_DK = 64
GLA_DV = 128
GLA_RANK = 16
GLA_TAU = 16.0
GLA_CHUNK = 64
MLA_HEADS = 4
MLA_Q_RANK = 256
MLA_KV_RANK = 128
MLA_NOPE = 128
MLA_ROPE = 64
MLA_V = 128
MLA_QK = MLA_NOPE + MLA_ROPE
ROPE_BASE = 10000.0
FOURIER_GROUPS = 4
FOURIER_GW = D_MODEL // FOURIER_GROUPS
N_EXPERTS = 8
LANES = 128
VMEM_LIMIT = 48 * 1024 * 1024


def _cparams(sem):
    return pltpu.CompilerParams(dimension_semantics=sem, vmem_limit_bytes=VMEM_LIMIT)


def _dot(a, b):
    return jnp.dot(a, b, preferred_element_type=F32)


def _dot_nt(a, b):
    return lax.dot_general(a, b, (((1,), (1,)), ((), ())), preferred_element_type=F32)


def _dot_tn(a, b):
    return lax.dot_general(a, b, (((0,), (0,)), ((), ())), preferred_element_type=F32)


def _split(x):
    hi = x.astype(BF16)
    lo = (x - hi.astype(F32)).astype(BF16)
    return hi, lo


def _dot3(a, b_hi, b_lo):
    a_hi, a_lo = _split(a)
    return _dot(a_hi, b_hi) + _dot(a_lo, b_hi) + _dot(a_hi, b_lo)


def _rms(x, g):
    return x * lax.rsqrt(jnp.mean(x * x, axis=-1, keepdims=True) + EPS) * g


def _silu(x):
    return x / (1.0 + jnp.exp(-x))


def _const_spec(shape):
    nd = len(shape)
    return pl.BlockSpec(shape, lambda *_: (0,) * nd)


def _batch_row_spec(d):
    return pl.BlockSpec((1, 1, d), lambda b, *_: (b, 0, 0))


def _adaln_kernel(c_ref, w_ref, b_ref, o_ref):
    a = _silu(c_ref[...])
    w_hi, w_lo = _split(w_ref[0])
    o_ref[0] = _dot3(a, w_hi, w_lo) + b_ref[0]


def _adaln(cond, w_mod, b_mod):
    depth, d, n = w_mod.shape
    rows = cond.shape[0]
    bn = 1536
    return pl.pallas_call(
        _adaln_kernel,
        grid=(depth, n // bn),
        in_specs=[
            pl.BlockSpec((rows, d), lambda i, j: (0, 0)),
            pl.BlockSpec((1, d, bn), lambda i, j: (i, 0, j)),
            pl.BlockSpec((1, 1, bn), lambda i, j: (i, 0, j)),
        ],
        out_specs=pl.BlockSpec((1, rows, bn), lambda i, j: (i, 0, j)),
        out_shape=jax.ShapeDtypeStruct((depth, rows, n), F32),
        compiler_params=_cparams(("arbitrary", "arbitrary")),
        name="adaln",
    )(cond, w_mod, b_mod.reshape(depth, 1, n))


def _in_proj_kernel(x_ref, g_ref, sh_ref, sc_ref, wp_ref, wkr_ref, wgh_ref, wgl_ref, bg_ref,
                    qn_ref, wuq_ref, kvn_ref, wuk_ref, wuv_ref, cq_ref, sq_ref, ck_ref, sk_ref,
                    gq_ref, gk_ref, gv_ref, gr_ref, laf_ref, lab_ref, q_ref, kt_ref, v_ref):
    h = _rms(x_ref[0], g_ref[...]) * (1.0 + sc_ref[0]) + sh_ref[0]
    hb = h.astype(BF16)
    p = _dot(hb, wp_ref[...])
    nqk = GLA_HEADS * GLA_DK
    nv = GLA_HEADS * GLA_DV
    gq_ref[0] = p[:, 0:nqk] * (GLA_DK ** -0.5)
    gk_ref[0] = p[:, nqk:2 * nqk]
    gv_ref[0] = p[:, 2 * nqk:2 * nqk + nv]
    gr_ref[0] = p[:, 2 * nqk + nv:2 * nqk + 2 * nv]
    o = 2 * nqk + 2 * nv
    cq = p[:, o:o + MLA_Q_RANK]
    ckv = p[:, o + MLA_Q_RANK:o + MLA_Q_RANK + MLA_KV_RANK]
    tail = p[:, o + MLA_Q_RANK + MLA_KV_RANK:]
    pre = _dot3(tail, wgh_ref[...], wgl_ref[...]) + bg_ref[...]
    la = (jnp.minimum(pre, 0.0) - jnp.log(1.0 + jnp.exp(-jnp.abs(pre)))) * (1.0 / GLA_TAU)
    laf_ref[0] = la[:, :nqk]
    lab_ref[0] = la[:, nqk:]
    q = _dot(_rms(cq, qn_ref[...]).astype(BF16), wuq_ref[...])
    att_scale = MLA_QK ** -0.5
    nn = MLA_HEADS * MLA_NOPE
    nr = MLA_HEADS * MLA_ROPE
    q_rope = q[:, nn:nn + nr] * cq_ref[...] + q[:, nn + nr:] * sq_ref[...]
    for hd in range(MLA_HEADS):
        q_ref[0, hd, :, 0:MLA_NOPE] = (q[:, hd * MLA_NOPE:(hd + 1) * MLA_NOPE] * att_scale).astype(BF16)
        q_ref[0, hd, :, MLA_NOPE:MLA_QK] = (q_rope[:, hd * MLA_ROPE:(hd + 1) * MLA_ROPE] * att_scale).astype(BF16)
    ckvn = _rms(ckv, kvn_ref[...]).astype(BF16)
    kt = _dot_nt(wuk_ref[...], ckvn)
    v = _dot(ckvn, wuv_ref[...])
    kr2 = _dot_nt(wkr_ref[...], hb)
    kr = (kr2[:MLA_ROPE] * ck_ref[...] + kr2[MLA_ROPE:] * sk_ref[...]).astype(BF16)
    for hd in range(MLA_HEADS):
        kt_ref[0, hd, 0:MLA_NOPE, :] = kt[hd * MLA_NOPE:(hd + 1) * MLA_NOPE].astype(BF16)
        kt_ref[0, hd, MLA_NOPE:MLA_QK, :] = kr
        v_ref[0, hd] = v[:, hd * MLA_V:(hd + 1) * MLA_V].astype(BF16)


def _in_proj(tok, g, sh, sc, w, tabs, tm):
    b, t, d = tok.shape
    cq, sq, ck, sk = tabs
    nqk = GLA_HEADS * GLA_DK
    nv = GLA_HEADS * GLA_DV
    row = lambda n: pl.BlockSpec((1, tm, n), lambda bi, ti: (bi, ti, 0))
    weights = (w["wp"], w["wkr"], w["wg_hi"], w["wg_lo"], w["bg"], w["qn"], w["wuq"], w["kvn"],
               w["wuk"], w["wuv"])
    return pl.pallas_call(
        _in_proj_kernel,
        grid=(b, t // tm),
        in_specs=[row(d), _const_spec((1, d)), _batch_row_spec(d), _batch_row_spec(d)]
        + [_const_spec(a.shape) for a in weights]
        + [pl.BlockSpec((tm, MLA_HEADS * MLA_ROPE), lambda bi, ti: (ti, 0)),
           pl.BlockSpec((tm, MLA_HEADS * MLA_ROPE), lambda bi, ti: (ti, 0)),
           pl.BlockSpec((MLA_ROPE, tm), lambda bi, ti: (0, ti)),
           pl.BlockSpec((MLA_ROPE, tm), lambda bi, ti: (0, ti))],
        out_specs=[row(nqk), row(nqk), row(nv), row(nv), row(nqk), row(nqk),
                   pl.BlockSpec((1, MLA_HEADS, tm, MLA_QK), lambda bi, ti: (bi, 0, ti, 0)),
                   pl.BlockSpec((1, MLA_HEADS, MLA_QK, tm), lambda bi, ti: (bi, 0, 0, ti)),
                   pl.BlockSpec((1, MLA_HEADS, tm, MLA_V), lambda bi, ti: (bi, 0, ti, 0))],
        out_shape=[jax.ShapeDtypeStruct((b, t, nqk), F32), jax.ShapeDtypeStruct((b, t, nqk), F32),
                   jax.ShapeDtypeStruct((b, t, nv), F32), jax.ShapeDtypeStruct((b, t, nv), F32),
                   jax.ShapeDtypeStruct((b, t, nqk), F32), jax.ShapeDtypeStruct((b, t, nqk), F32),
                   jax.ShapeDtypeStruct((b, MLA_HEADS, t, MLA_QK), BF16),
                   jax.ShapeDtypeStruct((b, MLA_HEADS, MLA_QK, t), BF16),
                   jax.ShapeDtypeStruct((b, MLA_HEADS, t, MLA_V), BF16)],
        compiler_params=_cparams(("parallel", "parallel")),
        name="even_in_proj",
    )(tok, g, sh, sc, *weights, cq, sq, ck, sk)


def _gla_chunks(streams):
    c = GLA_CHUNK
    heads = range(GLA_HEADS)
    ks = [slice(hd * GLA_DK, (hd + 1) * GLA_DK) for hd in heads]
    vs = [slice(hd * GLA_DV, (hd + 1) * GLA_DV) for hd in heads]
    bcs = []
    for (_, _, _, l_ref, _, _, g, row0, tri, _, _) in streams:
        la_hi, la_lo = _split(l_ref[g, pl.ds(row0, c), :])
        bcs.append(_dot(tri, la_hi) + _dot(tri, la_lo))
    ops = []
    for (q_ref, k_ref, v_ref, _, _, st_ref, g, row0, _, _, last_row), bc in zip(streams, bcs):
        bl = bc[last_row:last_row + 1, :]
        q = q_ref[g, pl.ds(row0, c), :]
        k = k_ref[g, pl.ds(row0, c), :]
        st = st_ref[g]
        ops.append(dict(qc=(q * jnp.exp(bc)).astype(BF16), kc=(k * jnp.exp(-bc)).astype(BF16),
                        kd=(k * jnp.exp(bl - bc)).astype(BF16), vb=v_ref[g, pl.ds(row0, c), :].astype(BF16),
                        st=st, stb=st.astype(BF16), decay=jnp.exp(bl)))
    atts = [[_dot_nt(o["qc"][:, ks[hd]], o["kc"][:, ks[hd]]) for hd in heads] for o in ops]
    inters = [[_dot_nt(o["qc"][:, ks[hd]], o["stb"][:, ks[hd]]) for hd in heads] for o in ops]
    upds = [[_dot_tn(o["vb"][:, vs[hd]], o["kd"][:, ks[hd]]) for hd in heads] for o in ops]
    intras = [[_dot(jnp.where(s[9], att[hd], 0.0).astype(BF16), o["vb"][:, vs[hd]]) for hd in heads]
              for s, o, att in zip(streams, ops, atts)]
    for s, o, intra, inter, upd in zip(streams, ops, intras, inters, upds):
        o_ref, st_ref, g, row0 = s[4], s[5], s[6], s[7]
        o_ref[g, pl.ds(row0, c), :] = jnp.concatenate([a + b for a, b in zip(intra, inter)], axis=1)
        st_ref[g] = o["st"] * o["decay"] + jnp.concatenate(upd, axis=1)


def _gla_kernel(qf, kf, vf, lf, qb, kb, vb, lb, s0f, s0b, of, ob, sff, sfb, stf, stb, *, nc, gb):
    j = pl.program_id(1)

    @pl.when(j == 0)
    def _():
        stf[...] = s0f[...]
        stb[...] = s0b[...]

    c = GLA_CHUNK
    r = lax.broadcasted_iota(jnp.int32, (c, c), 0)
    cc = lax.broadcasted_iota(jnp.int32, (c, c), 1)
    lower = r >= cc
    upper = r <= cc
    tri_l = jnp.where(lower, 1.0, 0.0).astype(BF16)
    tri_u = jnp.where(upper, 1.0, 0.0).astype(BF16)

    for ci in range(nc):
        streams = []
        for g in range(gb):
            streams.append((qf, kf, vf, lf, of, stf, g, ci * c, tri_l, lower, c - 1))
            streams.append((qb, kb, vb, lb, ob, stb, g, (nc - 1 - ci) * c, tri_u, upper, 0))
        _gla_chunks(streams)

    @pl.when(j == pl.num_programs(1) - 1)
    def _():
        sff[...] = stf[...]
        sfb[...] = stb[...]


GLA_BATCH_ROWS = 4


def _gla(gq, gk, gv, laf, lab, s0f, s0b, tb):
    b, t, nqk = gq.shape
    nv = gv.shape[-1]
    nblk = t // tb
    gb = min(GLA_BATCH_ROWS, b)
    fwd = lambda n: pl.BlockSpec((gb, tb, n), lambda bi, j: (bi, j, 0))
    bwd = lambda n: pl.BlockSpec((gb, tb, n), lambda bi, j: (bi, nblk - 1 - j, 0))
    st = pl.BlockSpec((gb, GLA_DV, nqk), lambda bi, j: (bi, 0, 0))
    return pl.pallas_call(
        functools.partial(_gla_kernel, nc=tb // GLA_CHUNK, gb=gb),
        grid=(b // gb, nblk),
        in_specs=[fwd(nqk), fwd(nqk), fwd(nv), fwd(nqk), bwd(nqk), bwd(nqk), bwd(nv), bwd(nqk), st, st],
        out_specs=[fwd(nv), bwd(nv), st, st],
        out_shape=[jax.ShapeDtypeStruct((b, t, nv), F32), jax.ShapeDtypeStruct((b, t, nv), F32),
                   jax.ShapeDtypeStruct((b, GLA_DV, nqk), F32), jax.ShapeDtypeStruct((b, GLA_DV, nqk), F32)],
        scratch_shapes=[pltpu.VMEM((gb, GLA_DV, nqk), F32), pltpu.VMEM((gb, GLA_DV, nqk), F32)],
        compiler_params=_cparams(("parallel", "arbitrary")),
        name="gla_scan",
    )(gq, gk, gv, laf, gq, gk, gv, lab, s0f, s0b)


ATTN_KEY_CHUNK = 1024


def _attn_kernel(*refs, n_src):
    q_ref = refs[0]
    kts = refs[1:1 + 2 * n_src:2]
    vs = refs[2:2 + 2 * n_src:2]
    o_ref = refs[1 + 2 * n_src]
    vexts = refs[2 + 2 * n_src:]

    @pl.when(pl.program_id(2) == 0)
    def _():
        for v, vext in zip(vs, vexts):
            tk = v.shape[2]
            lane = lax.broadcasted_iota(jnp.int32, (tk, LANES), 1)
            vext[:, 0:MLA_V] = v[0, 0]
            vext[:, MLA_V:MLA_V + LANES] = jnp.where(lane == 0, 1.0, 0.0).astype(BF16)

    q = q_ref[0, 0]
    bq = q.shape[0]
    m = jnp.full((bq, 1), -jnp.inf, F32)
    acc = jnp.zeros((bq, MLA_V + LANES), F32)
    chunks = []
    for kt, vext in zip(kts, vexts):
        tk = kt.shape[3]
        ck = min(ATTN_KEY_CHUNK, tk)
        chunks += [(kt, vext, c * ck, (c + 1) * ck) for c in range(tk // ck)]
    s_next = _dot(q, chunks[0][0][0, 0, :, chunks[0][2]:chunks[0][3]])
    for i, (kt, vext, lo, hi) in enumerate(chunks):
        s = s_next
        if i + 1 < len(chunks):
            kt_n, _, lo_n, hi_n = chunks[i + 1]
            s_next = _dot(q, kt_n[0, 0, :, lo_n:hi_n])
        m_new = jnp.maximum(m, s.max(axis=-1, keepdims=True))
        p = jnp.exp((s - m_new).astype(BF16))
        acc = acc * jnp.exp(m - m_new) + _dot(p, vext[lo:hi, :])
        m = m_new
    o_ref[0] = (acc[:, 0:MLA_V] / acc[:, MLA_V:MLA_V + 1]).astype(o_ref.dtype)


def _attention(q, srcs, bq):
    b, nh, t, dqk = q.shape
    in_specs = [pl.BlockSpec((1, 1, bq, dqk), lambda bi, hi, qi: (bi, hi, qi, 0))]
    args = [q]
    scratch = []
    for kt, v in srcs:
        tk = kt.shape[-1]
        in_specs.append(pl.BlockSpec((1, 1, dqk, tk), lambda bi, hi, qi: (bi, hi, 0, 0)))
        in_specs.append(pl.BlockSpec((1, 1, tk, MLA_V), lambda bi, hi, qi: (bi, hi, 0, 0)))
        args += [kt, v]
        scratch.append(pltpu.VMEM((tk, MLA_V + LANES), BF16))
    return pl.pallas_call(
        functools.partial(_attn_kernel, n_src=len(srcs)),
        grid=(b, nh, t // bq),
        in_specs=in_specs,
        out_specs=pl.BlockSpec((1, bq, MLA_V), lambda bi, hi, qi: (bi, qi, hi)),
        out_shape=jax.ShapeDtypeStruct((b, t, nh * MLA_V), BF16),
        scratch_shapes=scratch,
        compiler_params=_cparams(("parallel", "parallel", "arbitrary")),
        name="mla_attention",
    )(*args)


MXU_COLS = 256


def _ff_chunks(dff, max_tiles):
    n_tiles = dff // MXU_COLS
    n_chunks = -(-n_tiles // max_tiles)
    base, extra = divmod(n_tiles, n_chunks)
    bounds, start = [], 0
    for i in range(n_chunks):
        width = (base + (1 if i < extra else 0)) * MXU_COLS
        bounds.append((start, start + width))
        start += width
    return bounds


def _even_tail_kernel(x_ref, of_ref, ob_ref, gr_ref, a_ref, gn_ref, wo1_ref, wo2_ref, gpost_ref, ga_ref,
                      gpre_ref, sh_ref, sc_ref, w1_ref, w3_ref, w2_ref, fpost_ref, gaf_ref, xo_ref, *, chunks):
    o = of_ref[0] + ob_ref[0]
    parts = [_rms(o[:, hd * GLA_DV:(hd + 1) * GLA_DV], gn_ref[...]) for hd in range(GLA_HEADS)]
    fin = jnp.concatenate(parts, axis=1) * _silu(gr_ref[0])
    y = _dot(fin.astype(BF16), wo1_ref[...]) + _dot(a_ref[0], wo2_ref[...])
    xn = x_ref[0] + ga_ref[0] * _rms(y, gpost_ref[...])
    h = (_rms(xn, gpre_ref[...]) * (1.0 + sc_ref[0]) + sh_ref[0]).astype(BF16)
    z = None
    for lo, hi in chunks:
        a = _dot(h, w1_ref[:, lo:hi])
        mid = (_silu(a) * _dot(h, w3_ref[:, lo:hi])).astype(BF16)
        part = _dot(mid, w2_ref[lo:hi, :])
        z = part if z is None else z + part
    xo_ref[0] = xn + gaf_ref[0] * _rms(z, fpost_ref[...])


def _even_tail(tok, o_f, o_b, gr, a, gn, wo1, wo2, gpost, ga, gpre, sh, sc, w1, w3, w2, fpost, gaf, tm):
    b, t, d = tok.shape
    nv = o_f.shape[-1]
    dff = w1.shape[-1]
    row = lambda n: pl.BlockSpec((1, tm, n), lambda bi, ti: (bi, ti, 0))
    resident = lambda shape: pl.BlockSpec(shape, lambda bi, ti: (0, 0), pipeline_mode=pl.Buffered(1))
    return pl.pallas_call(
        functools.partial(_even_tail_kernel, chunks=_ff_chunks(dff, 6)),
        grid=(b, t // tm),
        in_specs=[row(d), row(nv), row(nv), row(nv), row(a.shape[-1]), _const_spec(gn.shape),
                  resident(wo1.shape), resident(wo2.shape), _const_spec((1, d)), _batch_row_spec(d),
                  _const_spec((1, d)), _batch_row_spec(d), _batch_row_spec(d),
                  resident((d, dff)), resident((d, dff)), resident((dff, d)),
                  _const_spec((1, d)), _batch_row_spec(d)],
        out_specs=row(d),
        out_shape=jax.ShapeDtypeStruct((b, t, d), F32),
        compiler_params=_cparams(("parallel", "parallel")),
        name="even_tail",
    )(tok, o_f, o_b, gr, a, gn, wo1, wo2, gpost, ga, gpre, sh, sc, w1, w3, w2, fpost, gaf)


def _dft_cos_sin(n):
    idx = (np.arange(n)[:, None] * np.arange(n)[None, :]) % n
    ang = 2.0 * np.pi * idx.astype(np.float64) / n
    return np.cos(ang), np.sin(ang)


def _fourier1_kernel(x_ref, g_ref, sh_ref, sc_ref, f1_ref, o_ref, *, nb, n1):
    xt = jnp.swapaxes(x_ref[0], 0, 1)
    for j in range(nb):
        hb = (_rms(xt[j], g_ref[...]) * (1.0 + sc_ref[0]) + sh_ref[0]).astype(BF16)
        a = _dot(f1_ref[j], hb)
        o_ref[0, 0, j] = a[:n1].astype(o_ref.dtype)
        o_ref[0, 1, j] = a[n1:].astype(o_ref.dtype)


def _fourier2_kernel(b_ref, x_ref, f2_ref, cc_ref, sc_ref, wo_ref, gpost_ref, ga_ref, xo_ref, *, kb, n2):
    d = b_ref.shape[-1]
    br = jnp.swapaxes(b_ref[0, 0].astype(F32), 0, 1)
    bi = jnp.swapaxes(b_ref[0, 1].astype(F32), 0, 1)
    urs, uis = [], []
    for j in range(kb):
        u = _dot(f2_ref[...], jnp.concatenate([br[j], bi[j]], axis=0).astype(BF16))
        urs.append(u[:n2])
        uis.append(u[n2:])
    ur = jnp.concatenate(urs, axis=0).astype(BF16)
    ui = jnp.concatenate(uis, axis=0).astype(BF16)
    gw = FOURIER_GW
    f = jnp.concatenate(
        [_dot(ur[:, g * gw:(g + 1) * gw], cc_ref[...]) + _dot(ui[:, g * gw:(g + 1) * gw], sc_ref[...])
         for g in range(FOURIER_GROUPS)], axis=1)
    y = _dot(f.astype(BF16), wo_ref[...])
    yn = ga_ref[0] * _rms(y, gpost_ref[...])
    xo_ref[0] = x_ref[0] + jnp.swapaxes(yn.reshape(kb, n2, d), 0, 1)


def _fourier_x(tok, g, sh, sc, wo, gpost, ga):
    b, t, d = tok.shape
    n1, n2 = 128, t // 128
    nb, kb = min(8, n2), 16
    c2, s2 = _dft_cos_sin(n2)
    cg, sg = _dft_cos_sin(FOURIER_GW)
    tok_idx = np.arange(n2)[:, None, None] + n2 * np.arange(n1)[None, None, :]
    ang = 2.0 * np.pi * ((np.arange(n1)[None, :, None] * tok_idx) % t).astype(np.float64) / t
    f1 = jnp.asarray(np.concatenate([np.cos(ang), -np.sin(ang)], axis=1) / np.sqrt(n1), BF16)
    f2 = jnp.asarray(np.block([[c2, s2], [-s2, c2]]) / np.sqrt(n2), BF16)
    ccg = jnp.asarray(cg / np.sqrt(FOURIER_GW), BF16)
    scg = jnp.asarray(sg / np.sqrt(FOURIER_GW), BF16)
    stage1 = pl.pallas_call(
        functools.partial(_fourier1_kernel, nb=nb, n1=n1),
        grid=(b, n2 // nb),
        in_specs=[pl.BlockSpec((1, n1, nb, d), lambda bi, ji: (bi, 0, ji, 0)),
                  _const_spec((1, d)), _batch_row_spec(d), _batch_row_spec(d),
                  pl.BlockSpec((nb, 2 * n1, n1), lambda bi, ji: (ji, 0, 0))],
        out_specs=pl.BlockSpec((1, 2, nb, n1, d), lambda bi, ji: (bi, 0, ji, 0, 0)),
        out_shape=jax.ShapeDtypeStruct((b, 2, n2, n1, d), BF16),
        compiler_params=_cparams(("parallel", "parallel")),
        name="fourier_stage1",
    )(tok.reshape(b, n1, n2, d), g, sh, sc, f1)
    freq = lambda n: pl.BlockSpec((1, n2, kb, n), lambda bi, ki: (bi, 0, ki, 0))
    out = pl.pallas_call(
        functools.partial(_fourier2_kernel, kb=kb, n2=n2),
        grid=(b, n1 // kb),
        in_specs=[pl.BlockSpec((1, 2, n2, kb, d), lambda bi, ki: (bi, 0, 0, ki, 0)),
                  freq(d),
                  _const_spec(f2.shape), _const_spec(ccg.shape), _const_spec(scg.shape),
                  _const_spec(wo.shape), _const_spec((1, d)), _batch_row_spec(d)],
        out_specs=freq(d),
        out_shape=jax.ShapeDtypeStruct((b, n2, n1, d), F32),
        compiler_params=_cparams(("parallel", "parallel")),
        name="fourier_stage2",
    )(stage1, tok.reshape(b, n2, n1, d), f2, ccg, scg, wo, gpost, ga)
    return out.reshape(b, t, d)


def _fourier_ctx_kernel(x_ref, g_ref, sh_ref, sc_ref, fl_ref, cc_ref, scg_ref, wo_ref, gpost_ref, ga_ref,
                        xo_ref):
    x = x_ref[0]
    t = x.shape[0]
    h = _rms(x, g_ref[...]) * (1.0 + sc_ref[0]) + sh_ref[0]
    u = _dot(fl_ref[...], h.astype(BF16))
    ur = u[:t].astype(BF16)
    ui = u[t:].astype(BF16)
    gw = FOURIER_GW
    f = jnp.concatenate(
        [_dot(ur[:, g * gw:(g + 1) * gw], cc_ref[...]) + _dot(ui[:, g * gw:(g + 1) * gw], scg_ref[...])
         for g in range(FOURIER_GROUPS)], axis=1)
    y = _dot(f.astype(BF16), wo_ref[...])
    xo_ref[0] = x + ga_ref[0] * _rms(y, gpost_ref[...])


def _fourier_ctx(tok, g, sh, sc, wo, gpost, ga):
    b, t, d = tok.shape
    cl, sl = _dft_cos_sin(t)
    cg, sg = _dft_cos_sin(FOURIER_GW)
    fl = jnp.asarray(np.concatenate([cl, -sl], axis=0) / np.sqrt(t), BF16)
    ccg = jnp.asarray(cg / np.sqrt(FOURIER_GW), BF16)
    scg = jnp.asarray(sg / np.sqrt(FOURIER_GW), BF16)
    row = pl.BlockSpec((1, t, d), lambda bi: (bi, 0, 0))
    return pl.pallas_call(
        _fourier_ctx_kernel,
        grid=(b,),
        in_specs=[row, _const_spec((1, d)), _batch_row_spec(d), _batch_row_spec(d), _const_spec(fl.shape),
                  _const_spec(ccg.shape), _const_spec(scg.shape), _const_spec(wo.shape),
                  _const_spec((1, d)), _batch_row_spec(d)],
        out_specs=row,
        out_shape=jax.ShapeDtypeStruct((b, t, d), F32),
        compiler_params=_cparams(("parallel",)),
        name="fourier_ctx",
    )(tok, g, sh, sc, fl, ccg, scg, wo, gpost, ga)


PLANE_W = 256
N_PLANES = D_MODEL // (2 * PLANE_W)
SC_WINDOW = 128


def _pack_planes(h):
    out = []
    for p in range(N_PLANES):
        base = 2 * p * PLANE_W
        hi = pltpu.bitcast(h[:, base:base + PLANE_W].astype(BF16).astype(F32), jnp.uint32)
        lo = pltpu.bitcast(h[:, base + PLANE_W:base + 2 * PLANE_W].astype(BF16).astype(F32), jnp.uint32)
        out.append(hi | (lo >> 16))
    return out


def _unpack_planes(planes):
    cols = []
    for w in planes:
        cols.append(pltpu.bitcast(w & jnp.uint32(0xFFFF0000), F32))
        cols.append(pltpu.bitcast(w << 16, F32))
    return jnp.concatenate(cols, axis=1)


def _route_rows(h, wrh_ref, wrl_ref, carry_ref):
    logits = _dot3(h, wrh_ref[...], wrl_ref[...])
    tm = logits.shape[0]
    lane = lax.broadcasted_iota(jnp.int32, logits.shape, 1)
    neg = -jnp.inf
    l1 = jnp.where(lane < N_EXPERTS, logits, neg)
    m1 = l1.max(axis=-1, keepdims=True)
    i1 = jnp.where(l1 == m1, lane, LANES).min(axis=-1, keepdims=True)
    l2 = jnp.where(lane == i1, neg, l1)
    m2 = l2.max(axis=-1, keepdims=True)
    i2 = jnp.where(l2 == m2, lane, LANES).min(axis=-1, keepdims=True)
    e = jnp.exp(m2 - m1)
    g1 = 1.0 / (1.0 + e)
    g2 = e / (1.0 + e)
    sel = jnp.logical_or(lane == i1, lane == i2)
    cnt = jnp.where(sel, 1.0, 0.0)
    r = lax.broadcasted_iota(jnp.int32, (tm, tm), 0)
    c = lax.broadcasted_iota(jnp.int32, (tm, tm), 1)
    below = jnp.where(r > c, 1.0, 0.0).astype(BF16)
    before = _dot(below, cnt.astype(BF16)) + carry_ref[...]
    r1 = jnp.where(lane == i1, before, 0.0).sum(axis=-1, keepdims=True).astype(jnp.int32)
    r2 = jnp.where(lane == i2, before, 0.0).sum(axis=-1, keepdims=True).astype(jnp.int32)
    carry_ref[...] = carry_ref[...] + cnt.sum(axis=0, keepdims=True)
    im = jnp.where(lane == 0, i1, jnp.where(lane == 1, i2, jnp.where(lane == 2, r1, r2)))
    return im, jnp.where(lane == 0, g1, g2)


def _route_kernel(x_ref, g_ref, sh_ref, sc_ref, wrh_ref, wrl_ref, hp_ref, im_ref, gm_ref, cnt_ref, carry_ref):
    first = jnp.logical_and(pl.program_id(0) == 0, pl.program_id(1) == 0)

    @pl.when(first)
    def _():
        carry_ref[...] = jnp.zeros_like(carry_ref)

    h = _rms(x_ref[0], g_ref[...]) * (1.0 + sc_ref[0]) + sh_ref[0]
    for p, w in enumerate(_pack_planes(h)):
        hp_ref[p, 0] = w
    im_ref[0], gm_ref[0] = _route_rows(h, wrh_ref, wrl_ref, carry_ref)
    cnt_ref[...] = carry_ref[...]


def _route(tok, g, sh, sc, wr_hi, wr_lo, tm):
    b, t, d = tok.shape
    row = lambda n: pl.BlockSpec((1, tm, n), lambda bi, ti: (bi, ti, 0))
    return pl.pallas_call(
        _route_kernel,
        grid=(b, t // tm),
        in_specs=[row(d), _const_spec((1, d)), _batch_row_spec(d), _batch_row_spec(d),
                  _const_spec(wr_hi.shape), _const_spec(wr_lo.shape)],
        out_specs=[pl.BlockSpec((N_PLANES, 1, tm, PLANE_W), lambda bi, ti: (0, bi, ti, 0)),
                   row(LANES), row(LANES), _const_spec((1, LANES))],
        out_shape=[jax.ShapeDtypeStruct((N_PLANES, b, t, PLANE_W), jnp.uint32),
                   jax.ShapeDtypeStruct((b, t, LANES), jnp.int32),
                   jax.ShapeDtypeStruct((b, t, LANES), F32),
                   jax.ShapeDtypeStruct((1, LANES), F32)],
        scratch_shapes=[pltpu.VMEM((1, LANES), F32)],
        compiler_params=_cparams(("arbitrary", "arbitrary")),
        name="moe_route",
    )(tok, g, sh, sc, wr_hi, wr_lo)


def _sc_mesh():
    return plsc.VectorSubcoreMesh(core_axis_name="core", subcore_axis_name="subcore")


def _sc_gather_rows(table, idx):
    n = idx.shape[0]
    w = table.shape[1]

    @pl.kernel(out_type=jax.ShapeDtypeStruct((n, w), table.dtype), mesh=_sc_mesh())
    def gather(t_hbm, i_hbm, o_hbm):
        def body(i_vmem, o_vmem):
            pltpu.sync_copy(t_hbm.at[i_vmem.at[0]], o_vmem)

        pltpu.emit_pipeline(
            body, grid=(n // SC_WINDOW,),
            in_specs=[pl.BlockSpec((1, SC_WINDOW), index_map=lambda i: (0, i))],
            out_specs=[pl.BlockSpec((SC_WINDOW, w), index_map=lambda i: (i, 0))],
            core_axis_name=("core", "subcore"), dimension_semantics=(pltpu.PARALLEL,),
        )(i_hbm, o_hbm)

    return gather(table, idx.reshape(1, n))


def _sc_scatter_rows(src, idx, n_out):
    n = idx.shape[0]
    w = src.shape[1]
    n_src_windows = src.shape[0] // SC_WINDOW

    @pl.kernel(out_type=jax.ShapeDtypeStruct((n_out, w), src.dtype), mesh=_sc_mesh(), scratch_types=[])
    def scatter(s_hbm, i_hbm, o_hbm):
        def body(s_vmem, i_vmem):
            pltpu.sync_copy(s_vmem, o_hbm.at[i_vmem.at[0]])

        pltpu.emit_pipeline(
            body, grid=(n // SC_WINDOW,),
            in_specs=[pl.BlockSpec((SC_WINDOW, w), index_map=lambda i: (i % n_src_windows, 0)),
                      pl.BlockSpec((1, SC_WINDOW), index_map=lambda i: (0, i))],
            out_specs=[],
            core_axis_name=("core", "subcore"), dimension_semantics=(pltpu.PARALLEL,),
        )(s_hbm, i_hbm)

    return scatter(src, idx.reshape(1, n))


def _grouped_ffn_kernel(te_ref, nv_ref, xs_ref, w1_ref, w3_ref, w2_ref, y_ref, hb_ref, acc_ref):
    i = pl.program_id(0)
    f = pl.program_id(1)

    @pl.when(i < nv_ref[0])
    def _():
        @pl.when(f == 0)
        def _():
            hb_ref[...] = _unpack_planes([xs_ref[p] for p in range(N_PLANES)]).astype(BF16)

        half = hb_ref.shape[0] // 2
        for r in range(2):
            rows = slice(r * half, (r + 1) * half)
            h = hb_ref[rows, :]
            a = _dot(h, w1_ref[0])
            mid = (_silu(a) * _dot(h, w3_ref[0])).astype(BF16)
            contrib = _dot(mid, w2_ref[0])

            @pl.when(f == 0)
            def _():
                acc_ref[rows, :] = contrib

            @pl.when(f > 0)
            def _():
                acc_ref[rows, :] += contrib

        @pl.when(f == pl.num_programs(1) - 1)
        def _():
            for p, w in enumerate(_pack_planes(acc_ref[...])):
                y_ref[p] = w


def _grouped_ffn(xs, tile_expert, n_valid, w1, w3, w2, tm, fc):
    n_pad = xs.shape[1]
    ne, d, dff = w1.shape
    plane = pl.BlockSpec((N_PLANES, tm, PLANE_W), lambda i, f, te, nv: (0, i, 0))
    return pl.pallas_call(
        _grouped_ffn_kernel,
        grid_spec=pltpu.PrefetchScalarGridSpec(
            num_scalar_prefetch=2,
            grid=(n_pad // tm, dff // fc),
            in_specs=[plane,
                      pl.BlockSpec((1, d, fc), lambda i, f, te, nv: (te[i], 0, f)),
                      pl.BlockSpec((1, d, fc), lambda i, f, te, nv: (te[i], 0, f)),
                      pl.BlockSpec((1, fc, d), lambda i, f, te, nv: (te[i], f, 0))],
            out_specs=plane,
            scratch_shapes=[pltpu.VMEM((tm, d), BF16), pltpu.VMEM((tm, d), F32)]),
        out_shape=jax.ShapeDtypeStruct(xs.shape, jnp.uint32),
        compiler_params=_cparams(("arbitrary", "arbitrary")),
        name="moe_grouped_ffn",
    )(tile_expert, n_valid, xs, w1, w3, w2)


def _combine_kernel(x_ref, y_ref, gm_ref, g_ref, ga_ref, xo_ref):
    gm = gm_ref[0]
    y1 = _unpack_planes([y_ref[p, 0, 0] for p in range(N_PLANES)])
    y2 = _unpack_planes([y_ref[p, 1, 0] for p in range(N_PLANES)])
    mix = gm[:, 0:1] * y1 + gm[:, 1:2] * y2
    xo_ref[0] = x_ref[0] + ga_ref[0] * _rms(mix, g_ref[...])


def _combine(tok, yg, gm, g, ga, tm):
    b, t, d = tok.shape
    row = lambda n: pl.BlockSpec((1, tm, n), lambda bi, ti: (bi, ti, 0))
    return pl.pallas_call(
        _combine_kernel,
        grid=(b, t // tm),
        in_specs=[row(d),
                  pl.BlockSpec((N_PLANES, 2, 1, tm, PLANE_W), lambda bi, ti: (0, 0, bi, ti, 0)),
                  row(LANES), _const_spec((1, d)), _batch_row_spec(d)],
        out_specs=row(d),
        out_shape=jax.ShapeDtypeStruct((b, t, d), F32),
        compiler_params=_cparams(("parallel", "parallel")),
        name="moe_combine",
    )(tok, yg, gm, g, ga)


def _moe_sorted(tok, g_pre, sh, sc, wr_hi, wr_lo, w1, w3, w2, g_post, ga, tm, tm_e, fc):
    routed = _route(tok, g_pre, sh, sc, wr_hi, wr_lo, tm)
    return _moe_experts(tok, routed, w1, w3, w2, g_post, ga, tm, tm_e, fc)


def _moe_experts(tok, routed, w1, w3, w2, g_post, ga, tm, tm_e, fc):
    b, t, d = tok.shape
    n_tok = b * t
    hp, im, gm, cnt = routed
    counts = cnt[0, :N_EXPERTS].astype(jnp.int32)
    padded = ((counts + tm_e - 1) // tm_e) * tm_e
    ends = jnp.cumsum(padded)
    starts = ends - padded
    n_pad = 2 * n_tok + N_EXPERTS * tm_e
    n_tiles = n_pad // tm_e
    tile_expert = jnp.minimum(
        jnp.sum((jnp.arange(n_tiles, dtype=jnp.int32)[:, None] * tm_e >= ends[None, :]).astype(jnp.int32), axis=1),
        N_EXPERTS - 1).astype(jnp.int32)
    n_valid = (ends[-1:] // tm_e).astype(jnp.int32)
    im2 = im.reshape(n_tok, LANES)
    pos = jnp.stack([starts[im2[:, 0]] + im2[:, 2], starts[im2[:, 1]] + im2[:, 3]], axis=0)
    plane_off = (jnp.arange(N_PLANES, dtype=jnp.int32) * n_pad)
    idx_dispatch = (pos[:, None, :] + plane_off[None, :, None]).reshape(-1)
    idx_return = (pos[None, :, :] + plane_off[:, None, None]).reshape(-1)
    xs = _sc_scatter_rows(hp.reshape(N_PLANES * n_tok, PLANE_W), idx_dispatch, N_PLANES * n_pad)
    ys = _grouped_ffn(xs.reshape(N_PLANES, n_pad, PLANE_W), tile_expert, n_valid, w1, w3, w2, tm_e, fc)
    yg = _sc_gather_rows(ys.reshape(N_PLANES * n_pad, PLANE_W), idx_return)
    return _combine(tok, yg.reshape(N_PLANES, 2, b, t, PLANE_W), gm, g_post, ga, tm)


_ROPE_SWAP = np.concatenate([np.arange(16, 32), np.arange(0, 16), np.arange(48, 64), np.arange(32, 48)])


def _rope_tables(n_tok):
    rows = n_tok // GRID_W
    row = jnp.broadcast_to(jnp.arange(rows, dtype=F32)[:, None], (rows, GRID_W)).reshape(-1)
    col = jnp.broadcast_to(jnp.arange(GRID_W, dtype=F32)[None, :], (rows, GRID_W)).reshape(-1)
    half = MLA_ROPE // 2
    inv = 1.0 / (ROPE_BASE ** (jnp.arange(0, half, 2, dtype=F32) / half))
    cr, sr = jnp.cos(row[:, None] * inv), jnp.sin(row[:, None] * inv)
    cc, sc = jnp.cos(col[:, None] * inv), jnp.sin(col[:, None] * inv)
    cos64 = jnp.concatenate([cr, cr, cc, cc], axis=-1)
    sin64 = jnp.concatenate([-sr, sr, -sc, sc], axis=-1)
    return (jnp.tile(cos64, (1, MLA_HEADS)), jnp.tile(sin64, (1, MLA_HEADS)), cos64.T, sin64.T)


def _identity_rope_tables(n_tok):
    one = jnp.ones((n_tok, MLA_ROPE), F32)
    zero = jnp.zeros((n_tok, MLA_ROPE), F32)
    return (jnp.tile(one, (1, MLA_HEADS)), jnp.tile(zero, (1, MLA_HEADS)), one.T, zero.T)


def _even_weights(w_in, w_gate_f, b_gate_f, w_gate_b, b_gate_b, q_norm, w_uq, kv_norm, w_ukv):
    nqk = GLA_HEADS * GLA_DK
    nv = GLA_HEADS * GLA_DV
    o_z = 2 * nqk + nv
    o_r = o_z + 2 * GLA_RANK
    o_cq = o_r + nv
    o_kv = o_cq + MLA_Q_RANK
    o_kr = o_kv + MLA_KV_RANK
    d = w_in.shape[0]
    wp = jnp.concatenate([w_in[:, :o_z], w_in[:, o_r:o_kr], w_in[:, o_z:o_r],
                          jnp.zeros((d, LANES - 2 * GLA_RANK), F32)], axis=1).astype(BF16)
    kr = w_in[:, o_kr:o_kr + MLA_ROPE]
    wkr = jnp.concatenate([kr, kr[:, _ROPE_SWAP]], axis=1).T.astype(BF16)
    wg = jnp.zeros((LANES, 2 * nqk), F32)
    wg = wg.at[0:GLA_RANK, 0:nqk].set(w_gate_f).at[GLA_RANK:2 * GLA_RANK, nqk:].set(w_gate_b)
    wg_hi = wg.astype(BF16)
    wg_lo = (wg - wg_hi.astype(F32)).astype(BF16)
    bg = jnp.concatenate([b_gate_f, b_gate_b])[None, :]
    hq = np.arange(MLA_HEADS)[:, None] * MLA_QK
    nope_idx = (hq + np.arange(MLA_NOPE)[None, :]).reshape(-1)
    rope_idx = (hq + MLA_NOPE + np.arange(MLA_ROPE)[None, :]).reshape(-1)
    swap_idx = (hq + MLA_NOPE + _ROPE_SWAP[None, :]).reshape(-1)
    wuq = w_uq[:, np.concatenate([nope_idx, rope_idx, swap_idx])].astype(BF16)
    hk = np.arange(MLA_HEADS)[:, None] * (MLA_NOPE + MLA_V)
    k_idx = (hk + np.arange(MLA_NOPE)[None, :]).reshape(-1)
    v_idx = (hk + MLA_NOPE + np.arange(MLA_V)[None, :]).reshape(-1)
    return dict(wp=wp, wkr=wkr, wg_hi=wg_hi, wg_lo=wg_lo, bg=bg, qn=q_norm[None, :], wuq=wuq,
                kvn=kv_norm[None, :], wuk=w_ukv[:, k_idx].T.astype(BF16), wuv=w_ukv[:, v_idx].astype(BF16))


def _mods(m, rows, batch):
    d = D_MODEL
    if rows is None:
        return [jnp.broadcast_to(m[batch, k * d:(k + 1) * d][None, None, :], (batch, 1, d)) for k in range(6)]
    return [m[:batch, k * d:(k + 1) * d][:, None, :] for k in range(6)]


def kernel(x, c, ctx, c_ctx, w_mod, b_mod, g_mix_pre, g_mix_post, g_ffn_pre, g_ffn_post, e_w_in, e_w_gate_f, e_b_gate_f, e_w_gate_b, e_b_gate_b, e_gla_norm, e_q_norm, e_w_uq, e_kv_norm, e_w_ukv, e_w_o, e_w1, e_w3, e_w2, o_w_o, o_w_router, o_w1, o_w3, o_w2):
    batch, seq, d = x.shape
    n_ctx = ctx.shape[1]
    cond = jnp.zeros((16, d), F32).at[:batch].set(c).at[batch].set(c_ctx)
    mods = _adaln(cond, w_mod, b_mod)
    rope_x = _rope_tables(seq)
    rope_c = _identity_rope_tables(n_ctx)
    nqk = GLA_HEADS * GLA_DK
    last_read = 2 * ((DEPTH - 1) // 2)
    xs, xc = x, ctx
    for i in range(DEPTH):
        j = i // 2
        ctx_live = i <= last_read
        ctx_full = i < last_read
        mx = _mods(mods[i], 0, batch)
        mc = _mods(mods[i], None, batch)
        gpre, gpost = g_mix_pre[i][None, :], g_mix_post[i][None, :]
        fpre, fpost = g_ffn_pre[i][None, :], g_ffn_post[i][None, :]
        if i % 2 == 0:
            w = _even_weights(e_w_in[j], e_w_gate_f[j], e_b_gate_f[j], e_w_gate_b[j], e_b_gate_b[j],
                              e_q_norm[j], e_w_uq[j], e_kv_norm[j], e_w_ukv[j])
            wo1 = e_w_o[j][:GLA_HEADS * GLA_DV].astype(BF16)
            wo2 = e_w_o[j][GLA_HEADS * GLA_DV:].astype(BF16)
            gn = e_gla_norm[j][None, :]
            w1, w3, w2 = e_w1[j].astype(BF16), e_w3[j].astype(BF16), e_w2[j].astype(BF16)
            zero_state = jnp.zeros((batch, GLA_DV, nqk), F32)
            if ctx_live:
                cgq, cgk, cgv, cgr, claf, clab, cq, ckt, cv = _in_proj(xc, gpre, mc[0], mc[1], w, rope_c, n_ctx)
                co_f, co_b, s_f, s_b = _gla(cgq, cgk, cgv, claf, clab, zero_state, zero_state, n_ctx)
                srcs_c = [(ckt, cv)]
            else:
                s_f = s_b = zero_state
                srcs_c = []
            gq, gk, gv, gr, laf, lab, q, kt, v = _in_proj(xs, gpre, mx[0], mx[1], w, rope_x, min(512, seq))
            o_f, o_b, _, _ = _gla(gq, gk, gv, laf, lab, s_f, s_b, min(256, seq))
            a = _attention(q, [(kt, v)] + srcs_c, min(1024, seq))
            xs = _even_tail(xs, o_f, o_b, gr, a, gn, wo1, wo2, gpost, mx[2], fpre, mx[3], mx[4],
                            w1, w3, w2, fpost, mx[5], min(512, seq))
            if ctx_full:
                ac = _attention(cq, srcs_c, n_ctx)
                xc = _even_tail(xc, co_f, co_b, cgr, ac, gn, wo1, wo2, gpost, mc[2], fpre, mc[3], mc[4],
                                w1, w3, w2, fpost, mc[5], n_ctx)
        else:
            wo = o_w_o[j].astype(BF16)
            wr = jnp.zeros((d, LANES), F32).at[:, :N_EXPERTS].set(o_w_router[j])
            wr_hi = wr.astype(BF16)
            wr_lo = (wr - wr_hi.astype(F32)).astype(BF16)
            w1, w3, w2 = o_w1[j].astype(BF16), o_w3[j].astype(BF16), o_w2[j].astype(BF16)
            xs = _fourier_x(xs, gpre, mx[0], mx[1], wo, gpost, mx[2])
            xs = _moe_sorted(xs, fpre, mx[3], mx[4], wr_hi, wr_lo, w1, w3, w2, fpost, mx[5],
                             min(512, seq), min(1024, seq), 1792)
            if ctx_full:
                xc = _fourier_ctx(xc, gpre, mc[0], mc[1], wo, gpost, mc[2])
                xc = _moe_sorted(xc, fpre, mc[3], mc[4], wr_hi, wr_lo, w1, w3, w2, fpost, mc[5],
                                 n_ctx, n_ctx, 1792)
    return xs
```

```python
import functools

import numpy as np
import jax
import jax.numpy as jnp
from jax import lax
from jax.experimental import pallas as pl
from jax.experimental.pallas import tpu as pltpu
from jax.experimental.pallas import tpu_sc as plsc

F32 = jnp.float32
BF16 = jnp.bfloat16

EPS = 1e-6
D_MODEL = 1024
DEPTH = 4
GRID_W = 64
GLA_HEADS = 4
GLA_DK = 64
GLA_DV = 128
GLA_RANK = 16
GLA_TAU = 16.0
GLA_CHUNK = 64
MLA_HEADS = 4
MLA_Q_RANK = 256
MLA_KV_RANK = 128
MLA_NOPE = 128
MLA_ROPE = 64
MLA_V = 128
MLA_QK = MLA_NOPE + MLA_ROPE
ROPE_BASE = 10000.0
FOURIER_GROUPS = 4
FOURIER_GW = D_MODEL // FOURIER_GROUPS
N_EXPERTS = 8
LANES = 128
VMEM_LIMIT = 48 * 1024 * 1024


def _cparams(sem):
    return pltpu.CompilerParams(dimension_semantics=sem, vmem_limit_bytes=VMEM_LIMIT)


def _dot(a, b):
    return jnp.dot(a, b, preferred_element_type=F32)


def _dot_nt(a, b):
    return lax.dot_general(a, b, (((1,), (1,)), ((), ())), preferred_element_type=F32)


def _dot_tn(a, b):
    return lax.dot_general(a, b, (((0,), (0,)), ((), ())), preferred_element_type=F32)


def _split(x):
    hi = x.astype(BF16)
    lo = (x - hi.astype(F32)).astype(BF16)
    return hi, lo


def _dot3(a, b_hi, b_lo):
    a_hi, a_lo = _split(a)
    return _dot(a_hi, b_hi) + _dot(a_lo, b_hi) + _dot(a_hi, b_lo)


def _rms(x, g):
    return x * lax.rsqrt(jnp.mean(x * x, axis=-1, keepdims=True) + EPS) * g


def _silu(x):
    return x / (1.0 + jnp.exp(-x))


def _const_spec(shape):
    nd = len(shape)
    return pl.BlockSpec(shape, lambda *_: (0,) * nd)


def _batch_row_spec(d):
    return pl.BlockSpec((1, 1, d), lambda b, *_: (b, 0, 0))


def _adaln_kernel(c_ref, w_ref, b_ref, o_ref):
    a = _silu(c_ref[...])
    w_hi, w_lo = _split(w_ref[0])
    o_ref[0] = _dot3(a, w_hi, w_lo) + b_ref[0]


def _adaln(cond, w_mod, b_mod):
    depth, d, n = w_mod.shape
    rows = cond.shape[0]
    bn = 1536
    return pl.pallas_call(
        _adaln_kernel,
        grid=(depth, n // bn),
        in_specs=[
            pl.BlockSpec((rows, d), lambda i, j: (0, 0)),
            pl.BlockSpec((1, d, bn), lambda i, j: (i, 0, j)),
            pl.BlockSpec((1, 1, bn), lambda i, j: (i, 0, j)),
        ],
        out_specs=pl.BlockSpec((1, rows, bn), lambda i, j: (i, 0, j)),
        out_shape=jax.ShapeDtypeStruct((depth, rows, n), F32),
        compiler_params=_cparams(("arbitrary", "arbitrary")),
        name="adaln",
    )(cond, w_mod, b_mod.reshape(depth, 1, n))


def _in_proj_kernel(x_ref, g_ref, sh_ref, sc_ref, wp_ref, wkr_ref, wgh_ref, wgl_ref, bg_ref,
                    qn_ref, wuq_ref, kvn_ref, wuk_ref, wuv_ref, cq_ref, sq_ref, ck_ref, sk_ref,
                    gq_ref, gk_ref, gv_ref, gr_ref, laf_ref, lab_ref, q_ref, kt_ref, v_ref):
    h = _rms(x_ref[0], g_ref[...]) * (1.0 + sc_ref[0]) + sh_ref[0]
    hb = h.astype(BF16)
    p = _dot(hb, wp_ref[...])
    nqk = GLA_HEADS * GLA_DK
    nv = GLA_HEADS * GLA_DV
    gq_ref[0] = p[:, 0:nqk] * (GLA_DK ** -0.5)
    gk_ref[0] = p[:, nqk:2 * nqk]
    gv_ref[0] = p[:, 2 * nqk:2 * nqk + nv]
    gr_ref[0] = p[:, 2 * nqk + nv:2 * nqk + 2 * nv]
    o = 2 * nqk + 2 * nv
    cq = p[:, o:o + MLA_Q_RANK]
    ckv = p[:, o + MLA_Q_RANK:o + MLA_Q_RANK + MLA_KV_RANK]
    tail = p[:, o + MLA_Q_RANK + MLA_KV_RANK:]
    pre = _dot3(tail, wgh_ref[...], wgl_ref[...]) + bg_ref[...]
    la = (jnp.minimum(pre, 0.0) - jnp.log(1.0 + jnp.exp(-jnp.abs(pre)))) * (1.0 / GLA_TAU)
    laf_ref[0] = la[:, :nqk]
    lab_ref[0] = la[:, nqk:]
    q = _dot(_rms(cq, qn_ref[...]).astype(BF16), wuq_ref[...])
    att_scale = MLA_QK ** -0.5
    nn = MLA_HEADS * MLA_NOPE
    nr = MLA_HEADS * MLA_ROPE
    q_rope = q[:, nn:nn + nr] * cq_ref[...] + q[:, nn + nr:] * sq_ref[...]
    for hd in range(MLA_HEADS):
        q_ref[0, hd, :, 0:MLA_NOPE] = (q[:, hd * MLA_NOPE:(hd + 1) * MLA_NOPE] * att_scale).astype(BF16)
        q_ref[0, hd, :, MLA_NOPE:MLA_QK] = (q_rope[:, hd * MLA_ROPE:(hd + 1) * MLA_ROPE] * att_scale).astype(BF16)
    ckvn = _rms(ckv, kvn_ref[...]).astype(BF16)
    kt = _dot_nt(wuk_ref[...], ckvn)
    v = _dot(ckvn, wuv_ref[...])
    kr2 = _dot_nt(wkr_ref[...], hb)
    kr = (kr2[:MLA_ROPE] * ck_ref[...] + kr2[MLA_ROPE:] * sk_ref[...]).astype(BF16)
    for hd in range(MLA_HEADS):
        kt_ref[0, hd, 0:MLA_NOPE, :] = kt[hd * MLA_NOPE:(hd + 1) * MLA_NOPE].astype(BF16)
        kt_ref[0, hd, MLA_NOPE:MLA_QK, :] = kr
        v_ref[0, hd] = v[:, hd * MLA_V:(hd + 1) * MLA_V].astype(BF16)


def _in_proj(tok, g, sh, sc, w, tabs, tm):
    b, t, d = tok.shape
    cq, sq, ck, sk = tabs
    nqk = GLA_HEADS * GLA_DK
    nv = GLA_HEADS * GLA_DV
    row = lambda n: pl.BlockSpec((1, tm, n), lambda bi, ti: (bi, ti, 0))
    weights = (w["wp"], w["wkr"], w["wg_hi"], w["wg_lo"], w["bg"], w["qn"], w["wuq"], w["kvn"],
               w["wuk"], w["wuv"])
    return pl.pallas_call(
        _in_proj_kernel,
        grid=(b, t // tm),
        in_specs=[row(d), _const_spec((1, d)), _batch_row_spec(d), _batch_row_spec(d)]
        + [_const_spec(a.shape) for a in weights]
        + [pl.BlockSpec((tm, MLA_HEADS * MLA_ROPE), lambda bi, ti: (ti, 0)),
           pl.BlockSpec((tm, MLA_HEADS * MLA_ROPE), lambda bi, ti: (ti, 0)),
           pl.BlockSpec((MLA_ROPE, tm), lambda bi, ti: (0, ti)),
           pl.BlockSpec((MLA_ROPE, tm), lambda bi, ti: (0, ti))],
        out_specs=[row(nqk), row(nqk), row(nv), row(nv), row(nqk), row(nqk),
                   pl.BlockSpec((1, MLA_HEADS, tm, MLA_QK), lambda bi, ti: (bi, 0, ti, 0)),
                   pl.BlockSpec((1, MLA_HEADS, MLA_QK, tm), lambda bi, ti: (bi, 0, 0, ti)),
                   pl.BlockSpec((1, MLA_HEADS, tm, MLA_V), lambda bi, ti: (bi, 0, ti, 0))],
        out_shape=[jax.ShapeDtypeStruct((b, t, nqk), F32), jax.ShapeDtypeStruct((b, t, nqk), F32),
                   jax.ShapeDtypeStruct((b, t, nv), F32), jax.ShapeDtypeStruct((b, t, nv), F32),
                   jax.ShapeDtypeStruct((b, t, nqk), F32), jax.ShapeDtypeStruct((b, t, nqk), F32),
                   jax.ShapeDtypeStruct((b, MLA_HEADS, t, MLA_QK), BF16),
                   jax.ShapeDtypeStruct((b, MLA_HEADS, MLA_QK, t), BF16),
                   jax.ShapeDtypeStruct((b, MLA_HEADS, t, MLA_V), BF16)],
        compiler_params=_cparams(("parallel", "parallel")),
        name="even_in_proj",
    )(tok, g, sh, sc, *weights, cq, sq, ck, sk)


def _gla_chunks(streams):
    c = GLA_CHUNK
    heads = range(GLA_HEADS)
    ks = [slice(hd * GLA_DK, (hd + 1) * GLA_DK) for hd in heads]
    vs = [slice(hd * GLA_DV, (hd + 1) * GLA_DV) for hd in heads]
    bcs = []
    for (_, _, _, l_ref, _, _, g, row0, tri, _, _) in streams:
        la_hi, la_lo = _split(l_ref[g, pl.ds(row0, c), :])
        bcs.append(_dot(tri, la_hi) + _dot(tri, la_lo))
    ops = []
    for (q_ref, k_ref, v_ref, _, _, st_ref, g, row0, _, _, last_row), bc in zip(streams, bcs):
        bl = bc[last_row:last_row + 1, :]
        q = q_ref[g, pl.ds(row0, c), :]
        k = k_ref[g, pl.ds(row0, c), :]
        st = st_ref[g]
        ops.append(dict(qc=(q * jnp.exp(bc)).astype(BF16), kc=(k * jnp.exp(-bc)).astype(BF16),
                        kd=(k * jnp.exp(bl - bc)).astype(BF16), vb=v_ref[g, pl.ds(row0, c), :].astype(BF16),
                        st=st, stb=st.astype(BF16), decay=jnp.exp(bl)))
    atts = [[_dot_nt(o["qc"][:, ks[hd]], o["kc"][:, ks[hd]]) for hd in heads] for o in ops]
    inters = [[_dot_nt(o["qc"][:, ks[hd]], o["stb"][:, ks[hd]]) for hd in heads] for o in ops]
    upds = [[_dot_tn(o["vb"][:, vs[hd]], o["kd"][:, ks[hd]]) for hd in heads] for o in ops]
    intras = [[_dot(jnp.where(s[9], att[hd], 0.0).astype(BF16), o["vb"][:, vs[hd]]) for hd in heads]
              for s, o, att in zip(streams, ops, atts)]
    for s, o, intra, inter, upd in zip(streams, ops, intras, inters, upds):
        o_ref, st_ref, g, row0 = s[4], s[5], s[6], s[7]
        o_ref[g, pl.ds(row0, c), :] = jnp.concatenate([a + b for a, b in zip(intra, inter)], axis=1)
        st_ref[g] = o["st"] * o["decay"] + jnp.concatenate(upd, axis=1)


def _gla_kernel(qf, kf, vf, lf, qb, kb, vb, lb, s0f, s0b, of, ob, sff, sfb, stf, stb, *, nc, gb):
    j = pl.program_id(1)

    @pl.when(j == 0)
    def _():
        stf[...] = s0f[...]
        stb[...] = s0b[...]

    c = GLA_CHUNK
    r = lax.broadcasted_iota(jnp.int32, (c, c), 0)
    cc = lax.broadcasted_iota(jnp.int32, (c, c), 1)
    lower = r >= cc
    upper = r <= cc
    tri_l = jnp.where(lower, 1.0, 0.0).astype(BF16)
    tri_u = jnp.where(upper, 1.0, 0.0).astype(BF16)

    for ci in range(nc):
        streams = []
        for g in range(gb):
            streams.append((qf, kf, vf, lf, of, stf, g, ci * c, tri_l, lower, c - 1))
            streams.append((qb, kb, vb, lb, ob, stb, g, (nc - 1 - ci) * c, tri_u, upper, 0))
        _gla_chunks(streams)

    @pl.when(j == pl.num_programs(1) - 1)
    def _():
        sff[...] = stf[...]
        sfb[...] = stb[...]


GLA_BATCH_ROWS = 4


def _gla(gq, gk, gv, laf, lab, s0f, s0b, tb):
    b, t, nqk = gq.shape
    nv = gv.shape[-1]
    nblk = t // tb
    gb = min(GLA_BATCH_ROWS, b)
    fwd = lambda n: pl.BlockSpec((gb, tb, n), lambda bi, j: (bi, j, 0))
    bwd = lambda n: pl.BlockSpec((gb, tb, n), lambda bi, j: (bi, nblk - 1 - j, 0))
    st = pl.BlockSpec((gb, GLA_DV, nqk), lambda bi, j: (bi, 0, 0))
    return pl.pallas_call(
        functools.partial(_gla_kernel, nc=tb // GLA_CHUNK, gb=gb),
        grid=(b // gb, nblk),
        in_specs=[fwd(nqk), fwd(nqk), fwd(nv), fwd(nqk), bwd(nqk), bwd(nqk), bwd(nv), bwd(nqk), st, st],
        out_specs=[fwd(nv), bwd(nv), st, st],
        out_shape=[jax.ShapeDtypeStruct((b, t, nv), F32), jax.ShapeDtypeStruct((b, t, nv), F32),
                   jax.ShapeDtypeStruct((b, GLA_DV, nqk), F32), jax.ShapeDtypeStruct((b, GLA_DV, nqk), F32)],
        scratch_shapes=[pltpu.VMEM((gb, GLA_DV, nqk), F32), pltpu.VMEM((gb, GLA_DV, nqk), F32)],
        compiler_params=_cparams(("parallel", "arbitrary")),
        name="gla_scan",
    )(gq, gk, gv, laf, gq, gk, gv, lab, s0f, s0b)


ATTN_KEY_CHUNK = 1024


def _attn_kernel(*refs, n_src):
    q_ref = refs[0]
    kts = refs[1:1 + 2 * n_src:2]
    vs = refs[2:2 + 2 * n_src:2]
    o_ref = refs[1 + 2 * n_src]
    vexts = refs[2 + 2 * n_src:]

    @pl.when(pl.program_id(2) == 0)
    def _():
        for v, vext in zip(vs, vexts):
            tk = v.shape[2]
            lane = lax.broadcasted_iota(jnp.int32, (tk, LANES), 1)
            vext[:, 0:MLA_V] = v[0, 0]
            vext[:, MLA_V:MLA_V + LANES] = jnp.where(lane == 0, 1.0, 0.0).astype(BF16)

    q = q_ref[0, 0]
    bq = q.shape[0]
    m = jnp.full((bq, 1), -jnp.inf, F32)
    acc = jnp.zeros((bq, MLA_V + LANES), F32)
    chunks = []
    for kt, vext in zip(kts, vexts):
        tk = kt.shape[3]
        ck = min(ATTN_KEY_CHUNK, tk)
        chunks += [(kt, vext, c * ck, (c + 1) * ck) for c in range(tk // ck)]
    s_next = _dot(q, chunks[0][0][0, 0, :, chunks[0][2]:chunks[0][3]])
    for i, (kt, vext, lo, hi) in enumerate(chunks):
        s = s_next
        if i + 1 < len(chunks):
            kt_n, _, lo_n, hi_n = chunks[i + 1]
            s_next = _dot(q, kt_n[0, 0, :, lo_n:hi_n])
        m_new = jnp.maximum(m, s.max(axis=-1, keepdims=True))
        p = jnp.exp((s - m_new).astype(BF16))
        acc = acc * jnp.exp(m - m_new) + _dot(p, vext[lo:hi, :])
        m = m_new
    o_ref[0] = (acc[:, 0:MLA_V] / acc[:, MLA_V:MLA_V + 1]).astype(o_ref.dtype)


def _attention(q, srcs, bq):
    b, nh, t, dqk = q.shape
    in_specs = [pl.BlockSpec((1, 1, bq, dqk), lambda bi, hi, qi: (bi, hi, qi, 0))]
    args = [q]
    scratch = []
    for kt, v in srcs:
        tk = kt.shape[-1]
        in_specs.append(pl.BlockSpec((1, 1, dqk, tk), lambda bi, hi, qi: (bi, hi, 0, 0)))
        in_specs.append(pl.BlockSpec((1, 1, tk, MLA_V), lambda bi, hi, qi: (bi, hi, 0, 0)))
        args += [kt, v]
        scratch.append(pltpu.VMEM((tk, MLA_V + LANES), BF16))
    return pl.pallas_call(
        functools.partial(_attn_kernel, n_src=len(srcs)),
        grid=(b, nh, t // bq),
        in_specs=in_specs,
        out_specs=pl.BlockSpec((1, bq, MLA_V), lambda bi, hi, qi: (bi, qi, hi)),
        out_shape=jax.ShapeDtypeStruct((b, t, nh * MLA_V), BF16),
        scratch_shapes=scratch,
        compiler_params=_cparams(("parallel", "parallel", "arbitrary")),
        name="mla_attention",
    )(*args)


MXU_COLS = 256


def _ff_chunks(dff, max_tiles):
    n_tiles = dff // MXU_COLS
    n_chunks = -(-n_tiles // max_tiles)
    base, extra = divmod(n_tiles, n_chunks)
    bounds, start = [], 0
    for i in range(n_chunks):
        width = (base + (1 if i < extra else 0)) * MXU_COLS
        bounds.append((start, start + width))
        start += width
    return bounds


def _even_tail_kernel(x_ref, of_ref, ob_ref, gr_ref, a_ref, gn_ref, wo1_ref, wo2_ref, gpost_ref, ga_ref,
                      gpre_ref, sh_ref, sc_ref, w1_ref, w3_ref, w2_ref, fpost_ref, gaf_ref, xo_ref, *, chunks):
    o = of_ref[0] + ob_ref[0]
    parts = [_rms(o[:, hd * GLA_DV:(hd + 1) * GLA_DV], gn_ref[...]) for hd in range(GLA_HEADS)]
    fin = jnp.concatenate(parts, axis=1) * _silu(gr_ref[0])
    y = _dot(fin.astype(BF16), wo1_ref[...]) + _dot(a_ref[0], wo2_ref[...])
    xn = x_ref[0] + ga_ref[0] * _rms(y, gpost_ref[...])
    h = (_rms(xn, gpre_ref[...]) * (1.0 + sc_ref[0]) + sh_ref[0]).astype(BF16)
    z = None
    for lo, hi in chunks:
        a = _dot(h, w1_ref[:, lo:hi])
        mid = (_silu(a) * _dot(h, w3_ref[:, lo:hi])).astype(BF16)
        part = _dot(mid, w2_ref[lo:hi, :])
        z = part if z is None else z + part
    xo_ref[0] = xn + gaf_ref[0] * _rms(z, fpost_ref[...])


def _even_tail(tok, o_f, o_b, gr, a, gn, wo1, wo2, gpost, ga, gpre, sh, sc, w1, w3, w2, fpost, gaf, tm):
    b, t, d = tok.shape
    nv = o_f.shape[-1]
    dff = w1.shape[-1]
    row = lambda n: pl.BlockSpec((1, tm, n), lambda bi, ti: (bi, ti, 0))
    resident = lambda shape: pl.BlockSpec(shape, lambda bi, ti: (0, 0), pipeline_mode=pl.Buffered(1))
    return pl.pallas_call(
        functools.partial(_even_tail_kernel, chunks=_ff_chunks(dff, 6)),
        grid=(b, t // tm),
        in_specs=[row(d), row(nv), row(nv), row(nv), row(a.shape[-1]), _const_spec(gn.shape),
                  resident(wo1.shape), resident(wo2.shape), _const_spec((1, d)), _batch_row_spec(d),
                  _const_spec((1, d)), _batch_row_spec(d), _batch_row_spec(d),
                  resident((d, dff)), resident((d, dff)), resident((dff, d)),
                  _const_spec((1, d)), _batch_row_spec(d)],
        out_specs=row(d),
        out_shape=jax.ShapeDtypeStruct((b, t, d), F32),
        compiler_params=_cparams(("parallel", "parallel")),
        name="even_tail",
    )(tok, o_f, o_b, gr, a, gn, wo1, wo2, gpost, ga, gpre, sh, sc, w1, w3, w2, fpost, gaf)


def _dft_cos_sin(n):
    idx = (np.arange(n)[:, None] * np.arange(n)[None, :]) % n
    ang = 2.0 * np.pi * idx.astype(np.float64) / n
    return np.cos(ang), np.sin(ang)


def _fourier1_kernel(x_ref, g_ref, sh_ref, sc_ref, f1_ref, o_ref, *, nb, n1):
    xt = jnp.swapaxes(x_ref[0], 0, 1)
    for j in range(nb):
        hb = (_rms(xt[j], g_ref[...]) * (1.0 + sc_ref[0]) + sh_ref[0]).astype(BF16)
        a = _dot(f1_ref[j], hb)
        o_ref[0, 0, j] = a[:n1].astype(o_ref.dtype)
        o_ref[0, 1, j] = a[n1:].astype(o_ref.dtype)


def _fourier2_kernel(b_ref, x_ref, f2_ref, cc_ref, sc_ref, wo_ref, gpost_ref, ga_ref, xo_ref, *, kb, n2):
    d = b_ref.shape[-1]
    br = jnp.swapaxes(b_ref[0, 0].astype(F32), 0, 1)
    bi = jnp.swapaxes(b_ref[0, 1].astype(F32), 0, 1)
    urs, uis = [], []
    for j in range(kb):
        u = _dot(f2_ref[...], jnp.concatenate([br[j], bi[j]], axis=0).astype(BF16))
        urs.append(u[:n2])
        uis.append(u[n2:])
    ur = jnp.concatenate(urs, axis=0).astype(BF16)
    ui = jnp.concatenate(uis, axis=0).astype(BF16)
    gw = FOURIER_GW
    f = jnp.concatenate(
        [_dot(ur[:, g * gw:(g + 1) * gw], cc_ref[...]) + _dot(ui[:, g * gw:(g + 1) * gw], sc_ref[...])
         for g in range(FOURIER_GROUPS)], axis=1)
    y = _dot(f.astype(BF16), wo_ref[...])
    yn = ga_ref[0] * _rms(y, gpost_ref[...])
    xo_ref[0] = x_ref[0] + jnp.swapaxes(yn.reshape(kb, n2, d), 0, 1)


def _fourier_x(tok, g, sh, sc, wo, gpost, ga):
    b, t, d = tok.shape
    n1, n2 = 128, t // 128
    nb, kb = min(8, n2), 16
    c2, s2 = _dft_cos_sin(n2)
    cg, sg = _dft_cos_sin(FOURIER_GW)
    tok_idx = np.arange(n2)[:, None, None] + n2 * np.arange(n1)[None, None, :]
    ang = 2.0 * np.pi * ((np.arange(n1)[None, :, None] * tok_idx) % t).astype(np.float64) / t
    f1 = jnp.asarray(np.concatenate([np.cos(ang), -np.sin(ang)], axis=1) / np.sqrt(n1), BF16)
    f2 = jnp.asarray(np.block([[c2, s2], [-s2, c2]]) / np.sqrt(n2), BF16)
    ccg = jnp.asarray(cg / np.sqrt(FOURIER_GW), BF16)
    scg = jnp.asarray(sg / np.sqrt(FOURIER_GW), BF16)
    stage1 = pl.pallas_call(
        functools.partial(_fourier1_kernel, nb=nb, n1=n1),
        grid=(b, n2 // nb),
        in_specs=[pl.BlockSpec((1, n1, nb, d), lambda bi, ji: (bi, 0, ji, 0)),
                  _const_spec((1, d)), _batch_row_spec(d), _batch_row_spec(d),
                  pl.BlockSpec((nb, 2 * n1, n1), lambda bi, ji: (ji, 0, 0))],
        out_specs=pl.BlockSpec((1, 2, nb, n1, d), lambda bi, ji: (bi, 0, ji, 0, 0)),
        out_shape=jax.ShapeDtypeStruct((b, 2, n2, n1, d), BF16),
        compiler_params=_cparams(("parallel", "parallel")),
        name="fourier_stage1",
    )(tok.reshape(b, n1, n2, d), g, sh, sc, f1)
    freq = lambda n: pl.BlockSpec((1, n2, kb, n), lambda bi, ki: (bi, 0, ki, 0))
    out = pl.pallas_call(
        functools.partial(_fourier2_kernel, kb=kb, n2=n2),
        grid=(b, n1 // kb),
        in_specs=[pl.BlockSpec((1, 2, n2, kb, d), lambda bi, ki: (bi, 0, 0, ki, 0)),
                  freq(d),
                  _const_spec(f2.shape), _const_spec(ccg.shape), _const_spec(scg.shape),
                  _const_spec(wo.shape), _const_spec((1, d)), _batch_row_spec(d)],
        out_specs=freq(d),
        out_shape=jax.ShapeDtypeStruct((b, n2, n1, d), F32),
        compiler_params=_cparams(("parallel", "parallel")),
        name="fourier_stage2",
    )(stage1, tok.reshape(b, n2, n1, d), f2, ccg, scg, wo, gpost, ga)
    return out.reshape(b, t, d)


def _fourier_ctx_kernel(x_ref, g_ref, sh_ref, sc_ref, fl_ref, cc_ref, scg_ref, wo_ref, gpost_ref, ga_ref,
                        xo_ref):
    x = x_ref[0]
    t = x.shape[0]
    h = _rms(x, g_ref[...]) * (1.0 + sc_ref[0]) + sh_ref[0]
    u = _dot(fl_ref[...], h.astype(BF16))
    ur = u[:t].astype(BF16)
    ui = u[t:].astype(BF16)
    gw = FOURIER_GW
    f = jnp.concatenate(
        [_dot(ur[:, g * gw:(g + 1) * gw], cc_ref[...]) + _dot(ui[:, g * gw:(g + 1) * gw], scg_ref[...])
         for g in range(FOURIER_GROUPS)], axis=1)
    y = _dot(f.astype(BF16), wo_ref[...])
    xo_ref[0] = x + ga_ref[0] * _rms(y, gpost_ref[...])


def _fourier_ctx(tok, g, sh, sc, wo, gpost, ga):
    b, t, d = tok.shape
    cl, sl = _dft_cos_sin(t)
    cg, sg = _dft_cos_sin(FOURIER_GW)
    fl = jnp.asarray(np.concatenate([cl, -sl], axis=0) / np.sqrt(t), BF16)
    ccg = jnp.asarray(cg / np.sqrt(FOURIER_GW), BF16)
    scg = jnp.asarray(sg / np.sqrt(FOURIER_GW), BF16)
    row = pl.BlockSpec((1, t, d), lambda bi: (bi, 0, 0))
    return pl.pallas_call(
        _fourier_ctx_kernel,
        grid=(b,),
        in_specs=[row, _const_spec((1, d)), _batch_row_spec(d), _batch_row_spec(d), _const_spec(fl.shape),
                  _const_spec(ccg.shape), _const_spec(scg.shape), _const_spec(wo.shape),
                  _const_spec((1, d)), _batch_row_spec(d)],
        out_specs=row,
        out_shape=jax.ShapeDtypeStruct((b, t, d), F32),
        compiler_params=_cparams(("parallel",)),
        name="fourier_ctx",
    )(tok, g, sh, sc, fl, ccg, scg, wo, gpost, ga)


PLANE_W = 256
N_PLANES = D_MODEL // (2 * PLANE_W)
SC_WINDOW = 128
ROUTE_META_ROWS = 8


def _pack_planes(h):
    out = []
    for p in range(N_PLANES):
        base = 2 * p * PLANE_W
        hi = pltpu.bitcast(h[:, base:base + PLANE_W].astype(BF16).astype(F32), jnp.uint32)
        lo = pltpu.bitcast(h[:, base + PLANE_W:base + 2 * PLANE_W].astype(BF16).astype(F32), jnp.uint32)
        out.append(hi | (lo >> 16))
    return out


def _unpack_planes(planes):
    cols = []
    for w in planes:
        cols.append(pltpu.bitcast(w & jnp.uint32(0xFFFF0000), F32))
        cols.append(pltpu.bitcast(w << 16, F32))
    return jnp.concatenate(cols, axis=1)


def _route_rows(h, wrh_ref, wrl_ref, carry_ref):
    logits = _dot3(h, wrh_ref[...], wrl_ref[...])
    tm = logits.shape[0]
    lane = lax.broadcasted_iota(jnp.int32, logits.shape, 1)
    neg = -jnp.inf
    l1 = jnp.where(lane < N_EXPERTS, logits, neg)
    m1 = l1.max(axis=-1, keepdims=True)
    i1 = jnp.where(l1 == m1, lane, LANES).min(axis=-1, keepdims=True)
    l2 = jnp.where(lane == i1, neg, l1)
    m2 = l2.max(axis=-1, keepdims=True)
    i2 = jnp.where(l2 == m2, lane, LANES).min(axis=-1, keepdims=True)
    e = jnp.exp(m2 - m1)
    g1 = 1.0 / (1.0 + e)
    g2 = e / (1.0 + e)
    sel = jnp.logical_or(lane == i1, lane == i2)
    cnt = jnp.where(sel, 1.0, 0.0)
    r = lax.broadcasted_iota(jnp.int32, (tm, tm), 0)
    c = lax.broadcasted_iota(jnp.int32, (tm, tm), 1)
    below = jnp.where(r > c, 1.0, 0.0).astype(BF16)
    before = _dot(below, cnt.astype(BF16)) + carry_ref[...]
    r1 = jnp.where(lane == i1, before, 0.0).sum(axis=-1, keepdims=True).astype(jnp.int32)
    r2 = jnp.where(lane == i2, before, 0.0).sum(axis=-1, keepdims=True).astype(jnp.int32)
    carry_ref[...] = carry_ref[...] + cnt.sum(axis=0, keepdims=True)
    im = jnp.where(lane == 0, i1, jnp.where(lane == 1, i2, jnp.where(lane == 2, r1, r2)))
    return im, jnp.where(lane == 0, g1, g2)


def _route_kernel(x_ref, g_ref, sh_ref, sc_ref, wrh_ref, wrl_ref, hp_ref, im_ref, gm_ref, cnt_ref, carry_ref):
    first = jnp.logical_and(pl.program_id(0) == 0, pl.program_id(1) == 0)

    @pl.when(first)
    def _():
        carry_ref[...] = jnp.zeros_like(carry_ref)

    h = _rms(x_ref[0], g_ref[...]) * (1.0 + sc_ref[0]) + sh_ref[0]
    for p, w in enumerate(_pack_planes(h)):
        hp_ref[p, 0] = w
    im, gm_ref[0] = _route_rows(h, wrh_ref, wrl_ref, carry_ref)
    im_ref[0] = jnp.transpose(im)[0:ROUTE_META_ROWS, :]
    cnt_ref[...] = carry_ref[...]


def _route(tok, g, sh, sc, wr_hi, wr_lo, tm):
    b, t, d = tok.shape
    row = lambda n: pl.BlockSpec((1, tm, n), lambda bi, ti: (bi, ti, 0))
    return pl.pallas_call(
        _route_kernel,
        grid=(b, t // tm),
        in_specs=[row(d), _const_spec((1, d)), _batch_row_spec(d), _batch_row_spec(d),
                  _const_spec(wr_hi.shape), _const_spec(wr_lo.shape)],
        out_specs=[pl.BlockSpec((N_PLANES, 1, tm, PLANE_W), lambda bi, ti: (0, bi, ti, 0)),
                   pl.BlockSpec((1, ROUTE_META_ROWS, tm), lambda bi, ti: (bi, 0, ti)),
                   row(LANES), _const_spec((1, LANES))],
        out_shape=[jax.ShapeDtypeStruct((N_PLANES, b, t, PLANE_W), jnp.uint32),
                   jax.ShapeDtypeStruct((b, ROUTE_META_ROWS, t), jnp.int32),
                   jax.ShapeDtypeStruct((b, t, LANES), F32),
                   jax.ShapeDtypeStruct((1, LANES), F32)],
        scratch_shapes=[pltpu.VMEM((1, LANES), F32)],
        compiler_params=_cparams(("arbitrary", "arbitrary")),
        name="moe_route",
    )(tok, g, sh, sc, wr_hi, wr_lo)


def _sc_mesh():
    return plsc.VectorSubcoreMesh(core_axis_name="core", subcore_axis_name="subcore")


def _sc_gather_rows(table, idx):
    n = idx.shape[0]
    w = table.shape[1]

    @pl.kernel(out_type=jax.ShapeDtypeStruct((n, w), table.dtype), mesh=_sc_mesh())
    def gather(t_hbm, i_hbm, o_hbm):
        def body(i_vmem, o_vmem):
            pltpu.sync_copy(t_hbm.at[i_vmem.at[0]], o_vmem)

        pltpu.emit_pipeline(
            body, grid=(n // SC_WINDOW,),
            in_specs=[pl.BlockSpec((1, SC_WINDOW), index_map=lambda i: (0, i))],
            out_specs=[pl.BlockSpec((SC_WINDOW, w), index_map=lambda i: (i, 0))],
            core_axis_name=("core", "subcore"), dimension_semantics=(pltpu.PARALLEL,),
        )(i_hbm, o_hbm)

    return gather(table, idx.reshape(1, n))


def _sc_scatter_rows(src, idx, n_out):
    n = idx.shape[0]
    w = src.shape[1]
    n_src_windows = src.shape[0] // SC_WINDOW

    @pl.kernel(out_type=jax.ShapeDtypeStruct((n_out, w), src.dtype), mesh=_sc_mesh(), scratch_types=[])
    def scatter(s_hbm, i_hbm, o_hbm):
        def body(s_vmem, i_vmem):
            pltpu.sync_copy(s_vmem, o_hbm.at[i_vmem.at[0]])

        pltpu.emit_pipeline(
            body, grid=(n // SC_WINDOW,),
            in_specs=[pl.BlockSpec((SC_WINDOW, w), index_map=lambda i: (i % n_src_windows, 0)),
                      pl.BlockSpec((1, SC_WINDOW), index_map=lambda i: (0, i))],
            out_specs=[],
            core_axis_name=("core", "subcore"), dimension_semantics=(pltpu.PARALLEL,),
        )(s_hbm, i_hbm)

    return scatter(src, idx.reshape(1, n))


def _grouped_ffn_kernel(te_ref, nv_ref, xs_ref, w1_ref, w3_ref, w2_ref, y_ref, hb_ref, acc_ref):
    i = pl.program_id(0)
    f = pl.program_id(1)

    @pl.when(i < nv_ref[0])
    def _():
        @pl.when(f == 0)
        def _():
            hb_ref[...] = _unpack_planes([xs_ref[p] for p in range(N_PLANES)]).astype(BF16)

        half = hb_ref.shape[0] // 2
        for r in range(2):
            rows = slice(r * half, (r + 1) * half)
            h = hb_ref[rows, :]
            a = _dot(h, w1_ref[0])
            mid = (_silu(a) * _dot(h, w3_ref[0])).astype(BF16)
            contrib = _dot(mid, w2_ref[0])

            @pl.when(f == 0)
            def _():
                acc_ref[rows, :] = contrib

            @pl.when(f > 0)
            def _():
                acc_ref[rows, :] += contrib

        @pl.when(f == pl.num_programs(1) - 1)
        def _():
            for p, w in enumerate(_pack_planes(acc_ref[...])):
                y_ref[p] = w


def _grouped_ffn(xs, tile_expert, n_valid, w1, w3, w2, tm, fc):
    n_pad = xs.shape[1]
    ne, d, dff = w1.shape
    plane = pl.BlockSpec((N_PLANES, tm, PLANE_W), lambda i, f, te, nv: (0, i, 0))
    return pl.pallas_call(
        _grouped_ffn_kernel,
        grid_spec=pltpu.PrefetchScalarGridSpec(
            num_scalar_prefetch=2,
            grid=(n_pad // tm, dff // fc),
            in_specs=[plane,
                      pl.BlockSpec((1, d, fc), lambda i, f, te, nv: (te[i], 0, f)),
                      pl.BlockSpec((1, d, fc), lambda i, f, te, nv: (te[i], 0, f)),
                      pl.BlockSpec((1, fc, d), lambda i, f, te, nv: (te[i], f, 0))],
            out_specs=plane,
            scratch_shapes=[pltpu.VMEM((tm, d), BF16), pltpu.VMEM((tm, d), F32)]),
        out_shape=jax.ShapeDtypeStruct(xs.shape, jnp.uint32),
        compiler_params=_cparams(("arbitrary", "arbitrary")),
        name="moe_grouped_ffn",
    )(tile_expert, n_valid, xs, w1, w3, w2)


def _combine_kernel(x_ref, y_ref, gm_ref, g_ref, ga_ref, xo_ref):
    gm = gm_ref[0]
    y1 = _unpack_planes([y_ref[p, 0, 0] for p in range(N_PLANES)])
    y2 = _unpack_planes([y_ref[p, 1, 0] for p in range(N_PLANES)])
    mix = gm[:, 0:1] * y1 + gm[:, 1:2] * y2
    xo_ref[0] = x_ref[0] + ga_ref[0] * _rms(mix, g_ref[...])


def _combine(tok, yg, gm, g, ga, tm):
    b, t, d = tok.shape
    row = lambda n: pl.BlockSpec((1, tm, n), lambda bi, ti: (bi, ti, 0))
    return pl.pallas_call(
        _combine_kernel,
        grid=(b, t // tm),
        in_specs=[row(d),
                  pl.BlockSpec((N_PLANES, 2, 1, tm, PLANE_W), lambda bi, ti: (0, 0, bi, ti, 0)),
                  row(LANES), _const_spec((1, d)), _batch_row_spec(d)],
        out_specs=row(d),
        out_shape=jax.ShapeDtypeStruct((b, t, d), F32),
        compiler_params=_cparams(("parallel", "parallel")),
        name="moe_combine",
    )(tok, yg, gm, g, ga)


def _moe_sorted(tok, g_pre, sh, sc, wr_hi, wr_lo, w1, w3, w2, expert_base, g_post, ga, tm, tm_e, fc):
    b, t, d = tok.shape
    n_tok = b * t
    hp, im, gm, cnt = _route(tok, g_pre, sh, sc, wr_hi, wr_lo, tm)
    counts = cnt[0, :N_EXPERTS].astype(jnp.int32)
    padded = ((counts + tm_e - 1) // tm_e) * tm_e
    ends = jnp.cumsum(padded)
    starts = ends - padded
    n_pad = 2 * n_tok + N_EXPERTS * tm_e
    n_tiles = n_pad // tm_e
    tile_expert = jnp.minimum(
        jnp.sum((jnp.arange(n_tiles, dtype=jnp.int32)[:, None] * tm_e >= ends[None, :]).astype(jnp.int32), axis=1),
        N_EXPERTS - 1).astype(jnp.int32) + expert_base
    n_valid = (ends[-1:] // tm_e).astype(jnp.int32)
    e1, e2, r1, r2 = (im[:, k, :].reshape(n_tok) for k in range(4))
    pos = jnp.stack([starts[e1] + r1, starts[e2] + r2], axis=0)
    plane_off = (jnp.arange(N_PLANES, dtype=jnp.int32) * n_pad)
    idx_dispatch = (pos[:, None, :] + plane_off[None, :, None]).reshape(-1)
    idx_return = (pos[None, :, :] + plane_off[:, None, None]).reshape(-1)
    xs = _sc_scatter_rows(hp.reshape(N_PLANES * n_tok, PLANE_W), idx_dispatch, N_PLANES * n_pad)
    ys = _grouped_ffn(xs.reshape(N_PLANES, n_pad, PLANE_W), tile_expert, n_valid, w1, w3, w2, tm_e, fc)
    yg = _sc_gather_rows(ys.reshape(N_PLANES * n_pad, PLANE_W), idx_return)
    return _combine(tok, yg.reshape(N_PLANES, 2, b, t, PLANE_W), gm, g_post, ga, tm)


_ROPE_SWAP = np.concatenate([np.arange(16, 32), np.arange(0, 16), np.arange(48, 64), np.arange(32, 48)])


def _rope_tables(n_tok):
    rows = n_tok // GRID_W
    row = jnp.broadcast_to(jnp.arange(rows, dtype=F32)[:, None], (rows, GRID_W)).reshape(-1)
    col = jnp.broadcast_to(jnp.arange(GRID_W, dtype=F32)[None, :], (rows, GRID_W)).reshape(-1)
    half = MLA_ROPE // 2
    inv = 1.0 / (ROPE_BASE ** (jnp.arange(0, half, 2, dtype=F32) / half))
    cr, sr = jnp.cos(row[:, None] * inv), jnp.sin(row[:, None] * inv)
    cc, sc = jnp.cos(col[:, None] * inv), jnp.sin(col[:, None] * inv)
    cos64 = jnp.concatenate([cr, cr, cc, cc], axis=-1)
    sin64 = jnp.concatenate([-sr, sr, -sc, sc], axis=-1)
    return (jnp.tile(cos64, (1, MLA_HEADS)), jnp.tile(sin64, (1, MLA_HEADS)), cos64.T, sin64.T)


def _identity_rope_tables(n_tok):
    one = jnp.ones((n_tok, MLA_ROPE), F32)
    zero = jnp.zeros((n_tok, MLA_ROPE), F32)
    return (jnp.tile(one, (1, MLA_HEADS)), jnp.tile(zero, (1, MLA_HEADS)), one.T, zero.T)


def _even_weights(w_in, w_gate_f, b_gate_f, w_gate_b, b_gate_b, q_norm, w_uq, kv_norm, w_ukv):
    nqk = GLA_HEADS * GLA_DK
    nv = GLA_HEADS * GLA_DV
    o_z = 2 * nqk + nv
    o_r = o_z + 2 * GLA_RANK
    o_cq = o_r + nv
    o_kv = o_cq + MLA_Q_RANK
    o_kr = o_kv + MLA_KV_RANK
    d = w_in.shape[0]
    wp = jnp.concatenate([w_in[:, :o_z], w_in[:, o_r:o_kr], w_in[:, o_z:o_r],
                          jnp.zeros((d, LANES - 2 * GLA_RANK), F32)], axis=1).astype(BF16)
    kr = w_in[:, o_kr:o_kr + MLA_ROPE]
    wkr = jnp.concatenate([kr, kr[:, _ROPE_SWAP]], axis=1).T.astype(BF16)
    wg = jnp.zeros((LANES, 2 * nqk), F32)
    wg = wg.at[0:GLA_RANK, 0:nqk].set(w_gate_f).at[GLA_RANK:2 * GLA_RANK, nqk:].set(w_gate_b)
    wg_hi = wg.astype(BF16)
    wg_lo = (wg - wg_hi.astype(F32)).astype(BF16)
    bg = jnp.concatenate([b_gate_f, b_gate_b])[None, :]
    hq = np.arange(MLA_HEADS)[:, None] * MLA_QK
    nope_idx = (hq + np.arange(MLA_NOPE)[None, :]).reshape(-1)
    rope_idx = (hq + MLA_NOPE + np.arange(MLA_ROPE)[None, :]).reshape(-1)
    swap_idx = (hq + MLA_NOPE + _ROPE_SWAP[None, :]).reshape(-1)
    wuq = w_uq[:, np.concatenate([nope_idx, rope_idx, swap_idx])].astype(BF16)
    hk = np.arange(MLA_HEADS)[:, None] * (MLA_NOPE + MLA_V)
    k_idx = (hk + np.arange(MLA_NOPE)[None, :]).reshape(-1)
    v_idx = (hk + MLA_NOPE + np.arange(MLA_V)[None, :]).reshape(-1)
    return dict(wp=wp, wkr=wkr, wg_hi=wg_hi, wg_lo=wg_lo, bg=bg, qn=q_norm[None, :], wuq=wuq,
                kvn=kv_norm[None, :], wuk=w_ukv[:, k_idx].T.astype(BF16), wuv=w_ukv[:, v_idx].astype(BF16))


def _mods(m, rows, batch):
    d = D_MODEL
    if rows is None:
        return [jnp.broadcast_to(m[batch, k * d:(k + 1) * d][None, None, :], (batch, 1, d)) for k in range(6)]
    return [m[:batch, k * d:(k + 1) * d][:, None, :] for k in range(6)]


def kernel(x, c, ctx, c_ctx, w_mod, b_mod, g_mix_pre, g_mix_post, g_ffn_pre, g_ffn_post, e_w_in, e_w_gate_f, e_b_gate_f, e_w_gate_b, e_b_gate_b, e_gla_norm, e_q_norm, e_w_uq, e_kv_norm, e_w_ukv, e_w_o, e_w1, e_w3, e_w2, o_w_o, o_w_router, o_w1, o_w3, o_w2):
    batch, seq, d = x.shape
    n_ctx = ctx.shape[1]
    cond = jnp.zeros((16, d), F32).at[:batch].set(c).at[batch].set(c_ctx)
    mods = _adaln(cond, w_mod, b_mod)
    rope_x = _rope_tables(seq)
    rope_c = _identity_rope_tables(n_ctx)
    nqk = GLA_HEADS * GLA_DK
    last_read = 2 * ((DEPTH - 1) // 2)
    ow1 = o_w1.astype(BF16).reshape((-1,) + o_w1.shape[2:])
    ow3 = o_w3.astype(BF16).reshape((-1,) + o_w3.shape[2:])
    ow2 = o_w2.astype(BF16).reshape((-1,) + o_w2.shape[2:])
    xs, xc = x, ctx
    for i in range(DEPTH):
        j = i // 2
        ctx_live = i <= last_read
        ctx_full = i < last_read
        mx = _mods(mods[i], 0, batch)
        mc = _mods(mods[i], None, batch)
        gpre, gpost = g_mix_pre[i][None, :], g_mix_post[i][None, :]
        fpre, fpost = g_ffn_pre[i][None, :], g_ffn_post[i][None, :]
        if i % 2 == 0:
            w = _even_weights(e_w_in[j], e_w_gate_f[j], e_b_gate_f[j], e_w_gate_b[j], e_b_gate_b[j],
                              e_q_norm[j], e_w_uq[j], e_kv_norm[j], e_w_ukv[j])
            wo1 = e_w_o[j][:GLA_HEADS * GLA_DV].astype(BF16)
            wo2 = e_w_o[j][GLA_HEADS * GLA_DV:].astype(BF16)
            gn = e_gla_norm[j][None, :]
            w1, w3, w2 = e_w1[j].astype(BF16), e_w3[j].astype(BF16), e_w2[j].astype(BF16)
            zero_state = jnp.zeros((batch, GLA_DV, nqk), F32)
            if ctx_live:
                cgq, cgk, cgv, cgr, claf, clab, cq, ckt, cv = _in_proj(xc, gpre, mc[0], mc[1], w, rope_c, n_ctx)
                co_f, co_b, s_f, s_b = _gla(cgq, cgk, cgv, claf, clab, zero_state, zero_state, n_ctx)
                srcs_c = [(ckt, cv)]
            else:
                s_f = s_b = zero_state
                srcs_c = []
            gq, gk, gv, gr, laf, lab, q, kt, v = _in_proj(xs, gpre, mx[0], mx[1], w, rope_x, min(512, seq))
            o_f, o_b, _, _ = _gla(gq, gk, gv, laf, lab, s_f, s_b, min(256, seq))
            a = _attention(q, [(kt, v)] + srcs_c, min(1024, seq))
            xs = _even_tail(xs, o_f, o_b, gr, a, gn, wo1, wo2, gpost, mx[2], fpre, mx[3], mx[4],
                            w1, w3, w2, fpost, mx[5], min(512, seq))
            if ctx_full:
                ac = _attention(cq, srcs_c, n_ctx)
                xc = _even_tail(xc, co_f, co_b, cgr, ac, gn, wo1, wo2, gpost, mc[2], fpre, mc[3], mc[4],
                                w1, w3, w2, fpost, mc[5], n_ctx)
        else:
            wo = o_w_o[j].astype(BF16)
            wr = jnp.zeros((d, LANES), F32).at[:, :N_EXPERTS].set(o_w_router[j])
            wr_hi = wr.astype(BF16)
            wr_lo = (wr - wr_hi.astype(F32)).astype(BF16)
            base = j * N_EXPERTS
            xs = _fourier_x(xs, gpre, mx[0], mx[1], wo, gpost, mx[2])
            xs = _moe_sorted(xs, fpre, mx[3], mx[4], wr_hi, wr_lo, ow1, ow3, ow2, base, fpost, mx[5],
                             min(512, seq), min(1024, seq), 1792)
            if ctx_full:
                xc = _fourier_ctx(xc, gpre, mc[0], mc[1], wo, gpost, mc[2])
                xc = _moe_sorted(xc, fpre, mc[3], mc[4], wr_hi, wr_lo, ow1, ow3, ow2, base, fpost, mc[5],
                                 n_ctx, n_ctx, 1792)
    return xs
```

```python
import functools

import numpy as np
import jax
import jax.numpy as jnp
from jax import lax
from jax.experimental import pallas as pl
from jax.experimental.pallas import tpu as pltpu
from jax.experimental.pallas import tpu_sc as plsc

F32 = jnp.float32
BF16 = jnp.bfloat16

EPS = 1e-6
D_MODEL = 1024
DEPTH = 4
GRID_W = 64
GLA_HEADS = 4
GLA_DK = 64
GLA_DV = 128
GLA_RANK = 16
GLA_TAU = 16.0
GLA_CHUNK = 64
MLA_HEADS = 4
MLA_Q_RANK = 256
MLA_KV_RANK = 128
MLA_NOPE = 128
MLA_ROPE = 64
MLA_V = 128
MLA_QK = MLA_NOPE + MLA_ROPE
ROPE_BASE = 10000.0
FOURIER_GROUPS = 4
FOURIER_GW = D_MODEL // FOURIER_GROUPS
N_EXPERTS = 8
LANES = 128
VMEM_LIMIT = 48 * 1024 * 1024


def _cparams(sem):
    return pltpu.CompilerParams(dimension_semantics=sem, vmem_limit_bytes=VMEM_LIMIT)


def _dot(a, b):
    return jnp.dot(a, b, preferred_element_type=F32)


def _dot_nt(a, b):
    return lax.dot_general(a, b, (((1,), (1,)), ((), ())), preferred_element_type=F32)


def _dot_tn(a, b):
    return lax.dot_general(a, b, (((0,), (0,)), ((), ())), preferred_element_type=F32)


def _split(x):
    hi = x.astype(BF16)
    lo = (x - hi.astype(F32)).astype(BF16)
    return hi, lo


def _dot3(a, b_hi, b_lo):
    a_hi, a_lo = _split(a)
    return _dot(a_hi, b_hi) + _dot(a_lo, b_hi) + _dot(a_hi, b_lo)


def _rms(x, g):
    return x * lax.rsqrt(jnp.mean(x * x, axis=-1, keepdims=True) + EPS) * g


def _silu(x):
    return x / (1.0 + jnp.exp(-x))


def _const_spec(shape):
    nd = len(shape)
    return pl.BlockSpec(shape, lambda *_: (0,) * nd)


def _batch_row_spec(d):
    return pl.BlockSpec((1, 1, d), lambda b, *_: (b, 0, 0))


def _adaln_kernel(c_ref, w_ref, b_ref, o_ref):
    a = _silu(c_ref[...])
    w_hi, w_lo = _split(w_ref[0])
    o_ref[0] = _dot3(a, w_hi, w_lo) + b_ref[0]


def _adaln(cond, w_mod, b_mod):
    depth, d, n = w_mod.shape
    rows = cond.shape[0]
    bn = 1536
    return pl.pallas_call(
        _adaln_kernel,
        grid=(depth, n // bn),
        in_specs=[
            pl.BlockSpec((rows, d), lambda i, j: (0, 0)),
            pl.BlockSpec((1, d, bn), lambda i, j: (i, 0, j)),
            pl.BlockSpec((1, 1, bn), lambda i, j: (i, 0, j)),
        ],
        out_specs=pl.BlockSpec((1, rows, bn), lambda i, j: (i, 0, j)),
        out_shape=jax.ShapeDtypeStruct((depth, rows, n), F32),
        compiler_params=_cparams(("arbitrary", "arbitrary")),
        name="adaln",
    )(cond, w_mod, b_mod.reshape(depth, 1, n))


def _in_proj_kernel(x_ref, g_ref, sh_ref, sc_ref, wp_ref, wkr_ref, wgh_ref, wgl_ref, bg_ref,
                    qn_ref, wuq_ref, kvn_ref, wuk_ref, wuv_ref, cq_ref, sq_ref, ck_ref, sk_ref,
                    gq_ref, gk_ref, gv_ref, gr_ref, laf_ref, lab_ref, q_ref, kt_ref, v_ref):
    h = _rms(x_ref[0], g_ref[...]) * (1.0 + sc_ref[0]) + sh_ref[0]
    hb = h.astype(BF16)
    p = _dot(hb, wp_ref[...])
    nqk = GLA_HEADS * GLA_DK
    nv = GLA_HEADS * GLA_DV
    gq_ref[0] = p[:, 0:nqk] * (GLA_DK ** -0.5)
    gk_ref[0] = p[:, nqk:2 * nqk]
    gv_ref[0] = p[:, 2 * nqk:2 * nqk + nv]
    gr_ref[0] = p[:, 2 * nqk + nv:2 * nqk + 2 * nv]
    o = 2 * nqk + 2 * nv
    cq = p[:, o:o + MLA_Q_RANK]
    ckv = p[:, o + MLA_Q_RANK:o + MLA_Q_RANK + MLA_KV_RANK]
    tail = p[:, o + MLA_Q_RANK + MLA_KV_RANK:]
    pre = _dot3(tail, wgh_ref[...], wgl_ref[...]) + bg_ref[...]
    la = (jnp.minimum(pre, 0.0) - jnp.log(1.0 + jnp.exp(-jnp.abs(pre)))) * (1.0 / GLA_TAU)
    laf_ref[0] = la[:, :nqk]
    lab_ref[0] = la[:, nqk:]
    q = _dot(_rms(cq, qn_ref[...]).astype(BF16), wuq_ref[...])
    att_scale = MLA_QK ** -0.5
    nn = MLA_HEADS * MLA_NOPE
    nr = MLA_HEADS * MLA_ROPE
    q_rope = q[:, nn:nn + nr] * cq_ref[...] + q[:, nn + nr:] * sq_ref[...]
    for hd in range(MLA_HEADS):
        q_ref[0, hd, :, 0:MLA_NOPE] = (q[:, hd * MLA_NOPE:(hd + 1) * MLA_NOPE] * att_scale).astype(BF16)
        q_ref[0, hd, :, MLA_NOPE:MLA_QK] = (q_rope[:, hd * MLA_ROPE:(hd + 1) * MLA_ROPE] * att_scale).astype(BF16)
    ckvn = _rms(ckv, kvn_ref[...]).astype(BF16)
    kt = _dot_nt(wuk_ref[...], ckvn)
    v = _dot(ckvn, wuv_ref[...])
    kr2 = _dot_nt(wkr_ref[...], hb)
    kr = (kr2[:MLA_ROPE] * ck_ref[...] + kr2[MLA_ROPE:] * sk_ref[...]).astype(BF16)
    for hd in range(MLA_HEADS):
        kt_ref[0, hd, 0:MLA_NOPE, :] = kt[hd * MLA_NOPE:(hd + 1) * MLA_NOPE].astype(BF16)
        kt_ref[0, hd, MLA_NOPE:MLA_QK, :] = kr
        v_ref[0, hd] = v[:, hd * MLA_V:(hd + 1) * MLA_V].astype(BF16)


def _in_proj(tok, g, sh, sc, w, tabs, tm):
    b, t, d = tok.shape
    cq, sq, ck, sk = tabs
    nqk = GLA_HEADS * GLA_DK
    nv = GLA_HEADS * GLA_DV
    row = lambda n: pl.BlockSpec((1, tm, n), lambda bi, ti: (bi, ti, 0))
    weights = (w["wp"], w["wkr"], w["wg_hi"], w["wg_lo"], w["bg"], w["qn"], w["wuq"], w["kvn"],
               w["wuk"], w["wuv"])
    return pl.pallas_call(
        _in_proj_kernel,
        grid=(b, t // tm),
        in_specs=[row(d), _const_spec((1, d)), _batch_row_spec(d), _batch_row_spec(d)]
        + [_const_spec(a.shape) for a in weights]
        + [pl.BlockSpec((tm, MLA_HEADS * MLA_ROPE), lambda bi, ti: (ti, 0)),
           pl.BlockSpec((tm, MLA_HEADS * MLA_ROPE), lambda bi, ti: (ti, 0)),
           pl.BlockSpec((MLA_ROPE, tm), lambda bi, ti: (0, ti)),
           pl.BlockSpec((MLA_ROPE, tm), lambda bi, ti: (0, ti))],
        out_specs=[row(nqk), row(nqk), row(nv), row(nv), row(nqk), row(nqk),
                   pl.BlockSpec((1, MLA_HEADS, tm, MLA_QK), lambda bi, ti: (bi, 0, ti, 0)),
                   pl.BlockSpec((1, MLA_HEADS, MLA_QK, tm), lambda bi, ti: (bi, 0, 0, ti)),
                   pl.BlockSpec((1, MLA_HEADS, tm, MLA_V), lambda bi, ti: (bi, 0, ti, 0))],
        out_shape=[jax.ShapeDtypeStruct((b, t, nqk), F32), jax.ShapeDtypeStruct((b, t, nqk), F32),
                   jax.ShapeDtypeStruct((b, t, nv), F32), jax.ShapeDtypeStruct((b, t, nv), F32),
                   jax.ShapeDtypeStruct((b, t, nqk), F32), jax.ShapeDtypeStruct((b, t, nqk), F32),
                   jax.ShapeDtypeStruct((b, MLA_HEADS, t, MLA_QK), BF16),
                   jax.ShapeDtypeStruct((b, MLA_HEADS, MLA_QK, t), BF16),
                   jax.ShapeDtypeStruct((b, MLA_HEADS, t, MLA_V), BF16)],
        compiler_params=_cparams(("parallel", "parallel")),
        name="even_in_proj",
    )(tok, g, sh, sc, *weights, cq, sq, ck, sk)


def _gla_chunks(streams):
    c = GLA_CHUNK
    heads = range(GLA_HEADS)
    ks = [slice(hd * GLA_DK, (hd + 1) * GLA_DK) for hd in heads]
    vs = [slice(hd * GLA_DV, (hd + 1) * GLA_DV) for hd in heads]
    bcs = []
    for (_, _, _, l_ref, _, _, g, row0, tri, _, _) in streams:
        la_hi, la_lo = _split(l_ref[g, pl.ds(row0, c), :])
        bcs.append(_dot(tri, la_hi) + _dot(tri, la_lo))
    ops = []
    for (q_ref, k_ref, v_ref, _, _, st_ref, g, row0, _, _, last_row), bc in zip(streams, bcs):
        bl = bc[last_row:last_row + 1, :]
        q = q_ref[g, pl.ds(row0, c), :]
        k = k_ref[g, pl.ds(row0, c), :]
        st = st_ref[g]
        ops.append(dict(qc=(q * jnp.exp(bc)).astype(BF16), kc=(k * jnp.exp(-bc)).astype(BF16),
                        kd=(k * jnp.exp(bl - bc)).astype(BF16), vb=v_ref[g, pl.ds(row0, c), :].astype(BF16),
                        st=st, stb=st.astype(BF16), decay=jnp.exp(bl)))
    atts = [[_dot_nt(o["qc"][:, ks[hd]], o["kc"][:, ks[hd]]) for hd in heads] for o in ops]
    inters = [[_dot_nt(o["qc"][:, ks[hd]], o["stb"][:, ks[hd]]) for hd in heads] for o in ops]
    upds = [[_dot_tn(o["vb"][:, vs[hd]], o["kd"][:, ks[hd]]) for hd in heads] for o in ops]
    intras = [[_dot(jnp.where(s[9], att[hd], 0.0).astype(BF16), o["vb"][:, vs[hd]]) for hd in heads]
              for s, o, att in zip(streams, ops, atts)]
    for s, o, intra, inter, upd in zip(streams, ops, intras, inters, upds):
        o_ref, st_ref, g, row0 = s[4], s[5], s[6], s[7]
        o_ref[g, pl.ds(row0, c), :] = jnp.concatenate([a + b for a, b in zip(intra, inter)], axis=1)
        st_ref[g] = o["st"] * o["decay"] + jnp.concatenate(upd, axis=1)


def _gla_kernel(qf, kf, vf, lf, qb, kb, vb, lb, s0f, s0b, of, ob, sff, sfb, stf, stb, *, nc, gb):
    j = pl.program_id(1)

    @pl.when(j == 0)
    def _():
        stf[...] = s0f[...]
        stb[...] = s0b[...]

    c = GLA_CHUNK
    r = lax.broadcasted_iota(jnp.int32, (c, c), 0)
    cc = lax.broadcasted_iota(jnp.int32, (c, c), 1)
    lower = r >= cc
    upper = r <= cc
    tri_l = jnp.where(lower, 1.0, 0.0).astype(BF16)
    tri_u = jnp.where(upper, 1.0, 0.0).astype(BF16)

    for ci in range(nc):
        streams = []
        for g in range(gb):
            streams.append((qf, kf, vf, lf, of, stf, g, ci * c, tri_l, lower, c - 1))
            streams.append((qb, kb, vb, lb, ob, stb, g, (nc - 1 - ci) * c, tri_u, upper, 0))
        _gla_chunks(streams)

    @pl.when(j == pl.num_programs(1) - 1)
    def _():
        sff[...] = stf[...]
        sfb[...] = stb[...]


GLA_BATCH_ROWS = 8


def _gla(gq, gk, gv, laf, lab, s0f, s0b, tb):
    b, t, nqk = gq.shape
    nv = gv.shape[-1]
    nblk = t // tb
    gb = min(GLA_BATCH_ROWS, b)
    fwd = lambda n: pl.BlockSpec((gb, tb, n), lambda bi, j: (bi, j, 0))
    bwd = lambda n: pl.BlockSpec((gb, tb, n), lambda bi, j: (bi, nblk - 1 - j, 0))
    st = pl.BlockSpec((gb, GLA_DV, nqk), lambda bi, j: (bi, 0, 0))
    return pl.pallas_call(
        functools.partial(_gla_kernel, nc=tb // GLA_CHUNK, gb=gb),
        grid=(b // gb, nblk),
        in_specs=[fwd(nqk), fwd(nqk), fwd(nv), fwd(nqk), bwd(nqk), bwd(nqk), bwd(nv), bwd(nqk), st, st],
        out_specs=[fwd(nv), bwd(nv), st, st],
        out_shape=[jax.ShapeDtypeStruct((b, t, nv), F32), jax.ShapeDtypeStruct((b, t, nv), F32),
                   jax.ShapeDtypeStruct((b, GLA_DV, nqk), F32), jax.ShapeDtypeStruct((b, GLA_DV, nqk), F32)],
        scratch_shapes=[pltpu.VMEM((gb, GLA_DV, nqk), F32), pltpu.VMEM((gb, GLA_DV, nqk), F32)],
        compiler_params=_cparams(("parallel", "arbitrary")),
        name="gla_scan",
    )(gq, gk, gv, laf, gq, gk, gv, lab, s0f, s0b)


ATTN_KEY_CHUNK = 1024


def _attn_kernel(*refs, n_src):
    q_ref = refs[0]
    kts = refs[1:1 + 2 * n_src:2]
    vs = refs[2:2 + 2 * n_src:2]
    o_ref = refs[1 + 2 * n_src]
    vexts = refs[2 + 2 * n_src:]

    @pl.when(pl.program_id(2) == 0)
    def _():
        for v, vext in zip(vs, vexts):
            tk = v.shape[2]
            lane = lax.broadcasted_iota(jnp.int32, (tk, LANES), 1)
            vext[:, 0:MLA_V] = v[0, 0]
            vext[:, MLA_V:MLA_V + LANES] = jnp.where(lane == 0, 1.0, 0.0).astype(BF16)

    q = q_ref[0, 0]
    bq = q.shape[0]
    m = jnp.full((bq, 1), -jnp.inf, F32)
    acc = jnp.zeros((bq, MLA_V + LANES), F32)
    chunks = []
    for kt, vext in zip(kts, vexts):
        tk = kt.shape[3]
        ck = min(ATTN_KEY_CHUNK, tk)
        chunks += [(kt, vext, c * ck, (c + 1) * ck) for c in range(tk // ck)]
    s_next = _dot(q, chunks[0][0][0, 0, :, chunks[0][2]:chunks[0][3]])
    for i, (kt, vext, lo, hi) in enumerate(chunks):
        s = s_next
        if i + 1 < len(chunks):
            kt_n, _, lo_n, hi_n = chunks[i + 1]
            s_next = _dot(q, kt_n[0, 0, :, lo_n:hi_n])
        m_new = jnp.maximum(m, s.max(axis=-1, keepdims=True))
        p = jnp.exp((s - m_new).astype(BF16))
        acc = acc * jnp.exp(m - m_new) + _dot(p, vext[lo:hi, :])
        m = m_new
    o_ref[0] = (acc[:, 0:MLA_V] / acc[:, MLA_V:MLA_V + 1]).astype(o_ref.dtype)


def _attention(q, srcs, bq):
    b, nh, t, dqk = q.shape
    in_specs = [pl.BlockSpec((1, 1, bq, dqk), lambda bi, hi, qi: (bi, hi, qi, 0))]
    args = [q]
    scratch = []
    for kt, v in srcs:
        tk = kt.shape[-1]
        in_specs.append(pl.BlockSpec((1, 1, dqk, tk), lambda bi, hi, qi: (bi, hi, 0, 0)))
        in_specs.append(pl.BlockSpec((1, 1, tk, MLA_V), lambda bi, hi, qi: (bi, hi, 0, 0)))
        args += [kt, v]
        scratch.append(pltpu.VMEM((tk, MLA_V + LANES), BF16))
    return pl.pallas_call(
        functools.partial(_attn_kernel, n_src=len(srcs)),
        grid=(b, nh, t // bq),
        in_specs=in_specs,
        out_specs=pl.BlockSpec((1, bq, MLA_V), lambda bi, hi, qi: (bi, qi, hi)),
        out_shape=jax.ShapeDtypeStruct((b, t, nh * MLA_V), BF16),
        scratch_shapes=scratch,
        compiler_params=_cparams(("parallel", "parallel", "arbitrary")),
        name="mla_attention",
    )(*args)


MXU_COLS = 256


def _ff_chunks(dff, max_tiles):
    n_tiles = dff // MXU_COLS
    n_chunks = -(-n_tiles // max_tiles)
    base, extra = divmod(n_tiles, n_chunks)
    bounds, start = [], 0
    for i in range(n_chunks):
        width = (base + (1 if i < extra else 0)) * MXU_COLS
        bounds.append((start, start + width))
        start += width
    return bounds


def _even_tail_kernel(x_ref, of_ref, ob_ref, gr_ref, a_ref, gn_ref, wo1_ref, wo2_ref, gpost_ref, ga_ref,
                      gpre_ref, sh_ref, sc_ref, w1_ref, w3_ref, w2_ref, fpost_ref, gaf_ref, xo_ref, *, chunks):
    o = of_ref[0] + ob_ref[0]
    parts = [_rms(o[:, hd * GLA_DV:(hd + 1) * GLA_DV], gn_ref[...]) for hd in range(GLA_HEADS)]
    fin = jnp.concatenate(parts, axis=1) * _silu(gr_ref[0])
    y = _dot(fin.astype(BF16), wo1_ref[...]) + _dot(a_ref[0], wo2_ref[...])
    xn = x_ref[0] + ga_ref[0] * _rms(y, gpost_ref[...])
    h = (_rms(xn, gpre_ref[...]) * (1.0 + sc_ref[0]) + sh_ref[0]).astype(BF16)
    z = None
    for lo, hi in chunks:
        a = _dot(h, w1_ref[:, lo:hi])
        mid = (_silu(a) * _dot(h, w3_ref[:, lo:hi])).astype(BF16)
        part = _dot(mid, w2_ref[lo:hi, :])
        z = part if z is None else z + part
    xo_ref[0] = xn + gaf_ref[0] * _rms(z, fpost_ref[...])


def _even_tail(tok, o_f, o_b, gr, a, gn, wo1, wo2, gpost, ga, gpre, sh, sc, w1, w3, w2, fpost, gaf, tm):
    b, t, d = tok.shape
    nv = o_f.shape[-1]
    dff = w1.shape[-1]
    row = lambda n: pl.BlockSpec((1, tm, n), lambda bi, ti: (bi, ti, 0))
    resident = lambda shape: pl.BlockSpec(shape, lambda bi, ti: (0, 0), pipeline_mode=pl.Buffered(1))
    return pl.pallas_call(
        functools.partial(_even_tail_kernel, chunks=_ff_chunks(dff, 6)),
        grid=(b, t // tm),
        in_specs=[row(d), row(nv), row(nv), row(nv), row(a.shape[-1]), _const_spec(gn.shape),
                  resident(wo1.shape), resident(wo2.shape), _const_spec((1, d)), _batch_row_spec(d),
                  _const_spec((1, d)), _batch_row_spec(d), _batch_row_spec(d),
                  resident((d, dff)), resident((d, dff)), resident((dff, d)),
                  _const_spec((1, d)), _batch_row_spec(d)],
        out_specs=row(d),
        out_shape=jax.ShapeDtypeStruct((b, t, d), F32),
        compiler_params=_cparams(("parallel", "parallel")),
        name="even_tail",
    )(tok, o_f, o_b, gr, a, gn, wo1, wo2, gpost, ga, gpre, sh, sc, w1, w3, w2, fpost, gaf)


def _dft_cos_sin(n):
    idx = (np.arange(n)[:, None] * np.arange(n)[None, :]) % n
    ang = 2.0 * np.pi * idx.astype(np.float64) / n
    return np.cos(ang), np.sin(ang)


def _fourier1_kernel(x_ref, g_ref, sh_ref, sc_ref, f1_ref, o_ref, *, nb, n1):
    xt = jnp.swapaxes(x_ref[0], 0, 1)
    for j in range(nb):
        hb = (_rms(xt[j], g_ref[...]) * (1.0 + sc_ref[0]) + sh_ref[0]).astype(BF16)
        a = _dot(f1_ref[j], hb)
        o_ref[0, 0, j] = a[:n1].astype(o_ref.dtype)
        o_ref[0, 1, j] = a[n1:].astype(o_ref.dtype)


def _fourier2_kernel(b_ref, x_ref, f2_ref, cc_ref, sc_ref, wo_ref, gpost_ref, ga_ref, xo_ref, *, kb, n2):
    d = b_ref.shape[-1]
    br = jnp.swapaxes(b_ref[0, 0].astype(F32), 0, 1)
    bi = jnp.swapaxes(b_ref[0, 1].astype(F32), 0, 1)
    urs, uis = [], []
    for j in range(kb):
        u = _dot(f2_ref[...], jnp.concatenate([br[j], bi[j]], axis=0).astype(BF16))
        urs.append(u[:n2])
        uis.append(u[n2:])
    ur = jnp.concatenate(urs, axis=0).astype(BF16)
    ui = jnp.concatenate(uis, axis=0).astype(BF16)
    gw = FOURIER_GW
    f = jnp.concatenate(
        [_dot(ur[:, g * gw:(g + 1) * gw], cc_ref[...]) + _dot(ui[:, g * gw:(g + 1) * gw], sc_ref[...])
         for g in range(FOURIER_GROUPS)], axis=1)
    y = _dot(f.astype(BF16), wo_ref[...])
    yn = ga_ref[0] * _rms(y, gpost_ref[...])
    xo_ref[0] = x_ref[0] + jnp.swapaxes(yn.reshape(kb, n2, d), 0, 1)


def _fourier_x(tok, g, sh, sc, wo, gpost, ga):
    b, t, d = tok.shape
    n1, n2 = 128, t // 128
    nb, kb = min(8, n2), 16
    c2, s2 = _dft_cos_sin(n2)
    cg, sg = _dft_cos_sin(FOURIER_GW)
    tok_idx = np.arange(n2)[:, None, None] + n2 * np.arange(n1)[None, None, :]
    ang = 2.0 * np.pi * ((np.arange(n1)[None, :, None] * tok_idx) % t).astype(np.float64) / t
    f1 = jnp.asarray(np.concatenate([np.cos(ang), -np.sin(ang)], axis=1) / np.sqrt(n1), BF16)
    f2 = jnp.asarray(np.block([[c2, s2], [-s2, c2]]) / np.sqrt(n2), BF16)
    ccg = jnp.asarray(cg / np.sqrt(FOURIER_GW), BF16)
    scg = jnp.asarray(sg / np.sqrt(FOURIER_GW), BF16)
    stage1 = pl.pallas_call(
        functools.partial(_fourier1_kernel, nb=nb, n1=n1),
        grid=(b, n2 // nb),
        in_specs=[pl.BlockSpec((1, n1, nb, d), lambda bi, ji: (bi, 0, ji, 0)),
                  _const_spec((1, d)), _batch_row_spec(d), _batch_row_spec(d),
                  pl.BlockSpec((nb, 2 * n1, n1), lambda bi, ji: (ji, 0, 0))],
        out_specs=pl.BlockSpec((1, 2, nb, n1, d), lambda bi, ji: (bi, 0, ji, 0, 0)),
        out_shape=jax.ShapeDtypeStruct((b, 2, n2, n1, d), BF16),
        compiler_params=_cparams(("parallel", "parallel")),
        name="fourier_stage1",
    )(tok.reshape(b, n1, n2, d), g, sh, sc, f1)
    freq = lambda n: pl.BlockSpec((1, n2, kb, n), lambda bi, ki: (bi, 0, ki, 0))
    out = pl.pallas_call(
        functools.partial(_fourier2_kernel, kb=kb, n2=n2),
        grid=(b, n1 // kb),
        in_specs=[pl.BlockSpec((1, 2, n2, kb, d), lambda bi, ki: (bi, 0, 0, ki, 0)),
                  freq(d),
                  _const_spec(f2.shape), _const_spec(ccg.shape), _const_spec(scg.shape),
                  _const_spec(wo.shape), _const_spec((1, d)), _batch_row_spec(d)],
        out_specs=freq(d),
        out_shape=jax.ShapeDtypeStruct((b, n2, n1, d), F32),
        compiler_params=_cparams(("parallel", "parallel")),
        name="fourier_stage2",
    )(stage1, tok.reshape(b, n2, n1, d), f2, ccg, scg, wo, gpost, ga)
    return out.reshape(b, t, d)


def _fourier_ctx_kernel(x_ref, g_ref, sh_ref, sc_ref, fl_ref, cc_ref, scg_ref, wo_ref, gpost_ref, ga_ref,
                        xo_ref):
    x = x_ref[0]
    t = x.shape[0]
    h = _rms(x, g_ref[...]) * (1.0 + sc_ref[0]) + sh_ref[0]
    u = _dot(fl_ref[...], h.astype(BF16))
    ur = u[:t].astype(BF16)
    ui = u[t:].astype(BF16)
    gw = FOURIER_GW
    f = jnp.concatenate(
        [_dot(ur[:, g * gw:(g + 1) * gw], cc_ref[...]) + _dot(ui[:, g * gw:(g + 1) * gw], scg_ref[...])
         for g in range(FOURIER_GROUPS)], axis=1)
    y = _dot(f.astype(BF16), wo_ref[...])
    xo_ref[0] = x + ga_ref[0] * _rms(y, gpost_ref[...])


def _fourier_ctx(tok, g, sh, sc, wo, gpost, ga):
    b, t, d = tok.shape
    cl, sl = _dft_cos_sin(t)
    cg, sg = _dft_cos_sin(FOURIER_GW)
    fl = jnp.asarray(np.concatenate([cl, -sl], axis=0) / np.sqrt(t), BF16)
    ccg = jnp.asarray(cg / np.sqrt(FOURIER_GW), BF16)
    scg = jnp.asarray(sg / np.sqrt(FOURIER_GW), BF16)
    row = pl.BlockSpec((1, t, d), lambda bi: (bi, 0, 0))
    return pl.pallas_call(
        _fourier_ctx_kernel,
        grid=(b,),
        in_specs=[row, _const_spec((1, d)), _batch_row_spec(d), _batch_row_spec(d), _const_spec(fl.shape),
                  _const_spec(ccg.shape), _const_spec(scg.shape), _const_spec(wo.shape),
                  _const_spec((1, d)), _batch_row_spec(d)],
        out_specs=row,
        out_shape=jax.ShapeDtypeStruct((b, t, d), F32),
        compiler_params=_cparams(("parallel",)),
        name="fourier_ctx",
    )(tok, g, sh, sc, fl, ccg, scg, wo, gpost, ga)


PLANE_W = 256
N_PLANES = D_MODEL // (2 * PLANE_W)
SC_WINDOW = 128
ROUTE_META_ROWS = 8


def _pack_planes(h):
    out = []
    for p in range(N_PLANES):
        base = 2 * p * PLANE_W
        hi = pltpu.bitcast(h[:, base:base + PLANE_W].astype(BF16).astype(F32), jnp.uint32)
        lo = pltpu.bitcast(h[:, base + PLANE_W:base + 2 * PLANE_W].astype(BF16).astype(F32), jnp.uint32)
        out.append(hi | (lo >> 16))
    return out


def _unpack_planes(planes):
    cols = []
    for w in planes:
        cols.append(pltpu.bitcast(w & jnp.uint32(0xFFFF0000), F32))
        cols.append(pltpu.bitcast(w << 16, F32))
    return jnp.concatenate(cols, axis=1)


def _route_rows(h, wrh_ref, wrl_ref, carry_ref):
    logits = _dot3(h, wrh_ref[...], wrl_ref[...])
    tm = logits.shape[0]
    lane = lax.broadcasted_iota(jnp.int32, logits.shape, 1)
    neg = -jnp.inf
    l1 = jnp.where(lane < N_EXPERTS, logits, neg)
    m1 = l1.max(axis=-1, keepdims=True)
    i1 = jnp.where(l1 == m1, lane, LANES).min(axis=-1, keepdims=True)
    l2 = jnp.where(lane == i1, neg, l1)
    m2 = l2.max(axis=-1, keepdims=True)
    i2 = jnp.where(l2 == m2, lane, LANES).min(axis=-1, keepdims=True)
    e = jnp.exp(m2 - m1)
    g1 = 1.0 / (1.0 + e)
    g2 = e / (1.0 + e)
    sel = jnp.logical_or(lane == i1, lane == i2)
    cnt = jnp.where(sel, 1.0, 0.0)
    r = lax.broadcasted_iota(jnp.int32, (tm, tm), 0)
    c = lax.broadcasted_iota(jnp.int32, (tm, tm), 1)
    below = jnp.where(r > c, 1.0, 0.0).astype(BF16)
    before = _dot(below, cnt.astype(BF16)) + carry_ref[...]
    r1 = jnp.where(lane == i1, before, 0.0).sum(axis=-1, keepdims=True).astype(jnp.int32)
    r2 = jnp.where(lane == i2, before, 0.0).sum(axis=-1, keepdims=True).astype(jnp.int32)
    carry_ref[...] = carry_ref[...] + cnt.sum(axis=0, keepdims=True)
    im = jnp.where(lane == 0, i1, jnp.where(lane == 1, i2, jnp.where(lane == 2, r1, r2)))
    return im, jnp.where(lane == 0, g1, g2)


def _route_kernel(x_ref, g_ref, sh_ref, sc_ref, wrh_ref, wrl_ref, hp_ref, im_ref, gm_ref, cnt_ref, carry_ref):
    first = jnp.logical_and(pl.program_id(0) == 0, pl.program_id(1) == 0)

    @pl.when(first)
    def _():
        carry_ref[...] = jnp.zeros_like(carry_ref)

    h = _rms(x_ref[0], g_ref[...]) * (1.0 + sc_ref[0]) + sh_ref[0]
    for p, w in enumerate(_pack_planes(h)):
        hp_ref[p, 0] = w
    im, gm_ref[0] = _route_rows(h, wrh_ref, wrl_ref, carry_ref)
    im_ref[0] = jnp.transpose(im)[0:ROUTE_META_ROWS, :]
    cnt_ref[...] = carry_ref[...]


def _route(tok, g, sh, sc, wr_hi, wr_lo, tm):
    b, t, d = tok.shape
    row = lambda n: pl.BlockSpec((1, tm, n), lambda bi, ti: (bi, ti, 0))
    return pl.pallas_call(
        _route_kernel,
        grid=(b, t // tm),
        in_specs=[row(d), _const_spec((1, d)), _batch_row_spec(d), _batch_row_spec(d),
                  _const_spec(wr_hi.shape), _const_spec(wr_lo.shape)],
        out_specs=[pl.BlockSpec((N_PLANES, 1, tm, PLANE_W), lambda bi, ti: (0, bi, ti, 0)),
                   pl.BlockSpec((1, ROUTE_META_ROWS, tm), lambda bi, ti: (bi, 0, ti)),
                   row(LANES), _const_spec((1, LANES))],
        out_shape=[jax.ShapeDtypeStruct((N_PLANES, b, t, PLANE_W), jnp.uint32),
                   jax.ShapeDtypeStruct((b, ROUTE_META_ROWS, t), jnp.int32),
                   jax.ShapeDtypeStruct((b, t, LANES), F32),
                   jax.ShapeDtypeStruct((1, LANES), F32)],
        scratch_shapes=[pltpu.VMEM((1, LANES), F32)],
        compiler_params=_cparams(("arbitrary", "arbitrary")),
        name="moe_route",
    )(tok, g, sh, sc, wr_hi, wr_lo)


def _sc_mesh():
    return plsc.VectorSubcoreMesh(core_axis_name="core", subcore_axis_name="subcore")


def _sc_gather_rows(table, idx):
    n = idx.shape[0]
    w = table.shape[1]

    @pl.kernel(out_type=jax.ShapeDtypeStruct((n, w), table.dtype), mesh=_sc_mesh())
    def gather(t_hbm, i_hbm, o_hbm):
        def body(i_vmem, o_vmem):
            pltpu.sync_copy(t_hbm.at[i_vmem.at[0]], o_vmem)

        pltpu.emit_pipeline(
            body, grid=(n // SC_WINDOW,),
            in_specs=[pl.BlockSpec((1, SC_WINDOW), index_map=lambda i: (0, i))],
            out_specs=[pl.BlockSpec((SC_WINDOW, w), index_map=lambda i: (i, 0))],
            core_axis_name=("core", "subcore"), dimension_semantics=(pltpu.PARALLEL,),
        )(i_hbm, o_hbm)

    return gather(table, idx.reshape(1, n))


def _sc_scatter_rows(src, idx, n_out):
    n = idx.shape[0]
    w = src.shape[1]
    n_src_windows = src.shape[0] // SC_WINDOW

    @pl.kernel(out_type=jax.ShapeDtypeStruct((n_out, w), src.dtype), mesh=_sc_mesh(), scratch_types=[])
    def scatter(s_hbm, i_hbm, o_hbm):
        def body(s_vmem, i_vmem):
            pltpu.sync_copy(s_vmem, o_hbm.at[i_vmem.at[0]])

        pltpu.emit_pipeline(
            body, grid=(n // SC_WINDOW,),
            in_specs=[pl.BlockSpec((SC_WINDOW, w), index_map=lambda i: (i % n_src_windows, 0)),
                      pl.BlockSpec((1, SC_WINDOW), index_map=lambda i: (0, i))],
            out_specs=[],
            core_axis_name=("core", "subcore"), dimension_semantics=(pltpu.PARALLEL,),
        )(s_hbm, i_hbm)

    return scatter(src, idx.reshape(1, n))


def _grouped_ffn_kernel(te_ref, nv_ref, xs_ref, w1_ref, w3_ref, w2_ref, y_ref, hb_ref, acc_ref):
    i = pl.program_id(0)
    f = pl.program_id(1)

    @pl.when(i < nv_ref[0])
    def _():
        @pl.when(f == 0)
        def _():
            hb_ref[...] = _unpack_planes([xs_ref[p] for p in range(N_PLANES)]).astype(BF16)

        half = hb_ref.shape[0] // 2
        for r in range(2):
            rows = slice(r * half, (r + 1) * half)
            h = hb_ref[rows, :]
            a = _dot(h, w1_ref[0])
            mid = (_silu(a) * _dot(h, w3_ref[0])).astype(BF16)
            contrib = _dot(mid, w2_ref[0])

            @pl.when(f == 0)
            def _():
                acc_ref[rows, :] = contrib

            @pl.when(f > 0)
            def _():
                acc_ref[rows, :] += contrib

        @pl.when(f == pl.num_programs(1) - 1)
        def _():
            for p, w in enumerate(_pack_planes(acc_ref[...])):
                y_ref[p] = w


def _grouped_ffn(xs, tile_expert, n_valid, w1, w3, w2, tm, fc):
    n_pad = xs.shape[1]
    ne, d, dff = w1.shape
    plane = pl.BlockSpec((N_PLANES, tm, PLANE_W), lambda i, f, te, nv: (0, i, 0))
    return pl.pallas_call(
        _grouped_ffn_kernel,
        grid_spec=pltpu.PrefetchScalarGridSpec(
            num_scalar_prefetch=2,
            grid=(n_pad // tm, dff // fc),
            in_specs=[plane,
                      pl.BlockSpec((1, d, fc), lambda i, f, te, nv: (te[i], 0, f)),
                      pl.BlockSpec((1, d, fc), lambda i, f, te, nv: (te[i], 0, f)),
                      pl.BlockSpec((1, fc, d), lambda i, f, te, nv: (te[i], f, 0))],
            out_specs=plane,
            scratch_shapes=[pltpu.VMEM((tm, d), BF16), pltpu.VMEM((tm, d), F32)]),
        out_shape=jax.ShapeDtypeStruct(xs.shape, jnp.uint32),
        compiler_params=_cparams(("arbitrary", "arbitrary")),
        name="moe_grouped_ffn",
    )(tile_expert, n_valid, xs, w1, w3, w2)


def _combine_kernel(x_ref, y_ref, gm_ref, g_ref, ga_ref, xo_ref):
    gm = gm_ref[0]
    y1 = _unpack_planes([y_ref[p, 0, 0] for p in range(N_PLANES)])
    y2 = _unpack_planes([y_ref[p, 1, 0] for p in range(N_PLANES)])
    mix = gm[:, 0:1] * y1 + gm[:, 1:2] * y2
    xo_ref[0] = x_ref[0] + ga_ref[0] * _rms(mix, g_ref[...])


def _combine(tok, yg, gm, g, ga, tm):
    b, t, d = tok.shape
    row = lambda n: pl.BlockSpec((1, tm, n), lambda bi, ti: (bi, ti, 0))
    return pl.pallas_call(
        _combine_kernel,
        grid=(b, t // tm),
        in_specs=[row(d),
                  pl.BlockSpec((N_PLANES, 2, 1, tm, PLANE_W), lambda bi, ti: (0, 0, bi, ti, 0)),
                  row(LANES), _const_spec((1, d)), _batch_row_spec(d)],
        out_specs=row(d),
        out_shape=jax.ShapeDtypeStruct((b, t, d), F32),
        compiler_params=_cparams(("parallel", "parallel")),
        name="moe_combine",
    )(tok, yg, gm, g, ga)


def _moe_sorted(tok, g_pre, sh, sc, wr_hi, wr_lo, w1, w3, w2, expert_base, g_post, ga, tm, tm_e, fc):
    b, t, d = tok.shape
    n_tok = b * t
    hp, im, gm, cnt = _route(tok, g_pre, sh, sc, wr_hi, wr_lo, tm)
    counts = cnt[0, :N_EXPERTS].astype(jnp.int32)
    padded = ((counts + tm_e - 1) // tm_e) * tm_e
    ends = jnp.cumsum(padded)
    starts = ends - padded
    n_pad = 2 * n_tok + N_EXPERTS * tm_e
    n_tiles = n_pad // tm_e
    tile_expert = jnp.minimum(
        jnp.sum((jnp.arange(n_tiles, dtype=jnp.int32)[:, None] * tm_e >= ends[None, :]).astype(jnp.int32), axis=1),
        N_EXPERTS - 1).astype(jnp.int32) + expert_base
    n_valid = (ends[-1:] // tm_e).astype(jnp.int32)
    e1, e2, r1, r2 = (im[:, k, :].reshape(n_tok) for k in range(4))
    pos = jnp.stack([starts[e1] + r1, starts[e2] + r2], axis=0)
    plane_off = (jnp.arange(N_PLANES, dtype=jnp.int32) * n_pad)
    idx_dispatch = (pos[:, None, :] + plane_off[None, :, None]).reshape(-1)
    idx_return = (pos[None, :, :] + plane_off[:, None, None]).reshape(-1)
    xs = _sc_scatter_rows(hp.reshape(N_PLANES * n_tok, PLANE_W), idx_dispatch, N_PLANES * n_pad)
    ys = _grouped_ffn(xs.reshape(N_PLANES, n_pad, PLANE_W), tile_expert, n_valid, w1, w3, w2, tm_e, fc)
    yg = _sc_gather_rows(ys.reshape(N_PLANES * n_pad, PLANE_W), idx_return)
    return _combine(tok, yg.reshape(N_PLANES, 2, b, t, PLANE_W), gm, g_post, ga, tm)


_ROPE_SWAP = np.concatenate([np.arange(16, 32), np.arange(0, 16), np.arange(48, 64), np.arange(32, 48)])


def _rope_tables(n_tok):
    rows = n_tok // GRID_W
    row = jnp.broadcast_to(jnp.arange(rows, dtype=F32)[:, None], (rows, GRID_W)).reshape(-1)
    col = jnp.broadcast_to(jnp.arange(GRID_W, dtype=F32)[None, :], (rows, GRID_W)).reshape(-1)
    half = MLA_ROPE // 2
    inv = 1.0 / (ROPE_BASE ** (jnp.arange(0, half, 2, dtype=F32) / half))
    cr, sr = jnp.cos(row[:, None] * inv), jnp.sin(row[:, None] * inv)
    cc, sc = jnp.cos(col[:, None] * inv), jnp.sin(col[:, None] * inv)
    cos64 = jnp.concatenate([cr, cr, cc, cc], axis=-1)
    sin64 = jnp.concatenate([-sr, sr, -sc, sc], axis=-1)
    return (jnp.tile(cos64, (1, MLA_HEADS)), jnp.tile(sin64, (1, MLA_HEADS)), cos64.T, sin64.T)


def _identity_rope_tables(n_tok):
    one = jnp.ones((n_tok, MLA_ROPE), F32)
    zero = jnp.zeros((n_tok, MLA_ROPE), F32)
    return (jnp.tile(one, (1, MLA_HEADS)), jnp.tile(zero, (1, MLA_HEADS)), one.T, zero.T)


def _even_weights(w_in, w_gate_f, b_gate_f, w_gate_b, b_gate_b, q_norm, w_uq, kv_norm, w_ukv):
    nqk = GLA_HEADS * GLA_DK
    nv = GLA_HEADS * GLA_DV
    o_z = 2 * nqk + nv
    o_r = o_z + 2 * GLA_RANK
    o_cq = o_r + nv
    o_kv = o_cq + MLA_Q_RANK
    o_kr = o_kv + MLA_KV_RANK
    d = w_in.shape[0]
    wp = jnp.concatenate([w_in[:, :o_z], w_in[:, o_r:o_kr], w_in[:, o_z:o_r],
                          jnp.zeros((d, LANES - 2 * GLA_RANK), F32)], axis=1).astype(BF16)
    kr = w_in[:, o_kr:o_kr + MLA_ROPE]
    wkr = jnp.concatenate([kr, kr[:, _ROPE_SWAP]], axis=1).T.astype(BF16)
    wg = jnp.zeros((LANES, 2 * nqk), F32)
    wg = wg.at[0:GLA_RANK, 0:nqk].set(w_gate_f).at[GLA_RANK:2 * GLA_RANK, nqk:].set(w_gate_b)
    wg_hi = wg.astype(BF16)
    wg_lo = (wg - wg_hi.astype(F32)).astype(BF16)
    bg = jnp.concatenate([b_gate_f, b_gate_b])[None, :]
    hq = np.arange(MLA_HEADS)[:, None] * MLA_QK
    nope_idx = (hq + np.arange(MLA_NOPE)[None, :]).reshape(-1)
    rope_idx = (hq + MLA_NOPE + np.arange(MLA_ROPE)[None, :]).reshape(-1)
    swap_idx = (hq + MLA_NOPE + _ROPE_SWAP[None, :]).reshape(-1)
    wuq = w_uq[:, np.concatenate([nope_idx, rope_idx, swap_idx])].astype(BF16)
    hk = np.arange(MLA_HEADS)[:, None] * (MLA_NOPE + MLA_V)
    k_idx = (hk + np.arange(MLA_NOPE)[None, :]).reshape(-1)
    v_idx = (hk + MLA_NOPE + np.arange(MLA_V)[None, :]).reshape(-1)
    return dict(wp=wp, wkr=wkr, wg_hi=wg_hi, wg_lo=wg_lo, bg=bg, qn=q_norm[None, :], wuq=wuq,
                kvn=kv_norm[None, :], wuk=w_ukv[:, k_idx].T.astype(BF16), wuv=w_ukv[:, v_idx].astype(BF16))


def _mods(m, rows, batch):
    d = D_MODEL
    if rows is None:
        return [jnp.broadcast_to(m[batch, k * d:(k + 1) * d][None, None, :], (batch, 1, d)) for k in range(6)]
    return [m[:batch, k * d:(k + 1) * d][:, None, :] for k in range(6)]


def kernel(x, c, ctx, c_ctx, w_mod, b_mod, g_mix_pre, g_mix_post, g_ffn_pre, g_ffn_post, e_w_in, e_w_gate_f, e_b_gate_f, e_w_gate_b, e_b_gate_b, e_gla_norm, e_q_norm, e_w_uq, e_kv_norm, e_w_ukv, e_w_o, e_w1, e_w3, e_w2, o_w_o, o_w_router, o_w1, o_w3, o_w2):
    batch, seq, d = x.shape
    n_ctx = ctx.shape[1]
    cond = jnp.zeros((16, d), F32).at[:batch].set(c).at[batch].set(c_ctx)
    mods = _adaln(cond, w_mod, b_mod)
    rope_x = _rope_tables(seq)
    rope_c = _identity_rope_tables(n_ctx)
    nqk = GLA_HEADS * GLA_DK
    last_read = 2 * ((DEPTH - 1) // 2)
    ow1 = o_w1.astype(BF16).reshape((-1,) + o_w1.shape[2:])
    ow3 = o_w3.astype(BF16).reshape((-1,) + o_w3.shape[2:])
    ow2 = o_w2.astype(BF16).reshape((-1,) + o_w2.shape[2:])
    xs, xc = x, ctx
    for i in range(DEPTH):
        j = i // 2
        ctx_live = i <= last_read
        ctx_full = i < last_read
        mx = _mods(mods[i], 0, batch)
        mc = _mods(mods[i], None, batch)
        gpre, gpost = g_mix_pre[i][None, :], g_mix_post[i][None, :]
        fpre, fpost = g_ffn_pre[i][None, :], g_ffn_post[i][None, :]
        if i % 2 == 0:
            w = _even_weights(e_w_in[j], e_w_gate_f[j], e_b_gate_f[j], e_w_gate_b[j], e_b_gate_b[j],
                              e_q_norm[j], e_w_uq[j], e_kv_norm[j], e_w_ukv[j])
            wo1 = e_w_o[j][:GLA_HEADS * GLA_DV].astype(BF16)
            wo2 = e_w_o[j][GLA_HEADS * GLA_DV:].astype(BF16)
            gn = e_gla_norm[j][None, :]
            w1, w3, w2 = e_w1[j].astype(BF16), e_w3[j].astype(BF16), e_w2[j].astype(BF16)
            zero_state = jnp.zeros((batch, GLA_DV, nqk), F32)
            if ctx_live:
                cgq, cgk, cgv, cgr, claf, clab, cq, ckt, cv = _in_proj(xc, gpre, mc[0], mc[1], w, rope_c, n_ctx)
                co_f, co_b, s_f, s_b = _gla(cgq, cgk, cgv, claf, clab, zero_state, zero_state, n_ctx)
                srcs_c = [(ckt, cv)]
            else:
                s_f = s_b = zero_state
                srcs_c = []
            gq, gk, gv, gr, laf, lab, q, kt, v = _in_proj(xs, gpre, mx[0], mx[1], w, rope_x, min(512, seq))
            o_f, o_b, _, _ = _gla(gq, gk, gv, laf, lab, s_f, s_b, min(128, seq))
            a = _attention(q, [(kt, v)] + srcs_c, min(1024, seq))
            xs = _even_tail(xs, o_f, o_b, gr, a, gn, wo1, wo2, gpost, mx[2], fpre, mx[3], mx[4],
                            w1, w3, w2, fpost, mx[5], min(512, seq))
            if ctx_full:
                ac = _attention(cq, srcs_c, n_ctx)
                xc = _even_tail(xc, co_f, co_b, cgr, ac, gn, wo1, wo2, gpost, mc[2], fpre, mc[3], mc[4],
                                w1, w3, w2, fpost, mc[5], n_ctx)
        else:
            wo = o_w_o[j].astype(BF16)
            wr = jnp.zeros((d, LANES), F32).at[:, :N_EXPERTS].set(o_w_router[j])
            wr_hi = wr.astype(BF16)
            wr_lo = (wr - wr_hi.astype(F32)).astype(BF16)
            base = j * N_EXPERTS
            xs = _fourier_x(xs, gpre, mx[0], mx[1], wo, gpost, mx[2])
            xs = _moe_sorted(xs, fpre, mx[3], mx[4], wr_hi, wr_lo, ow1, ow3, ow2, base, fpost, mx[5],
                             min(512, seq), min(1024, seq), 1792)
            if ctx_full:
                xc = _fourier_ctx(xc, gpre, mc[0], mc[1], wo, gpost, mc[2])
                xc = _moe_sorted(xc, fpre, mc[3], mc[4], wr_hi, wr_lo, ow1, ow3, ow2, base, fpost, mc[5],
                                 n_ctx, n_ctx, 1792)
    return xs
```

```python
import functools

import numpy as np
import jax
import jax.numpy as jnp
from jax import lax
from jax.experimental import pallas as pl
from jax.experimental.pallas import tpu as pltpu
from jax.experimental.pallas import tpu_sc as plsc

F32 = jnp.float32
BF16 = jnp.bfloat16

EPS = 1e-6
D_MODEL = 1024
DEPTH = 4
GRID_W = 64
GLA_HEADS = 4
GLA_DK = 64
GLA_DV = 128
GLA_RANK = 16
GLA_TAU = 16.0
GLA_CHUNK = 64
MLA_HEADS = 4
MLA_Q_RANK = 256
MLA_KV_RANK = 128
MLA_NOPE = 128
MLA_ROPE = 64
MLA_V = 128
MLA_QK = MLA_NOPE + MLA_ROPE
ROPE_BASE = 10000.0
FOURIER_GROUPS = 4
FOURIER_GW = D_MODEL // FOURIER_GROUPS
N_EXPERTS = 8
LANES = 128
VMEM_LIMIT = 48 * 1024 * 1024


def _cparams(sem):
    return pltpu.CompilerParams(dimension_semantics=sem, vmem_limit_bytes=VMEM_LIMIT)


def _dot(a, b):
    return jnp.dot(a, b, preferred_element_type=F32)


def _dot_nt(a, b):
    return lax.dot_general(a, b, (((1,), (1,)), ((), ())), preferred_element_type=F32)


def _dot_tn(a, b):
    return lax.dot_general(a, b, (((0,), (0,)), ((), ())), preferred_element_type=F32)


def _split(x):
    hi = x.astype(BF16)
    lo = (x - hi.astype(F32)).astype(BF16)
    return hi, lo


def _dot3(a, b_hi, b_lo):
    a_hi, a_lo = _split(a)
    return _dot(a_hi, b_hi) + _dot(a_lo, b_hi) + _dot(a_hi, b_lo)


def _rms(x, g):
    return x * lax.rsqrt(jnp.mean(x * x, axis=-1, keepdims=True) + EPS) * g


def _silu(x):
    return x / (1.0 + jnp.exp(-x))


def _const_spec(shape):
    nd = len(shape)
    return pl.BlockSpec(shape, lambda *_: (0,) * nd)


def _batch_row_spec(d):
    return pl.BlockSpec((1, 1, d), lambda b, *_: (b, 0, 0))


def _adaln_kernel(c_ref, w_ref, b_ref, o_ref):
    a = _silu(c_ref[...])
    w_hi, w_lo = _split(w_ref[0])
    o_ref[0] = _dot3(a, w_hi, w_lo) + b_ref[0]


def _adaln(cond, w_mod, b_mod):
    depth, d, n = w_mod.shape
    rows = cond.shape[0]
    bn = 1536
    return pl.pallas_call(
        _adaln_kernel,
        grid=(depth, n // bn),
        in_specs=[
            pl.BlockSpec((rows, d), lambda i, j: (0, 0)),
            pl.BlockSpec((1, d, bn), lambda i, j: (i, 0, j)),
            pl.BlockSpec((1, 1, bn), lambda i, j: (i, 0, j)),
        ],
        out_specs=pl.BlockSpec((1, rows, bn), lambda i, j: (i, 0, j)),
        out_shape=jax.ShapeDtypeStruct((depth, rows, n), F32),
        compiler_params=_cparams(("arbitrary", "arbitrary")),
        name="adaln",
    )(cond, w_mod, b_mod.reshape(depth, 1, n))


def _in_proj_kernel(x_ref, g_ref, sh_ref, sc_ref, wp_ref, wkr_ref, wgh_ref, wgl_ref, bg_ref,
                    qn_ref, wuq_ref, kvn_ref, wuk_ref, wuv_ref, cq_ref, sq_ref, ck_ref, sk_ref,
                    gq_ref, gk_ref, gv_ref, gr_ref, laf_ref, lab_ref, q_ref, kt_ref, v_ref):
    h = _rms(x_ref[0], g_ref[...]) * (1.0 + sc_ref[0]) + sh_ref[0]
    hb = h.astype(BF16)
    p = _dot(hb, wp_ref[...])
    nqk = GLA_HEADS * GLA_DK
    nv = GLA_HEADS * GLA_DV
    gq_ref[0] = p[:, 0:nqk] * (GLA_DK ** -0.5)
    gk_ref[0] = p[:, nqk:2 * nqk]
    gv_ref[0] = p[:, 2 * nqk:2 * nqk + nv]
    gr_ref[0] = p[:, 2 * nqk + nv:2 * nqk + 2 * nv]
    o = 2 * nqk + 2 * nv
    cq = p[:, o:o + MLA_Q_RANK]
    ckv = p[:, o + MLA_Q_RANK:o + MLA_Q_RANK + MLA_KV_RANK]
    tail = p[:, o + MLA_Q_RANK + MLA_KV_RANK:]
    pre = _dot3(tail, wgh_ref[...], wgl_ref[...]) + bg_ref[...]
    la = (jnp.minimum(pre, 0.0) - jnp.log(1.0 + jnp.exp(-jnp.abs(pre)))) * (1.0 / GLA_TAU)
    laf_ref[0] = la[:, :nqk]
    lab_ref[0] = la[:, nqk:]
    q = _dot(_rms(cq, qn_ref[...]).astype(BF16), wuq_ref[...])
    att_scale = MLA_QK ** -0.5
    nn = MLA_HEADS * MLA_NOPE
    nr = MLA_HEADS * MLA_ROPE
    q_rope = q[:, nn:nn + nr] * cq_ref[...] + q[:, nn + nr:] * sq_ref[...]
    for hd in range(MLA_HEADS):
        q_ref[0, hd, :, 0:MLA_NOPE] = (q[:, hd * MLA_NOPE:(hd + 1) * MLA_NOPE] * att_scale).astype(BF16)
        q_ref[0, hd, :, MLA_NOPE:MLA_QK] = (q_rope[:, hd * MLA_ROPE:(hd + 1) * MLA_ROPE] * att_scale).astype(BF16)
    ckvn = _rms(ckv, kvn_ref[...]).astype(BF16)
    kt = _dot_nt(wuk_ref[...], ckvn)
    v = _dot(ckvn, wuv_ref[...])
    kr2 = _dot_nt(wkr_ref[...], hb)
    kr = (kr2[:MLA_ROPE] * ck_ref[...] + kr2[MLA_ROPE:] * sk_ref[...]).astype(BF16)
    for hd in range(MLA_HEADS):
        kt_ref[0, hd, 0:MLA_NOPE, :] = kt[hd * MLA_NOPE:(hd + 1) * MLA_NOPE].astype(BF16)
        kt_ref[0, hd, MLA_NOPE:MLA_QK, :] = kr
        v_ref[0, hd] = v[:, hd * MLA_V:(hd + 1) * MLA_V].astype(BF16)


def _in_proj(tok, g, sh, sc, w, tabs, tm):
    b, t, d = tok.shape
    cq, sq, ck, sk = tabs
    nqk = GLA_HEADS * GLA_DK
    nv = GLA_HEADS * GLA_DV
    row = lambda n: pl.BlockSpec((1, tm, n), lambda bi, ti: (bi, ti, 0))
    weights = (w["wp"], w["wkr"], w["wg_hi"], w["wg_lo"], w["bg"], w["qn"], w["wuq"], w["kvn"],
               w["wuk"], w["wuv"])
    return pl.pallas_call(
        _in_proj_kernel,
        grid=(b, t // tm),
        in_specs=[row(d), _const_spec((1, d)), _batch_row_spec(d), _batch_row_spec(d)]
        + [_const_spec(a.shape) for a in weights]
        + [pl.BlockSpec((tm, MLA_HEADS * MLA_ROPE), lambda bi, ti: (ti, 0)),
           pl.BlockSpec((tm, MLA_HEADS * MLA_ROPE), lambda bi, ti: (ti, 0)),
           pl.BlockSpec((MLA_ROPE, tm), lambda bi, ti: (0, ti)),
           pl.BlockSpec((MLA_ROPE, tm), lambda bi, ti: (0, ti))],
        out_specs=[row(nqk), row(nqk), row(nv), row(nv), row(nqk), row(nqk),
                   pl.BlockSpec((1, MLA_HEADS, tm, MLA_QK), lambda bi, ti: (bi, 0, ti, 0)),
                   pl.BlockSpec((1, MLA_HEADS, MLA_QK, tm), lambda bi, ti: (bi, 0, 0, ti)),
                   pl.BlockSpec((1, MLA_HEADS, tm, MLA_V), lambda bi, ti: (bi, 0, ti, 0))],
        out_shape=[jax.ShapeDtypeStruct((b, t, nqk), F32), jax.ShapeDtypeStruct((b, t, nqk), F32),
                   jax.ShapeDtypeStruct((b, t, nv), F32), jax.ShapeDtypeStruct((b, t, nv), F32),
                   jax.ShapeDtypeStruct((b, t, nqk), F32), jax.ShapeDtypeStruct((b, t, nqk), F32),
                   jax.ShapeDtypeStruct((b, MLA_HEADS, t, MLA_QK), BF16),
                   jax.ShapeDtypeStruct((b, MLA_HEADS, MLA_QK, t), BF16),
                   jax.ShapeDtypeStruct((b, MLA_HEADS, t, MLA_V), BF16)],
        compiler_params=_cparams(("parallel", "parallel")),
        name="even_in_proj",
    )(tok, g, sh, sc, *weights, cq, sq, ck, sk)


def _gla_chunks(streams):
    c = GLA_CHUNK
    heads = range(GLA_HEADS)
    ks = [slice(hd * GLA_DK, (hd + 1) * GLA_DK) for hd in heads]
    vs = [slice(hd * GLA_DV, (hd + 1) * GLA_DV) for hd in heads]
    bcs = []
    for (_, _, _, l_ref, _, _, g, row0, tri, _, _) in streams:
        la_hi, la_lo = _split(l_ref[g, pl.ds(row0, c), :])
        bcs.append(_dot(tri, la_hi) + _dot(tri, la_lo))
    ops = []
    for (q_ref, k_ref, v_ref, _, _, st_ref, g, row0, _, _, last_row), bc in zip(streams, bcs):
        bl = bc[last_row:last_row + 1, :]
        q = q_ref[g, pl.ds(row0, c), :]
        k = k_ref[g, pl.ds(row0, c), :]
        st = st_ref[g]
        ops.append(dict(qc=(q * jnp.exp(bc)).astype(BF16), kc=(k * jnp.exp(-bc)).astype(BF16),
                        kd=(k * jnp.exp(bl - bc)).astype(BF16), vb=v_ref[g, pl.ds(row0, c), :].astype(BF16),
                        st=st, stb=st.astype(BF16), decay=jnp.exp(bl)))
    atts = [[_dot_nt(o["qc"][:, ks[hd]], o["kc"][:, ks[hd]]) for hd in heads] for o in ops]
    inters = [[_dot_nt(o["qc"][:, ks[hd]], o["stb"][:, ks[hd]]) for hd in heads] for o in ops]
    upds = [[_dot_tn(o["vb"][:, vs[hd]], o["kd"][:, ks[hd]]) for hd in heads] for o in ops]
    intras = [[_dot(jnp.where(s[9], att[hd], 0.0).astype(BF16), o["vb"][:, vs[hd]]) for hd in heads]
              for s, o, att in zip(streams, ops, atts)]
    for s, o, intra, inter, upd in zip(streams, ops, intras, inters, upds):
        o_ref, st_ref, g, row0 = s[4], s[5], s[6], s[7]
        o_ref[g, pl.ds(row0, c), :] = jnp.concatenate([a + b for a, b in zip(intra, inter)], axis=1)
        st_ref[g] = o["st"] * o["decay"] + jnp.concatenate(upd, axis=1)


def _gla_kernel(qf, kf, vf, lf, qb, kb, vb, lb, s0f, s0b, of, ob, sff, sfb, stf, stb, *, nc, gb):
    j = pl.program_id(1)

    @pl.when(j == 0)
    def _():
        stf[...] = s0f[...]
        stb[...] = s0b[...]

    c = GLA_CHUNK
    r = lax.broadcasted_iota(jnp.int32, (c, c), 0)
    cc = lax.broadcasted_iota(jnp.int32, (c, c), 1)
    lower = r >= cc
    upper = r <= cc
    tri_l = jnp.where(lower, 1.0, 0.0).astype(BF16)
    tri_u = jnp.where(upper, 1.0, 0.0).astype(BF16)

    for ci in range(nc):
        streams = []
        for g in range(gb):
            streams.append((qf, kf, vf, lf, of, stf, g, ci * c, tri_l, lower, c - 1))
            streams.append((qb, kb, vb, lb, ob, stb, g, (nc - 1 - ci) * c, tri_u, upper, 0))
        _gla_chunks(streams)

    @pl.when(j == pl.num_programs(1) - 1)
    def _():
        sff[...] = stf[...]
        sfb[...] = stb[...]


GLA_BATCH_ROWS = 4


def _gla(gq, gk, gv, laf, lab, s0f, s0b, tb):
    b, t, nqk = gq.shape
    nv = gv.shape[-1]
    nblk = t // tb
    gb = min(GLA_BATCH_ROWS, b)
    fwd = lambda n: pl.BlockSpec((gb, tb, n), lambda bi, j: (bi, j, 0))
    bwd = lambda n: pl.BlockSpec((gb, tb, n), lambda bi, j: (bi, nblk - 1 - j, 0))
    st = pl.BlockSpec((gb, GLA_DV, nqk), lambda bi, j: (bi, 0, 0))
    return pl.pallas_call(
        functools.partial(_gla_kernel, nc=tb // GLA_CHUNK, gb=gb),
        grid=(b // gb, nblk),
        in_specs=[fwd(nqk), fwd(nqk), fwd(nv), fwd(nqk), bwd(nqk), bwd(nqk), bwd(nv), bwd(nqk), st, st],
        out_specs=[fwd(nv), bwd(nv), st, st],
        out_shape=[jax.ShapeDtypeStruct((b, t, nv), F32), jax.ShapeDtypeStruct((b, t, nv), F32),
                   jax.ShapeDtypeStruct((b, GLA_DV, nqk), F32), jax.ShapeDtypeStruct((b, GLA_DV, nqk), F32)],
        scratch_shapes=[pltpu.VMEM((gb, GLA_DV, nqk), F32), pltpu.VMEM((gb, GLA_DV, nqk), F32)],
        compiler_params=_cparams(("parallel", "arbitrary")),
        name="gla_scan",
    )(gq, gk, gv, laf, gq, gk, gv, lab, s0f, s0b)


ATTN_KEY_CHUNK = 1024


def _attn_kernel(*refs, n_src):
    q_ref = refs[0]
    kts = refs[1:1 + 2 * n_src:2]
    vs = refs[2:2 + 2 * n_src:2]
    o_ref = refs[1 + 2 * n_src]
    vexts = refs[2 + 2 * n_src:]

    @pl.when(pl.program_id(2) == 0)
    def _():
        for v, vext in zip(vs, vexts):
            tk = v.shape[2]
            lane = lax.broadcasted_iota(jnp.int32, (tk, LANES), 1)
            vext[:, 0:MLA_V] = v[0, 0]
            vext[:, MLA_V:MLA_V + LANES] = jnp.where(lane == 0, 1.0, 0.0).astype(BF16)

    q = q_ref[0, 0]
    bq = q.shape[0]
    m = jnp.full((bq, 1), -jnp.inf, F32)
    acc = jnp.zeros((bq, MLA_V + LANES), F32)
    chunks = []
    for kt, vext in zip(kts, vexts):
        tk = kt.shape[3]
        ck = min(ATTN_KEY_CHUNK, tk)
        chunks += [(kt, vext, c * ck, (c + 1) * ck) for c in range(tk // ck)]
    s_next = _dot(q, chunks[0][0][0, 0, :, chunks[0][2]:chunks[0][3]])
    for i, (kt, vext, lo, hi) in enumerate(chunks):
        s = s_next
        if i + 1 < len(chunks):
            kt_n, _, lo_n, hi_n = chunks[i + 1]
            s_next = _dot(q, kt_n[0, 0, :, lo_n:hi_n])
        m_new = jnp.maximum(m, s.max(axis=-1, keepdims=True))
        p = jnp.exp((s - m_new).astype(BF16))
        acc = acc * jnp.exp(m - m_new) + _dot(p, vext[lo:hi, :])
        m = m_new
    o_ref[0] = (acc[:, 0:MLA_V] / acc[:, MLA_V:MLA_V + 1]).astype(o_ref.dtype)


def _attention(q, srcs, bq):
    b, nh, t, dqk = q.shape
    in_specs = [pl.BlockSpec((1, 1, bq, dqk), lambda bi, hi, qi: (bi, hi, qi, 0))]
    args = [q]
    scratch = []
    for kt, v in srcs:
        tk = kt.shape[-1]
        in_specs.append(pl.BlockSpec((1, 1, dqk, tk), lambda bi, hi, qi: (bi, hi, 0, 0)))
        in_specs.append(pl.BlockSpec((1, 1, tk, MLA_V), lambda bi, hi, qi: (bi, hi, 0, 0)))
        args += [kt, v]
        scratch.append(pltpu.VMEM((tk, MLA_V + LANES), BF16))
    return pl.pallas_call(
        functools.partial(_attn_kernel, n_src=len(srcs)),
        grid=(b, nh, t // bq),
        in_specs=in_specs,
        out_specs=pl.BlockSpec((1, bq, MLA_V), lambda bi, hi, qi: (bi, qi, hi)),
        out_shape=jax.ShapeDtypeStruct((b, t, nh * MLA_V), BF16),
        scratch_shapes=scratch,
        compiler_params=_cparams(("parallel", "parallel", "arbitrary")),
        name="mla_attention",
    )(*args)


MXU_COLS = 256


def _ff_chunks(dff, max_tiles):
    n_tiles = dff // MXU_COLS
    n_chunks = -(-n_tiles // max_tiles)
    base, extra = divmod(n_tiles, n_chunks)
    bounds, start = [], 0
    for i in range(n_chunks):
        width = (base + (1 if i < extra else 0)) * MXU_COLS
        bounds.append((start, start + width))
        start += width
    return bounds


def _even_tail_kernel(x_ref, of_ref, ob_ref, gr_ref, a_ref, gn_ref, wo1_ref, wo2_ref, gpost_ref, ga_ref,
                      gpre_ref, sh_ref, sc_ref, w1_ref, w3_ref, w2_ref, fpost_ref, gaf_ref, xo_ref, *, chunks):
    o = of_ref[0] + ob_ref[0]
    parts = [_rms(o[:, hd * GLA_DV:(hd + 1) * GLA_DV], gn_ref[...]) for hd in range(GLA_HEADS)]
    fin = jnp.concatenate(parts, axis=1) * _silu(gr_ref[0])
    y = _dot(fin.astype(BF16), wo1_ref[...]) + _dot(a_ref[0], wo2_ref[...])
    xn = x_ref[0] + ga_ref[0] * _rms(y, gpost_ref[...])
    h = (_rms(xn, gpre_ref[...]) * (1.0 + sc_ref[0]) + sh_ref[0]).astype(BF16)
    z = None
    for lo, hi in chunks:
        a = _dot(h, w1_ref[:, lo:hi])
        mid = (_silu(a) * _dot(h, w3_ref[:, lo:hi])).astype(BF16)
        part = _dot(mid, w2_ref[lo:hi, :])
        z = part if z is None else z + part
    xo_ref[0] = xn + gaf_ref[0] * _rms(z, fpost_ref[...])


def _even_tail(tok, o_f, o_b, gr, a, gn, wo1, wo2, gpost, ga, gpre, sh, sc, w1, w3, w2, fpost, gaf, tm):
    b, t, d = tok.shape
    nv = o_f.shape[-1]
    dff = w1.shape[-1]
    row = lambda n: pl.BlockSpec((1, tm, n), lambda bi, ti: (bi, ti, 0))
    resident = lambda shape: pl.BlockSpec(shape, lambda bi, ti: (0, 0), pipeline_mode=pl.Buffered(1))
    return pl.pallas_call(
        functools.partial(_even_tail_kernel, chunks=_ff_chunks(dff, 6)),
        grid=(b, t // tm),
        in_specs=[row(d), row(nv), row(nv), row(nv), row(a.shape[-1]), _const_spec(gn.shape),
                  resident(wo1.shape), resident(wo2.shape), _const_spec((1, d)), _batch_row_spec(d),
                  _const_spec((1, d)), _batch_row_spec(d), _batch_row_spec(d),
                  resident((d, dff)), resident((d, dff)), resident((dff, d)),
                  _const_spec((1, d)), _batch_row_spec(d)],
        out_specs=row(d),
        out_shape=jax.ShapeDtypeStruct((b, t, d), F32),
        compiler_params=_cparams(("parallel", "parallel")),
        name="even_tail",
    )(tok, o_f, o_b, gr, a, gn, wo1, wo2, gpost, ga, gpre, sh, sc, w1, w3, w2, fpost, gaf)


def _dft_cos_sin(n):
    idx = (np.arange(n)[:, None] * np.arange(n)[None, :]) % n
    ang = 2.0 * np.pi * idx.astype(np.float64) / n
    return np.cos(ang), np.sin(ang)


def _fourier1_kernel(x_ref, g_ref, sh_ref, sc_ref, f1_ref, o_ref, *, nb, n1):
    xt = jnp.swapaxes(x_ref[0], 0, 1)
    for j in range(nb):
        hb = (_rms(xt[j], g_ref[...]) * (1.0 + sc_ref[0]) + sh_ref[0]).astype(BF16)
        a = _dot(f1_ref[j], hb)
        o_ref[0, 0, j] = a[:n1].astype(o_ref.dtype)
        o_ref[0, 1, j] = a[n1:].astype(o_ref.dtype)


def _fourier2_kernel(b_ref, x_ref, f2_ref, cc_ref, sc_ref, wo_ref, gpost_ref, ga_ref, xo_ref, *, kb, n2):
    d = b_ref.shape[-1]
    br = jnp.swapaxes(b_ref[0, 0].astype(F32), 0, 1)
    bi = jnp.swapaxes(b_ref[0, 1].astype(F32), 0, 1)
    urs, uis = [], []
    for j in range(kb):
        u = _dot(f2_ref[...], jnp.concatenate([br[j], bi[j]], axis=0).astype(BF16))
        urs.append(u[:n2])
        uis.append(u[n2:])
    ur = jnp.concatenate(urs, axis=0).astype(BF16)
    ui = jnp.concatenate(uis, axis=0).astype(BF16)
    gw = FOURIER_GW
    f = jnp.concatenate(
        [_dot(ur[:, g * gw:(g + 1) * gw], cc_ref[...]) + _dot(ui[:, g * gw:(g + 1) * gw], sc_ref[...])
         for g in range(FOURIER_GROUPS)], axis=1)
    y = _dot(f.astype(BF16), wo_ref[...])
    yn = ga_ref[0] * _rms(y, gpost_ref[...])
    xo_ref[0] = x_ref[0] + jnp.swapaxes(yn.reshape(kb, n2, d), 0, 1)


def _fourier_x(tok, g, sh, sc, wo, gpost, ga):
    b, t, d = tok.shape
    n1, n2 = 128, t // 128
    nb, kb = min(8, n2), 16
    c2, s2 = _dft_cos_sin(n2)
    cg, sg = _dft_cos_sin(FOURIER_GW)
    tok_idx = np.arange(n2)[:, None, None] + n2 * np.arange(n1)[None, None, :]
    ang = 2.0 * np.pi * ((np.arange(n1)[None, :, None] * tok_idx) % t).astype(np.float64) / t
    f1 = jnp.asarray(np.concatenate([np.cos(ang), -np.sin(ang)], axis=1) / np.sqrt(n1), BF16)
    f2 = jnp.asarray(np.block([[c2, s2], [-s2, c2]]) / np.sqrt(n2), BF16)
    ccg = jnp.asarray(cg / np.sqrt(FOURIER_GW), BF16)
    scg = jnp.asarray(sg / np.sqrt(FOURIER_GW), BF16)
    stage1 = pl.pallas_call(
        functools.partial(_fourier1_kernel, nb=nb, n1=n1),
        grid=(b, n2 // nb),
        in_specs=[pl.BlockSpec((1, n1, nb, d), lambda bi, ji: (bi, 0, ji, 0)),
                  _const_spec((1, d)), _batch_row_spec(d), _batch_row_spec(d),
                  pl.BlockSpec((nb, 2 * n1, n1), lambda bi, ji: (ji, 0, 0))],
        out_specs=pl.BlockSpec((1, 2, nb, n1, d), lambda bi, ji: (bi, 0, ji, 0, 0)),
        out_shape=jax.ShapeDtypeStruct((b, 2, n2, n1, d), BF16),
        compiler_params=_cparams(("parallel", "parallel")),
        name="fourier_stage1",
    )(tok.reshape(b, n1, n2, d), g, sh, sc, f1)
    freq = lambda n: pl.BlockSpec((1, n2, kb, n), lambda bi, ki: (bi, 0, ki, 0))
    out = pl.pallas_call(
        functools.partial(_fourier2_kernel, kb=kb, n2=n2),
        grid=(b, n1 // kb),
        in_specs=[pl.BlockSpec((1, 2, n2, kb, d), lambda bi, ki: (bi, 0, 0, ki, 0)),
                  freq(d),
                  _const_spec(f2.shape), _const_spec(ccg.shape), _const_spec(scg.shape),
                  _const_spec(wo.shape), _const_spec((1, d)), _batch_row_spec(d)],
        out_specs=freq(d),
        out_shape=jax.ShapeDtypeStruct((b, n2, n1, d), F32),
        compiler_params=_cparams(("parallel", "parallel")),
        name="fourier_stage2",
    )(stage1, tok.reshape(b, n2, n1, d), f2, ccg, scg, wo, gpost, ga)
    return out.reshape(b, t, d)


def _fourier_ctx_kernel(x_ref, g_ref, sh_ref, sc_ref, fl_ref, cc_ref, scg_ref, wo_ref, gpost_ref, ga_ref,
                        xo_ref):
    x = x_ref[0]
    t = x.shape[0]
    h = _rms(x, g_ref[...]) * (1.0 + sc_ref[0]) + sh_ref[0]
    u = _dot(fl_ref[...], h.astype(BF16))
    ur = u[:t].astype(BF16)
    ui = u[t:].astype(BF16)
    gw = FOURIER_GW
    f = jnp.concatenate(
        [_dot(ur[:, g * gw:(g + 1) * gw], cc_ref[...]) + _dot(ui[:, g * gw:(g + 1) * gw], scg_ref[...])
         for g in range(FOURIER_GROUPS)], axis=1)
    y = _dot(f.astype(BF16), wo_ref[...])
    xo_ref[0] = x + ga_ref[0] * _rms(y, gpost_ref[...])


def _fourier_ctx(tok, g, sh, sc, wo, gpost, ga):
    b, t, d = tok.shape
    cl, sl = _dft_cos_sin(t)
    cg, sg = _dft_cos_sin(FOURIER_GW)
    fl = jnp.asarray(np.concatenate([cl, -sl], axis=0) / np.sqrt(t), BF16)
    ccg = jnp.asarray(cg / np.sqrt(FOURIER_GW), BF16)
    scg = jnp.asarray(sg / np.sqrt(FOURIER_GW), BF16)
    row = pl.BlockSpec((1, t, d), lambda bi: (bi, 0, 0))
    return pl.pallas_call(
        _fourier_ctx_kernel,
        grid=(b,),
        in_specs=[row, _const_spec((1, d)), _batch_row_spec(d), _batch_row_spec(d), _const_spec(fl.shape),
                  _const_spec(ccg.shape), _const_spec(scg.shape), _const_spec(wo.shape),
                  _const_spec((1, d)), _batch_row_spec(d)],
        out_specs=row,
        out_shape=jax.ShapeDtypeStruct((b, t, d), F32),
        compiler_params=_cparams(("parallel",)),
        name="fourier_ctx",
    )(tok, g, sh, sc, fl, ccg, scg, wo, gpost, ga)


PLANE_W = 256
N_PLANES = D_MODEL // (2 * PLANE_W)
SC_WINDOW = 128
ROUTE_META_ROWS = 8


def _pack_planes(h):
    out = []
    for p in range(N_PLANES):
        base = 2 * p * PLANE_W
        hi = pltpu.bitcast(h[:, base:base + PLANE_W].astype(BF16).astype(F32), jnp.uint32)
        lo = pltpu.bitcast(h[:, base + PLANE_W:base + 2 * PLANE_W].astype(BF16).astype(F32), jnp.uint32)
        out.append(hi | (lo >> 16))
    return out


def _unpack_planes(planes):
    cols = []
    for w in planes:
        cols.append(pltpu.bitcast(w & jnp.uint32(0xFFFF0000), F32))
        cols.append(pltpu.bitcast(w << 16, F32))
    return jnp.concatenate(cols, axis=1)


def _route_rows(h, wrh_ref, wrl_ref, carry_ref):
    logits = _dot3(h, wrh_ref[...], wrl_ref[...])
    tm = logits.shape[0]
    lane = lax.broadcasted_iota(jnp.int32, logits.shape, 1)
    neg = -jnp.inf
    l1 = jnp.where(lane < N_EXPERTS, logits, neg)
    m1 = l1.max(axis=-1, keepdims=True)
    i1 = jnp.where(l1 == m1, lane, LANES).min(axis=-1, keepdims=True)
    l2 = jnp.where(lane == i1, neg, l1)
    m2 = l2.max(axis=-1, keepdims=True)
    i2 = jnp.where(l2 == m2, lane, LANES).min(axis=-1, keepdims=True)
    e = jnp.exp(m2 - m1)
    g1 = 1.0 / (1.0 + e)
    g2 = e / (1.0 + e)
    sel = jnp.logical_or(lane == i1, lane == i2)
    cnt = jnp.where(sel, 1.0, 0.0)
    r = lax.broadcasted_iota(jnp.int32, (tm, tm), 0)
    c = lax.broadcasted_iota(jnp.int32, (tm, tm), 1)
    below = jnp.where(r > c, 1.0, 0.0).astype(BF16)
    before = _dot(below, cnt.astype(BF16)) + carry_ref[...]
    r1 = jnp.where(lane == i1, before, 0.0).sum(axis=-1, keepdims=True).astype(jnp.int32)
    r2 = jnp.where(lane == i2, before, 0.0).sum(axis=-1, keepdims=True).astype(jnp.int32)
    carry_ref[...] = carry_ref[...] + cnt.sum(axis=0, keepdims=True)
    im = jnp.where(lane == 0, i1, jnp.where(lane == 1, i2, jnp.where(lane == 2, r1, r2)))
    return im, jnp.where(lane == 0, g1, g2)


def _route_kernel(x_ref, g_ref, sh_ref, sc_ref, wrh_ref, wrl_ref, hp_ref, im_ref, gm_ref, cnt_ref, carry_ref):
    first = jnp.logical_and(pl.program_id(0) == 0, pl.program_id(1) == 0)

    @pl.when(first)
    def _():
        carry_ref[...] = jnp.zeros_like(carry_ref)

    h = _rms(x_ref[0], g_ref[...]) * (1.0 + sc_ref[0]) + sh_ref[0]
    for p, w in enumerate(_pack_planes(h)):
        hp_ref[p, 0] = w
    im, gm_ref[0] = _route_rows(h, wrh_ref, wrl_ref, carry_ref)
    im_ref[0] = jnp.transpose(im)[0:ROUTE_META_ROWS, :]
    cnt_ref[...] = carry_ref[...]


def _route(tok, g, sh, sc, wr_hi, wr_lo, tm):
    b, t, d = tok.shape
    row = lambda n: pl.BlockSpec((1, tm, n), lambda bi, ti: (bi, ti, 0))
    return pl.pallas_call(
        _route_kernel,
        grid=(b, t // tm),
        in_specs=[row(d), _const_spec((1, d)), _batch_row_spec(d), _batch_row_spec(d),
                  _const_spec(wr_hi.shape), _const_spec(wr_lo.shape)],
        out_specs=[pl.BlockSpec((N_PLANES, 1, tm, PLANE_W), lambda bi, ti: (0, bi, ti, 0)),
                   pl.BlockSpec((1, ROUTE_META_ROWS, tm), lambda bi, ti: (bi, 0, ti)),
                   row(LANES), _const_spec((1, LANES))],
        out_shape=[jax.ShapeDtypeStruct((N_PLANES, b, t, PLANE_W), jnp.uint32),
                   jax.ShapeDtypeStruct((b, ROUTE_META_ROWS, t), jnp.int32),
                   jax.ShapeDtypeStruct((b, t, LANES), F32),
                   jax.ShapeDtypeStruct((1, LANES), F32)],
        scratch_shapes=[pltpu.VMEM((1, LANES), F32)],
        compiler_params=_cparams(("arbitrary", "arbitrary")),
        name="moe_route",
    )(tok, g, sh, sc, wr_hi, wr_lo)


def _sc_mesh():
    return plsc.VectorSubcoreMesh(core_axis_name="core", subcore_axis_name="subcore")


def _sc_gather_rows(table, idx):
    n = idx.shape[0]
    w = table.shape[1]

    @pl.kernel(out_type=jax.ShapeDtypeStruct((n, w), table.dtype), mesh=_sc_mesh())
    def gather(t_hbm, i_hbm, o_hbm):
        def body(i_vmem, o_vmem):
            pltpu.sync_copy(t_hbm.at[i_vmem.at[0]], o_vmem)

        pltpu.emit_pipeline(
            body, grid=(n // SC_WINDOW,),
            in_specs=[pl.BlockSpec((1, SC_WINDOW), index_map=lambda i: (0, i))],
            out_specs=[pl.BlockSpec((SC_WINDOW, w), index_map=lambda i: (i, 0))],
            core_axis_name=("core", "subcore"), dimension_semantics=(pltpu.PARALLEL,),
        )(i_hbm, o_hbm)

    return gather(table, idx.reshape(1, n))


def _sc_scatter_rows(src, idx, n_out):
    n = idx.shape[0]
    w = src.shape[1]
    n_src_windows = src.shape[0] // SC_WINDOW

    @pl.kernel(out_type=jax.ShapeDtypeStruct((n_out, w), src.dtype), mesh=_sc_mesh(), scratch_types=[])
    def scatter(s_hbm, i_hbm, o_hbm):
        def body(s_vmem, i_vmem):
            pltpu.sync_copy(s_vmem, o_hbm.at[i_vmem.at[0]])

        pltpu.emit_pipeline(
            body, grid=(n // SC_WINDOW,),
            in_specs=[pl.BlockSpec((SC_WINDOW, w), index_map=lambda i: (i % n_src_windows, 0)),
                      pl.BlockSpec((1, SC_WINDOW), index_map=lambda i: (0, i))],
            out_specs=[],
            core_axis_name=("core", "subcore"), dimension_semantics=(pltpu.PARALLEL,),
        )(s_hbm, i_hbm)

    return scatter(src, idx.reshape(1, n))


def _grouped_ffn_kernel(te_ref, nv_ref, xs_ref, w1_ref, w3_ref, w2_ref, y_ref, hb_ref, acc_ref):
    i = pl.program_id(0)
    f = pl.program_id(1)

    @pl.when(i < nv_ref[0])
    def _():
        @pl.when(f == 0)
        def _():
            hb_ref[...] = _unpack_planes([xs_ref[p] for p in range(N_PLANES)]).astype(BF16)

        half = hb_ref.shape[0] // 2
        for r in range(2):
            rows = slice(r * half, (r + 1) * half)
            h = hb_ref[rows, :]
            a = _dot(h, w1_ref[0])
            mid = (_silu(a) * _dot(h, w3_ref[0])).astype(BF16)
            contrib = _dot(mid, w2_ref[0])

            @pl.when(f == 0)
            def _():
                acc_ref[rows, :] = contrib

            @pl.when(f > 0)
            def _():
                acc_ref[rows, :] += contrib

        @pl.when(f == pl.num_programs(1) - 1)
        def _():
            for p, w in enumerate(_pack_planes(acc_ref[...])):
                y_ref[p] = w


def _grouped_ffn(xs, tile_expert, n_valid, w1, w3, w2, tm, fc):
    n_pad = xs.shape[1]
    ne, d, dff = w1.shape
    plane = pl.BlockSpec((N_PLANES, tm, PLANE_W), lambda i, f, te, nv: (0, i, 0))
    return pl.pallas_call(
        _grouped_ffn_kernel,
        grid_spec=pltpu.PrefetchScalarGridSpec(
            num_scalar_prefetch=2,
            grid=(n_pad // tm, dff // fc),
            in_specs=[plane,
                      pl.BlockSpec((1, d, fc), lambda i, f, te, nv: (te[i], 0, f)),
                      pl.BlockSpec((1, d, fc), lambda i, f, te, nv: (te[i], 0, f)),
                      pl.BlockSpec((1, fc, d), lambda i, f, te, nv: (te[i], f, 0))],
            out_specs=plane,
            scratch_shapes=[pltpu.VMEM((tm, d), BF16), pltpu.VMEM((tm, d), F32)]),
        out_shape=jax.ShapeDtypeStruct(xs.shape, jnp.uint32),
        compiler_params=_cparams(("arbitrary", "arbitrary")),
        name="moe_grouped_ffn",
    )(tile_expert, n_valid, xs, w1, w3, w2)


def _combine_kernel(x_ref, y_ref, gm_ref, g_ref, ga_ref, xo_ref):
    gm = gm_ref[0]
    y1 = _unpack_planes([y_ref[p, 0, 0] for p in range(N_PLANES)])
    y2 = _unpack_planes([y_ref[p, 1, 0] for p in range(N_PLANES)])
    mix = gm[:, 0:1] * y1 + gm[:, 1:2] * y2
    xo_ref[0] = x_ref[0] + ga_ref[0] * _rms(mix, g_ref[...])


def _combine(tok, yg, gm, g, ga, tm):
    b, t, d = tok.shape
    row = lambda n: pl.BlockSpec((1, tm, n), lambda bi, ti: (bi, ti, 0))
    return pl.pallas_call(
        _combine_kernel,
        grid=(b, t // tm),
        in_specs=[row(d),
                  pl.BlockSpec((N_PLANES, 2, 1, tm, PLANE_W), lambda bi, ti: (0, 0, bi, ti, 0)),
                  row(LANES), _const_spec((1, d)), _batch_row_spec(d)],
        out_specs=row(d),
        out_shape=jax.ShapeDtypeStruct((b, t, d), F32),
        compiler_params=_cparams(("parallel", "parallel")),
        name="moe_combine",
    )(tok, yg, gm, g, ga)


def _moe_sorted(tok, g_pre, sh, sc, wr_hi, wr_lo, w1, w3, w2, expert_base, g_post, ga, tm, tm_e, fc):
    b, t, d = tok.shape
    n_tok = b * t
    hp, im, gm, cnt = _route(tok, g_pre, sh, sc, wr_hi, wr_lo, tm)
    counts = cnt[0, :N_EXPERTS].astype(jnp.int32)
    padded = ((counts + tm_e - 1) // tm_e) * tm_e
    ends = jnp.cumsum(padded)
    starts = ends - padded
    n_pad = 2 * n_tok + N_EXPERTS * tm_e
    n_tiles = n_pad // tm_e
    tile_expert = jnp.minimum(
        jnp.sum((jnp.arange(n_tiles, dtype=jnp.int32)[:, None] * tm_e >= ends[None, :]).astype(jnp.int32), axis=1),
        N_EXPERTS - 1).astype(jnp.int32) + expert_base
    n_valid = (ends[-1:] // tm_e).astype(jnp.int32)
    e1, e2, r1, r2 = (im[:, k, :].reshape(n_tok) for k in range(4))
    pos = jnp.stack([starts[e1] + r1, starts[e2] + r2], axis=0)
    plane_off = (jnp.arange(N_PLANES, dtype=jnp.int32) * n_pad)
    idx_dispatch = (pos[:, None, :] + plane_off[None, :, None]).reshape(-1)
    idx_return = (pos[None, :, :] + plane_off[:, None, None]).reshape(-1)
    xs = _sc_scatter_rows(hp.reshape(N_PLANES * n_tok, PLANE_W), idx_dispatch, N_PLANES * n_pad)
    ys = _grouped_ffn(xs.reshape(N_PLANES, n_pad, PLANE_W), tile_expert, n_valid, w1, w3, w2, tm_e, fc)
    yg = _sc_gather_rows(ys.reshape(N_PLANES * n_pad, PLANE_W), idx_return)
    return _combine(tok, yg.reshape(N_PLANES, 2, b, t, PLANE_W), gm, g_post, ga, tm)


_ROPE_SWAP = np.concatenate([np.arange(16, 32), np.arange(0, 16), np.arange(48, 64), np.arange(32, 48)])


def _rope_tables(n_tok):
    rows = n_tok // GRID_W
    row = jnp.broadcast_to(jnp.arange(rows, dtype=F32)[:, None], (rows, GRID_W)).reshape(-1)
    col = jnp.broadcast_to(jnp.arange(GRID_W, dtype=F32)[None, :], (rows, GRID_W)).reshape(-1)
    half = MLA_ROPE // 2
    inv = 1.0 / (ROPE_BASE ** (jnp.arange(0, half, 2, dtype=F32) / half))
    cr, sr = jnp.cos(row[:, None] * inv), jnp.sin(row[:, None] * inv)
    cc, sc = jnp.cos(col[:, None] * inv), jnp.sin(col[:, None] * inv)
    cos64 = jnp.concatenate([cr, cr, cc, cc], axis=-1)
    sin64 = jnp.concatenate([-sr, sr, -sc, sc], axis=-1)
    return (jnp.tile(cos64, (1, MLA_HEADS)), jnp.tile(sin64, (1, MLA_HEADS)), cos64.T, sin64.T)


def _identity_rope_tables(n_tok):
    one = jnp.ones((n_tok, MLA_ROPE), F32)
    zero = jnp.zeros((n_tok, MLA_ROPE), F32)
    return (jnp.tile(one, (1, MLA_HEADS)), jnp.tile(zero, (1, MLA_HEADS)), one.T, zero.T)


def _even_weights(w_in, w_gate_f, b_gate_f, w_gate_b, b_gate_b, q_norm, w_uq, kv_norm, w_ukv):
    nqk = GLA_HEADS * GLA_DK
    nv = GLA_HEADS * GLA_DV
    o_z = 2 * nqk + nv
    o_r = o_z + 2 * GLA_RANK
    o_cq = o_r + nv
    o_kv = o_cq + MLA_Q_RANK
    o_kr = o_kv + MLA_KV_RANK
    d = w_in.shape[0]
    wp = jnp.concatenate([w_in[:, :o_z], w_in[:, o_r:o_kr], w_in[:, o_z:o_r],
                          jnp.zeros((d, LANES - 2 * GLA_RANK), F32)], axis=1).astype(BF16)
    kr = w_in[:, o_kr:o_kr + MLA_ROPE]
    wkr = jnp.concatenate([kr, kr[:, _ROPE_SWAP]], axis=1).T.astype(BF16)
    wg = jnp.zeros((LANES, 2 * nqk), F32)
    wg = wg.at[0:GLA_RANK, 0:nqk].set(w_gate_f).at[GLA_RANK:2 * GLA_RANK, nqk:].set(w_gate_b)
    wg_hi = wg.astype(BF16)
    wg_lo = (wg - wg_hi.astype(F32)).astype(BF16)
    bg = jnp.concatenate([b_gate_f, b_gate_b])[None, :]
    hq = np.arange(MLA_HEADS)[:, None] * MLA_QK
    nope_idx = (hq + np.arange(MLA_NOPE)[None, :]).reshape(-1)
    rope_idx = (hq + MLA_NOPE + np.arange(MLA_ROPE)[None, :]).reshape(-1)
    swap_idx = (hq + MLA_NOPE + _ROPE_SWAP[None, :]).reshape(-1)
    wuq = w_uq[:, np.concatenate([nope_idx, rope_idx, swap_idx])].astype(BF16)
    hk = np.arange(MLA_HEADS)[:, None] * (MLA_NOPE + MLA_V)
    k_idx = (hk + np.arange(MLA_NOPE)[None, :]).reshape(-1)
    v_idx = (hk + MLA_NOPE + np.arange(MLA_V)[None, :]).reshape(-1)
    return dict(wp=wp, wkr=wkr, wg_hi=wg_hi, wg_lo=wg_lo, bg=bg, qn=q_norm[None, :], wuq=wuq,
                kvn=kv_norm[None, :], wuk=w_ukv[:, k_idx].T.astype(BF16), wuv=w_ukv[:, v_idx].astype(BF16))


def _mods(m, rows, batch):
    d = D_MODEL
    if rows is None:
        return [jnp.broadcast_to(m[batch, k * d:(k + 1) * d][None, None, :], (batch, 1, d)) for k in range(6)]
    return [m[:batch, k * d:(k + 1) * d][:, None, :] for k in range(6)]


def kernel(x, c, ctx, c_ctx, w_mod, b_mod, g_mix_pre, g_mix_post, g_ffn_pre, g_ffn_post, e_w_in, e_w_gate_f, e_b_gate_f, e_w_gate_b, e_b_gate_b, e_gla_norm, e_q_norm, e_w_uq, e_kv_norm, e_w_ukv, e_w_o, e_w1, e_w3, e_w2, o_w_o, o_w_router, o_w1, o_w3, o_w2):
    batch, seq, d = x.shape
    n_ctx = ctx.shape[1]
    cond = jnp.zeros((16, d), F32).at[:batch].set(c).at[batch].set(c_ctx)
    mods = _adaln(cond, w_mod, b_mod)
    rope_x = _rope_tables(seq)
    rope_c = _identity_rope_tables(n_ctx)
    nqk = GLA_HEADS * GLA_DK
    last_read = 2 * ((DEPTH - 1) // 2)
    ow1 = o_w1.astype(BF16).reshape((-1,) + o_w1.shape[2:])
    ow3 = o_w3.astype(BF16).reshape((-1,) + o_w3.shape[2:])
    ow2 = o_w2.astype(BF16).reshape((-1,) + o_w2.shape[2:])
    xs, xc = x, ctx
    for i in range(DEPTH):
        j = i // 2
        ctx_live = i <= last_read
        ctx_full = i < last_read
        mx = _mods(mods[i], 0, batch)
        mc = _mods(mods[i], None, batch)
        gpre, gpost = g_mix_pre[i][None, :], g_mix_post[i][None, :]
        fpre, fpost = g_ffn_pre[i][None, :], g_ffn_post[i][None, :]
        if i % 2 == 0:
            w = _even_weights(e_w_in[j], e_w_gate_f[j], e_b_gate_f[j], e_w_gate_b[j], e_b_gate_b[j],
                              e_q_norm[j], e_w_uq[j], e_kv_norm[j], e_w_ukv[j])
            wo1 = e_w_o[j][:GLA_HEADS * GLA_DV].astype(BF16)
            wo2 = e_w_o[j][GLA_HEADS * GLA_DV:].astype(BF16)
            gn = e_gla_norm[j][None, :]
            w1, w3, w2 = e_w1[j].astype(BF16), e_w3[j].astype(BF16), e_w2[j].astype(BF16)
            zero_state = jnp.zeros((batch, GLA_DV, nqk), F32)
            if ctx_live:
                cgq, cgk, cgv, cgr, claf, clab, cq, ckt, cv = _in_proj(xc, gpre, mc[0], mc[1], w, rope_c, n_ctx)
                co_f, co_b, s_f, s_b = _gla(cgq, cgk, cgv, claf, clab, zero_state, zero_state, n_ctx)
                srcs_c = [(ckt, cv)]
            else:
                s_f = s_b = zero_state
                srcs_c = []
            gq, gk, gv, gr, laf, lab, q, kt, v = _in_proj(xs, gpre, mx[0], mx[1], w, rope_x, min(512, seq))
            o_f, o_b, _, _ = _gla(gq, gk, gv, laf, lab, s_f, s_b, min(256, seq))
            a = _attention(q, [(kt, v)] + srcs_c, min(1024, seq))
            xs = _even_tail(xs, o_f, o_b, gr, a, gn, wo1, wo2, gpost, mx[2], fpre, mx[3], mx[4],
                            w1, w3, w2, fpost, mx[5], min(512, seq))
            if ctx_full:
                ac = _attention(cq, srcs_c, n_ctx)
                xc = _even_tail(xc, co_f, co_b, cgr, ac, gn, wo1, wo2, gpost, mc[2], fpre, mc[3], mc[4],
                                w1, w3, w2, fpost, mc[5], n_ctx)
        else:
            wo = o_w_o[j].astype(BF16)
            wr = jnp.zeros((d, LANES), F32).at[:, :N_EXPERTS].set(o_w_router[j])
            wr_hi = wr.astype(BF16)
            wr_lo = (wr - wr_hi.astype(F32)).astype(BF16)
            base = j * N_EXPERTS
            xs = _fourier_x(xs, gpre, mx[0], mx[1], wo, gpost, mx[2])
            hb = batch // 2
            xs = jnp.concatenate(
                [_moe_sorted(xs[lo:lo + hb], fpre, mx[3][lo:lo + hb], mx[4][lo:lo + hb], wr_hi, wr_lo, ow1, ow3, ow2,
                             base, fpost, mx[5][lo:lo + hb], min(512, seq), min(1024, seq), 1792)
                 for lo in (0, hb)], axis=0)
            if ctx_full:
                xc = _fourier_ctx(xc, gpre, mc[0], mc[1], wo, gpost, mc[2])
                xc = _moe_sorted(xc, fpre, mc[3], mc[4], wr_hi, wr_lo, ow1, ow3, ow2, base, fpost, mc[5],
                                 n_ctx, n_ctx, 1792)
    return xs
```
